```python
import jax, jax.numpy as jnp
from jax import lax
import numpy as np

D_MODEL = 2048
BATCH = 4
SEQ = 2048
DEPTH = 1
DEC_BATCH = 128
DEC_SEQ = 4
PAST_LEN = 16384
PAGE_SIZE = 128

D_A = D_MODEL // 2
D_B = D_MODEL - D_A
DK = 128
H_A = D_A // DK
DV = D_A // H_A
CONV_W = 3
N_META = 16
CHUNK = 64
D_FF = -(-8 * D_MODEL // (3 * 256)) * 256
ALPHA = (2 * DEPTH) ** 0.25
BETA = (8 * DEPTH) ** -0.25
LN_EPS = 1e-5
RMS_EPS = 1e-6
N_IN = 2 * H_A * DK + 2 * H_A * DV + 3 * D_B

kernel_name = "hgrn2_shortconv_hybrid_step"


def _layernorm(x, g, b):
    xf = x.astype(jnp.float32)
    mu = jnp.mean(xf, axis=-1, keepdims=True)
    var = jnp.mean(jnp.square(xf - mu), axis=-1, keepdims=True)
    return ((xf - mu) * lax.rsqrt(var + LN_EPS) * g.astype(jnp.float32) + b.astype(jnp.float32)).astype(x.dtype)


def _gla_chunked(q, k, v, logf, S0):
    B, L, H, _ = q.shape
    c = min(CHUNK, L)
    n = -(-L // c)
    pad = n * c - L

    def prep(t):
        t = jnp.pad(t, ((0, 0), (0, pad), (0, 0), (0, 0)))
        return t.reshape(B, n, c, H, t.shape[-1]).transpose(1, 0, 3, 2, 4)

    qs, ks, vs, gs = prep(q), prep(k), prep(v), prep(logf)
    causal = jnp.tril(jnp.ones((c, c), dtype=bool))
    mid = c // 2

    def step(S, inp):
        qc, kc, vc, gc = inp
        b = jnp.cumsum(gc, axis=2)
        b_mid = b[:, :, mid:mid + 1]
        b_last = b[:, :, -1:]
        a = jnp.einsum('bhtk,bhsk->bhts', qc * jnp.exp(b - b_mid), kc * jnp.exp(b_mid - b))
        a = jnp.where(causal, a, 0.0)
        o = (jnp.einsum('bhts,bhsv->bhtv', a, vc)
             + jnp.einsum('bhtk,bhkv->bhtv', qc * jnp.exp(b), S))
        S = (jnp.exp(b_last[:, :, 0, :, None]) * S
             + jnp.einsum('bhsk,bhsv->bhkv', kc * jnp.exp(b_last - b), vc))
        return S, o

    S, o = lax.scan(step, S0.astype(jnp.float32), (qs, ks, vs, gs))
    o = o.transpose(1, 0, 3, 2, 4).reshape(B, n * c, H, DV)[:, :L]
    return o, S


def _layer(x, S0, buf, n_lead, lb, w_in, b_f, gnorm_g, conv_w, w_o,
           ln1_g, ln1_b, w_gate, w_up, w_down, ln2_g, ln2_b):
    f32 = jnp.float32
    Bsz, L, _ = x.shape
    proj = x @ w_in
    cuts = np.cumsum([H_A * DK, H_A * DK, H_A * DV, H_A * DV, D_B, D_B]).tolist()
    q, zf, v, g, gate_b, gate_c, xin = jnp.split(proj, cuts, axis=-1)

    zf = zf.astype(f32) + b_f.astype(f32)
    logf = jnp.logaddexp(jnp.log(lb), jnp.log1p(-lb) + jax.nn.log_sigmoid(zf))
    k = (1.0 - lb) * jax.nn.sigmoid(-zf)
    heads = lambda t: t.astype(f32).reshape(Bsz, L, H_A, -1)
    qh, kh, vh, gh = heads(jax.nn.silu(q.astype(f32))), heads(k), heads(v), heads(logf)
    if n_lead:
        o1, S1 = _gla_chunked(qh[:, :n_lead], kh[:, :n_lead], vh[:, :n_lead], gh[:, :n_lead], S0)
        o2, S = _gla_chunked(qh[:, n_lead:], kh[:, n_lead:], vh[:, n_lead:], gh[:, n_lead:], S1)
        o = jnp.concatenate([o1, o2], axis=1)
    else:
        o, S = _gla_chunked(qh, kh, vh, gh, S0)
    o = o * lax.rsqrt(jnp.mean(jnp.square(o), axis=-1, keepdims=True) + RMS_EPS)
    o_a = (o.reshape(Bsz, L, H_A * DV) * gnorm_g.astype(f32) * jax.nn.silu(g.astype(f32))).astype(x.dtype)

    u = gate_c * xin
    full = jnp.concatenate([buf.astype(u.dtype), u], axis=1)
    conv = sum(conv_w[j] * full[:, j:j + L] for j in range(CONV_W))
    y_b = gate_b * conv
    new_buf = full[:, -(CONV_W - 1):]

    mix = jnp.concatenate([o_a, y_b], axis=-1) @ w_o
    h = _layernorm(ALPHA * x + mix, ln1_g, ln1_b)
    ff = (jax.nn.silu(h @ w_gate) * (h @ w_up)) @ w_down
    out = _layernorm(ALPHA * h + ff, ln2_g, ln2_b)
    return out, S, new_buf


def setup_inputs(seed: int = 0) -> dict:
    key = jax.random.key(seed)
    ks = jax.random.split(key, 24)
    nrm = lambda i, shape: jax.random.normal(ks[i], shape, jnp.float32)
    col_scale = jnp.concatenate([
        jnp.ones((2 * H_A * DK,)),
        jnp.full((H_A * DV,), BETA),
        jnp.ones((H_A * DV + 2 * D_B,)),
        jnp.full((D_B,), BETA),
    ]).astype(jnp.float32)
    return {
        "x_prompt": nrm(0, (BATCH, SEQ, D_MODEL)),
        "x_sample": nrm(1, (DEC_BATCH, DEC_SEQ, D_MODEL)),
        "state_hgrn": 0.5 * nrm(2, (DEPTH, DEC_BATCH, H_A, DK, DV)),
        "state_conv": nrm(3, (DEPTH, DEC_BATCH, CONV_W - 1, D_B)),
        "meta_tokens": nrm(4, (N_META, D_MODEL)),
        "ln0_g": 1.0 + 0.02 * nrm(5, (D_MODEL,)),
        "ln0_b": 0.02 * nrm(6, (D_MODEL,)),
        "w_in": nrm(7, (DEPTH, D_MODEL, N_IN)) * D_MODEL ** -0.5 * col_scale,
        "b_f": 0.1 * nrm(8, (DEPTH, H_A * DK)),
        "lb_param": 0.1 * nrm(9, (DEPTH + 1, H_A * DK)),
        "gnorm_g": 1.0 + 0.02 * nrm(10, (DEPTH, H_A * DV)),
        "conv_w": nrm(11, (DEPTH, CONV_W, D_B)) * CONV_W ** -0.5,
        "w_o": nrm(12, (DEPTH, D_MODEL, D_MODEL)) * D_MODEL ** -0.5 * BETA,
        "ln1_g": 1.0 + 0.02 * nrm(13, (DEPTH, D_MODEL)),
        "ln1_b": 0.02 * nrm(14, (DEPTH, D_MODEL)),
        "w_gate": nrm(15, (DEPTH, D_MODEL, D_FF)) * D_MODEL ** -0.5,
        "w_up": nrm(16, (DEPTH, D_MODEL, D_FF)) * D_MODEL ** -0.5,
        "w_down": nrm(17, (DEPTH, D_FF, D_MODEL)) * D_FF ** -0.5 * BETA,
        "ln2_g": 1.0 + 0.02 * nrm(18, (DEPTH, D_MODEL)),
        "ln2_b": 0.02 * nrm(19, (DEPTH, D_MODEL)),
    }


def reference(x_prompt, x_sample, state_hgrn, state_conv, meta_tokens, ln0_g, ln0_b,
              w_in, b_f, lb_param, gnorm_g, conv_w, w_o, ln1_g, ln1_b,
              w_gate, w_up, w_down, ln2_g, ln2_b):
    f32 = jnp.float32
    lbs = jnp.cumsum(jax.nn.softmax(lb_param.astype(f32), axis=0), axis=0)

    meta = jnp.broadcast_to(meta_tokens.astype(x_prompt.dtype)[None], (x_prompt.shape[0], N_META, D_MODEL))
    xp = _layernorm(jnp.concatenate([meta, x_prompt], axis=1), ln0_g, ln0_b)
    xs = _layernorm(x_sample, ln0_g, ln0_b)
    bp = x_prompt.shape[0]

    hp, cp, hs, cs = [], [], [], []
    for l in range(DEPTH):
        w = (lbs[l], w_in[l], b_f[l], gnorm_g[l], conv_w[l], w_o[l],
             ln1_g[l], ln1_b[l], w_gate[l], w_up[l], w_down[l], ln2_g[l], ln2_b[l])
        S0p = jnp.zeros((bp, H_A, DK, DV), f32)
        buf0p = jnp.zeros((bp, CONV_W - 1, D_B), xp.dtype)
        xp, Sp, bufp = _layer(xp, S0p, buf0p, N_META, *w)
        xs, Ss, bufs = _layer(xs, state_hgrn[l], state_conv[l], 0, *w)
        hp.append(Sp); cp.append(bufp); hs.append(Ss); cs.append(bufs)

    y_prompt = xp[:, N_META:]
    y_sample = xs
    return (y_prompt, y_sample, jnp.stack(hp), jnp.stack(cp), jnp.stack(hs), jnp.stack(cs))
```

```python
import functools

import numpy as np
import jax
import jax.numpy as jnp
from jax.experimental import pallas as pl
from jax.experimental.pallas import tpu as pltpu

F32 = jnp.float32
BF16 = jnp.bfloat16

D_MODEL = 2048
D_A = 1024
D_B = 1024
DK = 128
DV = 128
H_A = 8
SEG = 1024
N_SEG = 7
N_META = 16
CHUNK = 64
D_FF = 5632
CONV_W = 3
ALPHA = 2.0 ** 0.25
LN_EPS = 1e-5
RMS_EPS = 1e-6

V7X_VMEM_BYTES = 64 * 1024 * 1024
VMEM_LIMIT = V7X_VMEM_BYTES - 2 * 1024 * 1024


def _layernorm(x, g, b):
    mu = jnp.mean(x, axis=-1, keepdims=True)
    xc = x - mu
    var = jnp.mean(xc * xc, axis=-1, keepdims=True)
    return xc * jax.lax.rsqrt(var + LN_EPS) * g + b


def _silu(x):
    return x * (1.0 / (1.0 + jnp.exp(-x)))


def _inproj_kernel(x_ref, g0_ref, b0_ref, w_ref, bf_ref, lb_ref, main_ref, k_ref, xn_ref):
    j = pl.program_id(1)

    @pl.when(j == 0)
    def _():
        xn_ref[...] = _layernorm(x_ref[...], g0_ref[...], b0_ref[...]).astype(BF16)

    acc = jnp.dot(xn_ref[...], w_ref[...], preferred_element_type=F32)

    @pl.when((j == 0) | (j == 3))
    def _():
        main_ref[...] = _silu(acc)

    @pl.when(j == 1)
    def _():
        z = acc + bf_ref[...]
        lb = lb_ref[...]
        e = jnp.exp(-jnp.abs(z))
        r = 1.0 / (1.0 + e)
        er = e * r
        pos = z >= 0.0
        sig_pos = jnp.where(pos, r, er)
        sig_neg = jnp.where(pos, er, r)
        main_ref[...] = jnp.log(lb + (1.0 - lb) * sig_pos)
        k_ref[...] = (1.0 - lb) * sig_neg

    @pl.when((j == 2) | (j >= 4))
    def _():
        main_ref[...] = acc


def _inproj(x, g0, b0, w_in, b_f, lb, tm):
    rows = x.shape[0]
    assert rows % tm == 0
    row = lambda a: a.reshape(1, -1)
    const = lambda i, j: (0, 0)
    return pl.pallas_call(
        _inproj_kernel,
        grid=(rows // tm, N_SEG),
        in_specs=[
            pl.BlockSpec((tm, D_MODEL), lambda i, j: (i, 0)),
            pl.BlockSpec((1, D_MODEL), const),
            pl.BlockSpec((1, D_MODEL), const),
            pl.BlockSpec((D_MODEL, SEG), lambda i, j: (0, j)),
            pl.BlockSpec((1, SEG), const),
            pl.BlockSpec((1, SEG), const),
        ],
        out_specs=[
            pl.BlockSpec((tm, SEG), lambda i, j: (i, j)),
            pl.BlockSpec((tm, SEG), lambda i, j: (i, 0)),
        ],
        out_shape=[
            jax.ShapeDtypeStruct((rows, N_SEG * SEG), F32),
            jax.ShapeDtypeStruct((rows, SEG), F32),
        ],
        scratch_shapes=[pltpu.VMEM((tm, D_MODEL), BF16)],
        compiler_params=pltpu.CompilerParams(
            dimension_semantics=("arbitrary", "arbitrary"), vmem_limit_bytes=VMEM_LIMIT),
        name="inproj",
    )(x, row(g0), row(b0), w_in, row(b_f), row(lb))


def _split3(x):
    hi = x.astype(BF16)
    r1 = x - hi.astype(F32)
    mid = r1.astype(BF16)
    lo = (r1 - mid.astype(F32)).astype(BF16)
    return jnp.concatenate([hi, mid, lo], axis=0)


def _head_tiles(qt, kk, b, bm, bl):
    q1 = (qt * jnp.exp(b - bm)).astype(BF16)
    k1 = (kk * jnp.exp(bm - b)).astype(BF16)
    q2 = qt * jnp.exp(b)
    k2 = kk * jnp.exp(bl - b)
    return q1, k1, q2, k2


def _gated_rmsnorm(o, gn, sg):
    ms = jnp.mean(o * o, axis=-1, keepdims=True)
    return (o * jax.lax.rsqrt(ms + RMS_EPS) * gn * sg).astype(BF16)


def _mix_seq_kernel(q_ref, lf_ref, v_ref, sg_ref, gb_ref, gc_ref, xi_ref, k_ref,
                    st0_ref, tail0_ref, gn_ref, cw_ref, tmat_ref, amask_ref,
                    out_ref, sfin_ref, cfin_ref, st_scr, ubuf,
                    *, tb, mid, valid_rows, zero_init, state_transposed_out):
    t = pl.program_id(1)

    @pl.when(t == 0)
    def _():
        if zero_init:
            st_scr[...] = jnp.zeros_like(st_scr)
            ubuf[6:8, :] = jnp.zeros((2, D_B), F32)
        else:
            st_scr[...] = st0_ref[...]
            ubuf[6:8, :] = tail0_ref[...]

    u = gc_ref[...] * xi_ref[...]
    ubuf[8:8 + tb, :] = u
    cw = cw_ref[...]
    conv = cw[0:1] * ubuf[6:6 + tb, :] + cw[1:2] * ubuf[7:7 + tb, :] + cw[2:3] * u
    out_ref[:, D_A:] = (gb_ref[...] * conv).astype(BF16)
    last = valid_rows if valid_rows is not None else tb
    ubuf[6:8, :] = ubuf[6 + last:8 + last, :]

    tmat = tmat_ref[...]
    causal = amask_ref[...] > 0.0
    gn = gn_ref[...]
    if valid_rows is not None:
        rowmask = jax.lax.broadcasted_iota(jnp.int32, (CHUNK, SEG), 0) < valid_rows
        rowmask_h = jax.lax.broadcasted_iota(jnp.int32, (CHUNK, DK), 0) < valid_rows
    for ci in range(tb // CHUNK):
        rows = slice(ci * CHUNK, (ci + 1) * CHUNK)
        lf = lf_ref[rows, :]
        if valid_rows is not None:
            lf = jnp.where(rowmask, lf, 0.0)
        b_all = jnp.dot(tmat, _split3(lf), preferred_element_type=F32)
        for h in range(H_A):
            sl = slice(h * DK, (h + 1) * DK)
            b = b_all[:, sl]
            bm = b[mid:mid + 1, :]
            bl = b[CHUNK - 1:CHUNK, :]
            kk = k_ref[rows, sl]
            if valid_rows is not None:
                kk = jnp.where(rowmask_h, kk, 0.0)
            q1, k1, q2, k2 = _head_tiles(q_ref[rows, sl], kk, b, bm, bl)
            vb = v_ref[rows, sl].astype(BF16)
            st = st_scr[h]
            a = pl.dot(q1, k1, trans_b=True)
            a = jnp.where(causal, a, 0.0).astype(BF16)
            o = (jnp.dot(a, vb, preferred_element_type=F32)
                 + pl.dot(q2.astype(BF16), st.astype(BF16), trans_b=True))
            st_scr[h] = st * jnp.exp(bl) + pl.dot(vb, k2.astype(BF16), trans_a=True)
            out_ref[rows, sl] = _gated_rmsnorm(o, gn[:, sl], sg_ref[rows, sl])

    @pl.when(t == pl.num_programs(1) - 1)
    def _():
        for h in range(H_A):
            sfin_ref[0, h] = st_scr[h] if state_transposed_out else st_scr[h].T
        cfin_ref[0] = ubuf[6:8, :]


def _mix_seq(main, kk, row0, nseq, seq_len, tb, st0, tail0, gn, cw, *, mid, valid_rows=None,
             zero_init=False, state_transposed_out=False):
    nt = seq_len // tb
    assert seq_len % tb == 0 and row0 % tb == 0 and tb % CHUNK == 0
    rb0 = row0 // tb
    seg = lambda s: pl.BlockSpec((tb, SEG), lambda b, t, s=s: (rb0 + b * nt + t, s))
    const2 = lambda b, t: (0, 0)
    tril = np.tril(np.ones((CHUNK, CHUNK), np.float32))
    tmat = jnp.asarray(np.concatenate([tril, tril, tril], axis=1), BF16)
    kern = functools.partial(_mix_seq_kernel, tb=tb, mid=mid, valid_rows=valid_rows,
                             zero_init=zero_init, state_transposed_out=state_transposed_out)
    return pl.pallas_call(
        kern,
        grid=(nseq, nt),
        in_specs=[seg(0), seg(1), seg(2), seg(3), seg(4), seg(5), seg(6),
                  pl.BlockSpec((tb, SEG), lambda b, t: (rb0 + b * nt + t, 0)),
                  pl.BlockSpec((H_A, DV, DK), lambda b, t: (0, 0, 0)),
                  pl.BlockSpec((CONV_W - 1, D_B), const2),
                  pl.BlockSpec((1, D_A), const2),
                  pl.BlockSpec((CONV_W, D_B), const2),
                  pl.BlockSpec((CHUNK, 3 * CHUNK), const2),
                  pl.BlockSpec((CHUNK, CHUNK), const2)],
        out_specs=[pl.BlockSpec((tb, D_MODEL), lambda b, t: (b * nt + t, 0)),
                   pl.BlockSpec((1, H_A, DK, DV), lambda b, t: (b, 0, 0, 0)),
                   pl.BlockSpec((1, CONV_W - 1, D_B), lambda b, t: (b, 0, 0))],
        out_shape=[jax.ShapeDtypeStruct((nseq * seq_len, D_MODEL), BF16),
                   jax.ShapeDtypeStruct((nseq, H_A, DK, DV), F32),
                   jax.ShapeDtypeStruct((nseq, CONV_W - 1, D_B), F32)],
        scratch_shapes=[pltpu.VMEM((H_A, DV, DK), F32), pltpu.VMEM((8 + tb, D_B), F32)],
        compiler_params=pltpu.CompilerParams(
            dimension_semantics=("arbitrary", "arbitrary"), vmem_limit_bytes=VMEM_LIMIT),
        name="mix_seq",
    )(main, main, main, main, main, main, main, kk, st0, tail0, gn.reshape(1, -1), cw,
      tmat, jnp.asarray(tril))


GROUP = 16
S_LEN = 4
S_MID = 2


def _mix_group_kernel(q_ref, lf_ref, v_ref, sg_ref, gb_ref, gc_ref, xi_ref, k_ref,
                      s_ref, cbuf_ref, gn_ref, cw_ref, lmat_ref, amask_ref,
                      out_ref, sfin_ref, cfin_ref, full_scr, y_scr):
    ubuf = y_scr
    ubuf[...] = gc_ref[...] * xi_ref[...]
    cw = cw_ref[...]
    for s in range(GROUP):
        full_scr[8 * s:8 * s + 2, :] = cbuf_ref[s]
        full_scr[8 * s + 2:8 * s + 2 + S_LEN, :] = ubuf[S_LEN * s:S_LEN * (s + 1), :]
    for s in range(GROUP):
        f0 = full_scr[8 * s:8 * s + S_LEN, :]
        f1 = full_scr[8 * s + 1:8 * s + 1 + S_LEN, :]
        f2 = full_scr[8 * s + 2:8 * s + 2 + S_LEN, :]
        y_scr[S_LEN * s:S_LEN * (s + 1), :] = cw[0:1] * f0 + cw[1:2] * f1 + cw[2:3] * f2
        cfin_ref[s] = full_scr[8 * s + S_LEN:8 * s + S_LEN + 2, :]
    out_ref[:, D_A:] = (gb_ref[...] * y_scr[...]).astype(BF16)

    b3 = jnp.dot(lmat_ref[...], _split3(lf_ref[...]), preferred_element_type=F32)
    causal = amask_ref[...] > 0.0
    gn = gn_ref[...]
    sub = 16
    per_sub = sub // S_LEN
    rid = jax.lax.broadcasted_iota(jnp.int32, (sub, DK), 0)
    own = [(rid >= S_LEN * j) & (rid < S_LEN * (j + 1)) for j in range(per_sub)]
    for h in range(H_A):
        sl = slice(h * DK, (h + 1) * DK)
        b = b3[0:CHUNK, sl]
        bm = b3[CHUNK:2 * CHUNK, sl]
        bl = b3[2 * CHUNK:3 * CHUNK, sl]
        q1, k1, q2, k2 = _head_tiles(q_ref[:, sl], k_ref[:, sl], b, bm, bl)
        vb = v_ref[:, sl].astype(BF16)
        a = pl.dot(q1, k1, trans_b=True)
        a = jnp.where(causal, a, 0.0).astype(BF16)
        o1 = jnp.dot(a, vb, preferred_element_type=F32)
        decay = jnp.exp(bl)
        for blk in range(CHUNK // sub):
            rs = slice(blk * sub, (blk + 1) * sub)
            q2b, k2b, vbb = q2[rs], k2[rs], vb[rs]
            acc = o1[rs]
            for j in range(per_sub):
                s = blk * per_sub + j
                st = s_ref[s, h].T
                qm = jnp.where(own[j], q2b, 0.0).astype(BF16)
                km = jnp.where(own[j], k2b, 0.0).astype(BF16)
                acc = acc + pl.dot(qm, st.astype(BF16), trans_b=True)
                d = decay[S_LEN * s:S_LEN * s + 1, :]
                sfin_ref[s, h] = (st * d + pl.dot(vbb, km, trans_a=True)).T
            out_ref[rs, sl] = _gated_rmsnorm(acc, gn[:, sl], sg_ref[rs, sl])


def _group_mats():
    r = np.arange(CHUNK)
    seq, pos = r // S_LEN, r % S_LEN
    same = seq[:, None] == seq[None, :]
    cum = same & (pos[None, :] <= pos[:, None])
    midm = same & (pos[None, :] <= S_MID)
    lmat = np.concatenate([cum, midm, same], axis=0).astype(np.float32)
    lmat = np.concatenate([lmat, lmat, lmat], axis=1)
    return jnp.asarray(lmat, BF16), jnp.asarray(cum.astype(np.float32))


def _mix_group(main, kk, nseq, s0, cbuf, gn, cw):
    steps = nseq // GROUP
    seg = lambda s: pl.BlockSpec((CHUNK, SEG), lambda i, s=s: (i, s))
    const2 = lambda i: (0, 0)
    lmat, amask = _group_mats()
    return pl.pallas_call(
        _mix_group_kernel,
        grid=(steps,),
        in_specs=[seg(0), seg(1), seg(2), seg(3), seg(4), seg(5), seg(6),
                  pl.BlockSpec((CHUNK, SEG), lambda i: (i, 0)),
                  pl.BlockSpec((GROUP, H_A, DK, DV), lambda i: (i, 0, 0, 0)),
                  pl.BlockSpec((GROUP, CONV_W - 1, D_B), lambda i: (i, 0, 0)),
                  pl.BlockSpec((1, D_A), const2),
                  pl.BlockSpec((CONV_W, D_B), const2),
                  pl.BlockSpec((3 * CHUNK, 3 * CHUNK), const2),
                  pl.BlockSpec((CHUNK, CHUNK), const2)],
        out_specs=[pl.BlockSpec((CHUNK, D_MODEL), lambda i: (i, 0)),
                   pl.BlockSpec((GROUP, H_A, DK, DV), lambda i: (i, 0, 0, 0)),
                   pl.BlockSpec((GROUP, CONV_W - 1, D_B), lambda i: (i, 0, 0))],
        out_shape=[jax.ShapeDtypeStruct((nseq * S_LEN, D_MODEL), BF16),
                   jax.ShapeDtypeStruct((nseq, H_A, DK, DV), F32),
                   jax.ShapeDtypeStruct((nseq, CONV_W - 1, D_B), F32)],
        scratch_shapes=[pltpu.VMEM((8 * GROUP, D_B), F32), pltpu.VMEM((CHUNK, D_B), F32)],
        compiler_params=pltpu.CompilerParams(
            dimension_semantics=("arbitrary",), vmem_limit_bytes=VMEM_LIMIT),
        name="mix_group",
    )(main, main, main, main, main, main, main, kk, s0, cbuf, gn.reshape(1, -1), cw,
      lmat, amask)


def _outproj_kernel(x_ref, mix_ref, wo_ref, g0_ref, b0_ref, g1_ref, b1_ref, h_ref):
    xn = _layernorm(x_ref[...], g0_ref[...], b0_ref[...])
    m = jnp.dot(mix_ref[...], wo_ref[...], preferred_element_type=F32)
    h_ref[...] = _layernorm(ALPHA * xn + m, g1_ref[...], b1_ref[...])


def _outproj(x, mix, w_o, g0, b0, g1, b1, tm):
    rows = x.shape[0]
    assert rows % tm == 0
    row = lambda a: a.reshape(1, -1)
    const = lambda i: (0, 0)
    vec = pl.BlockSpec((1, D_MODEL), const)
    return pl.pallas_call(
        _outproj_kernel,
        grid=(rows // tm,),
        in_specs=[pl.BlockSpec((tm, D_MODEL), lambda i: (i, 0)),
                  pl.BlockSpec((tm, D_MODEL), lambda i: (i, 0)),
                  pl.BlockSpec((D_MODEL, D_MODEL), const),
                  vec, vec, vec, vec],
        out_specs=pl.BlockSpec((tm, D_MODEL), lambda i: (i, 0)),
        out_shape=jax.ShapeDtypeStruct((rows, D_MODEL), F32),
        compiler_params=pltpu.CompilerParams(
            dimension_semantics=("arbitrary",), vmem_limit_bytes=VMEM_LIMIT),
        name="outproj",
    )(x, mix, w_o, row(g0), row(b0), row(g1), row(b1))


def _ffn_kernel(h_ref, wg_ref, wu_ref, wd_ref, g2_ref, b2_ref, o_ref, hb_ref):
    f = pl.program_id(1)

    @pl.when(f == 0)
    def _():
        h = h_ref[...]
        hb_ref[...] = h.astype(BF16)
        o_ref[...] = ALPHA * h

    hb = hb_ref[...]
    g = jnp.dot(hb, wg_ref[...], preferred_element_type=F32)
    u = jnp.dot(hb, wu_ref[...], preferred_element_type=F32)
    a = (_silu(g) * u).astype(BF16)
    o_ref[...] += jnp.dot(a, wd_ref[...], preferred_element_type=F32)

    @pl.when(f == pl.num_programs(1) - 1)
    def _():
        o_ref[...] = _layernorm(o_ref[...], g2_ref[...], b2_ref[...])


def _ffn(h, w_gate, w_up, w_down, g2, b2, tm, tf):
    rows = h.shape[0]
    assert rows % tm == 0 and D_FF % tf == 0
    row = lambda a: a.reshape(1, -1)
    vec = pl.BlockSpec((1, D_MODEL), lambda i, f: (0, 0))
    return pl.pallas_call(
        _ffn_kernel,
        grid=(rows // tm, D_FF // tf),
        in_specs=[pl.BlockSpec((tm, D_MODEL), lambda i, f: (i, 0)),
                  pl.BlockSpec((D_MODEL, tf), lambda i, f: (0, f)),
                  pl.BlockSpec((D_MODEL, tf), lambda i, f: (0, f)),
                  pl.BlockSpec((tf, D_MODEL), lambda i, f: (f, 0)),
                  vec, vec],
        out_specs=pl.BlockSpec((tm, D_MODEL), lambda i, f: (i, 0)),
        out_shape=jax.ShapeDtypeStruct((rows, D_MODEL), F32),
        scratch_shapes=[pltpu.VMEM((tm, D_MODEL), BF16)],
        compiler_params=pltpu.CompilerParams(
            dimension_semantics=("arbitrary", "arbitrary"), vmem_limit_bytes=VMEM_LIMIT),
        name="ffn",
    )(h, w_gate, w_up, w_down, row(g2), row(b2))


def kernel(x_prompt, x_sample, state_hgrn, state_conv, meta_tokens, ln0_g, ln0_b, w_in, b_f, lb_param, gnorm_g, conv_w, w_o, ln1_g, ln1_b, w_gate, w_up, w_down, ln2_g, ln2_b):
    bp, seq, _ = x_prompt.shape
    bs, dseq, _ = x_sample.shape
    assert dseq == S_LEN and seq % CHUNK == 0 and bs % GROUP == 0

    lb = jnp.cumsum(jax.nn.softmax(lb_param.astype(F32), axis=0), axis=0)[0]
    w_in_b, w_o_b = w_in[0].astype(BF16), w_o[0].astype(BF16)
    wg_b, wu_b, wd_b = w_gate[0].astype(BF16), w_up[0].astype(BF16), w_down[0].astype(BF16)
    g0, b0 = ln0_g.astype(F32), ln0_b.astype(F32)

    xp = x_prompt.reshape(bp * seq, D_MODEL)
    xs = x_sample.reshape(bs * dseq, D_MODEL)
    xs_ext = jnp.concatenate(
        [xs, meta_tokens.astype(F32), jnp.zeros((CHUNK - N_META, D_MODEL), F32)], axis=0)

    main_p, k_p = _inproj(xp, g0, b0, w_in_b, b_f[0], lb, tm=1024)
    main_s, k_s = _inproj(xs_ext, g0, b0, w_in_b, b_f[0], lb, tm=xs_ext.shape[0])

    zero_st = jnp.zeros((H_A, DV, DK), F32)
    zero_tail = jnp.zeros((CONV_W - 1, D_B), F32)
    _, st_meta, tail_meta = _mix_seq(
        main_s, k_s, bs * dseq, 1, CHUNK, CHUNK, zero_st, zero_tail, gnorm_g[0], conv_w[0],
        mid=N_META // 2, valid_rows=N_META, zero_init=True, state_transposed_out=True)

    mix_p, hgrn_p, conv_p = _mix_seq(
        main_p, k_p, 0, bp, seq, 256, st_meta[0], tail_meta[0], gnorm_g[0], conv_w[0],
        mid=CHUNK // 2)
    mix_s, hgrn_s, conv_s = _mix_group(
        main_s, k_s, bs, state_hgrn[0], state_conv[0], gnorm_g[0], conv_w[0])

    h_p = _outproj(xp, mix_p, w_o_b, g0, b0, ln1_g[0], ln1_b[0], tm=512)
    h_s = _outproj(xs, mix_s, w_o_b, g0, b0, ln1_g[0], ln1_b[0], tm=512)

    y_p = _ffn(h_p, wg_b, wu_b, wd_b, ln2_g[0], ln2_b[0], tm=1024, tf=512)
    y_s = _ffn(h_s, wg_b, wu_b, wd_b, ln2_g[0], ln2_b[0], tm=512, tf=512)

    return (y_p.reshape(bp, seq, D_MODEL), y_s.reshape(bs, dseq, D_MODEL),
            hgrn_p[None], conv_p[None], hgrn_s[None], conv_s[None])
```

```python
import functools

import numpy as np
import jax
import jax.numpy as jnp
from jax.experimental import pallas as pl
from jax.experimental.pallas import tpu as pltpu

F32 = jnp.float32
BF16 = jnp.bfloat16

D_MODEL = 2048
D_A = 1024
D_B = 1024
DK = 128
DV = 128
H_A = 8
SEG = 1024
N_SEG = 7
N_META = 16
CHUNK = 64
D_FF = 5632
CONV_W = 3
ALPHA = 2.0 ** 0.25
LN_EPS = 1e-5
RMS_EPS = 1e-6

V7X_VMEM_BYTES = 64 * 1024 * 1024
VMEM_LIMIT = V7X_VMEM_BYTES - 2 * 1024 * 1024


def _layernorm(x, g, b):
    mu = jnp.mean(x, axis=-1, keepdims=True)
    xc = x - mu
    var = jnp.mean(xc * xc, axis=-1, keepdims=True)
    return xc * jax.lax.rsqrt(var + LN_EPS) * g + b


def _silu(x):
    return x * (1.0 / (1.0 + jnp.exp(-x)))


def _inproj_kernel(x_ref, g0_ref, b0_ref, w_ref, bf_ref, lb_ref,
                   q_ref, lf_ref, k_ref, v_ref, sg_ref, gb_ref, u_ref):
    xn = _layernorm(x_ref[...], g0_ref[...], b0_ref[...]).astype(BF16)

    def seg(j):
        return jnp.dot(xn, w_ref[:, j * SEG:(j + 1) * SEG], preferred_element_type=F32)

    q_ref[...] = _silu(seg(0)).astype(BF16)

    z = seg(1) + bf_ref[...]
    lb = lb_ref[...]
    e = jnp.exp(-jnp.abs(z))
    r = 1.0 / (1.0 + e)
    er = e * r
    pos = z >= 0.0
    lf_ref[...] = jnp.log(lb + (1.0 - lb) * jnp.where(pos, r, er))
    k_ref[...] = ((1.0 - lb) * jnp.where(pos, er, r)).astype(BF16)

    v_ref[...] = seg(2).astype(BF16)
    sg_ref[...] = _silu(seg(3)).astype(BF16)
    gb_ref[...] = seg(4).astype(BF16)
    u_ref[...] = seg(5) * seg(6)


def _inproj(x, g0, b0, w_in, b_f, lb, tm):
    rows = x.shape[0]
    assert rows % tm == 0
    row = lambda a: a.reshape(1, -1)
    const = lambda i: (0, 0)
    out = lambda: pl.BlockSpec((tm, SEG), lambda i: (i, 0))
    sds = lambda dt: jax.ShapeDtypeStruct((rows, SEG), dt)
    return pl.pallas_call(
        _inproj_kernel,
        grid=(rows // tm,),
        in_specs=[
            pl.BlockSpec((tm, D_MODEL), lambda i: (i, 0)),
            pl.BlockSpec((1, D_MODEL), const),
            pl.BlockSpec((1, D_MODEL), const),
            pl.BlockSpec((D_MODEL, N_SEG * SEG), const, pipeline_mode=pl.Buffered(1)),
            pl.BlockSpec((1, SEG), const),
            pl.BlockSpec((1, SEG), const),
        ],
        out_specs=[out() for _ in range(7)],
        out_shape=[sds(BF16), sds(F32), sds(BF16), sds(BF16), sds(BF16), sds(BF16), sds(F32)],
        compiler_params=pltpu.CompilerParams(
            dimension_semantics=("arbitrary",), vmem_limit_bytes=VMEM_LIMIT),
        name="inproj",
    )(x, row(g0), row(b0), w_in, row(b_f), row(lb))


def _split3(x):
    hi = x.astype(BF16)
    r1 = x - hi.astype(F32)
    mid = r1.astype(BF16)
    lo = (r1 - mid.astype(F32)).astype(BF16)
    return jnp.concatenate([hi, mid, lo], axis=0)


def _head_tiles(qt, kk, b, bm, bl):
    q1 = (qt * jnp.exp(b - bm)).astype(BF16)
    k1 = (kk * jnp.exp(bm - b)).astype(BF16)
    q2 = qt * jnp.exp(b)
    k2 = kk * jnp.exp(bl - b)
    return q1, k1, q2, k2


def _gated_rmsnorm(o, gn, sg):
    ms = jnp.mean(o * o, axis=-1, keepdims=True)
    return (o * jax.lax.rsqrt(ms + RMS_EPS) * gn * sg).astype(BF16)


def _mix_seq_kernel(q_ref, lf_ref, k_ref, v_ref, sg_ref, gb_ref, u_ref,
                    st0_ref, tail0_ref, gn_ref, cw_ref, tmat_ref, amask_ref,
                    out_ref, sfin_ref, cfin_ref, st_scr, ubuf,
                    *, tb, mid, valid_rows, zero_init, state_transposed_out):
    t = pl.program_id(1)

    @pl.when(t == 0)
    def _():
        if zero_init:
            st_scr[...] = jnp.zeros_like(st_scr)
            ubuf[6:8, :] = jnp.zeros((2, D_B), F32)
        else:
            st_scr[...] = st0_ref[...]
            ubuf[6:8, :] = tail0_ref[...]

    u = u_ref[...]
    ubuf[8:8 + tb, :] = u
    cw = cw_ref[...]
    conv = cw[0:1] * ubuf[6:6 + tb, :] + cw[1:2] * ubuf[7:7 + tb, :] + cw[2:3] * u
    out_ref[:, D_A:] = (gb_ref[...].astype(F32) * conv).astype(BF16)
    last = valid_rows if valid_rows is not None else tb
    ubuf[6:8, :] = ubuf[6 + last:8 + last, :]

    tmat = tmat_ref[...]
    causal = amask_ref[...] > 0.0
    gn = gn_ref[...]
    if valid_rows is not None:
        rowmask = jax.lax.broadcasted_iota(jnp.int32, (CHUNK, SEG), 0) < valid_rows
        rowmask_h = jax.lax.broadcasted_iota(jnp.int32, (CHUNK, DK), 0) < valid_rows
    for ci in range(tb // CHUNK):
        rows = slice(ci * CHUNK, (ci + 1) * CHUNK)
        lf = lf_ref[rows, :]
        if valid_rows is not None:
            lf = jnp.where(rowmask, lf, 0.0)
        b_all = jnp.dot(tmat, _split3(lf), preferred_element_type=F32)
        for h in range(H_A):
            sl = slice(h * DK, (h + 1) * DK)
            b = b_all[:, sl]
            bm = b[mid:mid + 1, :]
            bl = b[CHUNK - 1:CHUNK, :]
            kk = k_ref[rows, sl].astype(F32)
            if valid_rows is not None:
                kk = jnp.where(rowmask_h, kk, 0.0)
            q1, k1, q2, k2 = _head_tiles(q_ref[rows, sl].astype(F32), kk, b, bm, bl)
            vb = v_ref[rows, sl]
            st = st_scr[h]
            a = pl.dot(q1, k1, trans_b=True)
            a = jnp.where(causal, a, 0.0).astype(BF16)
            o = (jnp.dot(a, vb, preferred_element_type=F32)
                 + pl.dot(q2.astype(BF16), st.astype(BF16), trans_b=True))
            st_scr[h] = st * jnp.exp(bl) + pl.dot(vb, k2.astype(BF16), trans_a=True)
            out_ref[rows, sl] = _gated_rmsnorm(o, gn[:, sl], sg_ref[rows, sl].astype(F32))

    @pl.when(t == pl.num_programs(1) - 1)
    def _():
        for h in range(H_A):
            sfin_ref[0, h] = st_scr[h] if state_transposed_out else st_scr[h].T
        cfin_ref[0] = ubuf[6:8, :]


def _mix_seq(proj, row0, nseq, seq_len, tb, st0, tail0, gn, cw, *, mid, valid_rows=None,
             zero_init=False, state_transposed_out=False):
    nt = seq_len // tb
    assert seq_len % tb == 0 and row0 % tb == 0 and tb % CHUNK == 0
    rb0 = row0 // tb
    seg = lambda: pl.BlockSpec((tb, SEG), lambda b, t: (rb0 + b * nt + t, 0))
    const2 = lambda b, t: (0, 0)
    tril = np.tril(np.ones((CHUNK, CHUNK), np.float32))
    tmat = jnp.asarray(np.concatenate([tril, tril, tril], axis=1), BF16)
    kern = functools.partial(_mix_seq_kernel, tb=tb, mid=mid, valid_rows=valid_rows,
                             zero_init=zero_init, state_transposed_out=state_transposed_out)
    return pl.pallas_call(
        kern,
        grid=(nseq, nt),
        in_specs=[seg() for _ in range(7)] + [
                  pl.BlockSpec((H_A, DV, DK), lambda b, t: (0, 0, 0)),
                  pl.BlockSpec((CONV_W - 1, D_B), const2),
                  pl.BlockSpec((1, D_A), const2),
                  pl.BlockSpec((CONV_W, D_B), const2),
                  pl.BlockSpec((CHUNK, 3 * CHUNK), const2),
                  pl.BlockSpec((CHUNK, CHUNK), const2)],
        out_specs=[pl.BlockSpec((tb, D_MODEL), lambda b, t: (b * nt + t, 0)),
                   pl.BlockSpec((1, H_A, DK, DV), lambda b, t: (b, 0, 0, 0)),
                   pl.BlockSpec((1, CONV_W - 1, D_B), lambda b, t: (b, 0, 0))],
        out_shape=[jax.ShapeDtypeStruct((nseq * seq_len, D_MODEL), BF16),
                   jax.ShapeDtypeStruct((nseq, H_A, DK, DV), F32),
                   jax.ShapeDtypeStruct((nseq, CONV_W - 1, D_B), F32)],
        scratch_shapes=[pltpu.VMEM((H_A, DV, DK), F32), pltpu.VMEM((8 + tb, D_B), F32)],
        compiler_params=pltpu.CompilerParams(
            dimension_semantics=("arbitrary", "arbitrary"), vmem_limit_bytes=VMEM_LIMIT),
        name="mix_seq",
    )(*proj, st0, tail0, gn.reshape(1, -1), cw, tmat, jnp.asarray(tril))


GROUP = 16
S_LEN = 4
S_MID = 2


def _mix_group_kernel(q_ref, lf_ref, k_ref, v_ref, sg_ref, gb_ref, u_ref,
                      s_ref, cbuf_ref, gn_ref, cw_ref, lmat_ref, amask_ref,
                      out_ref, sfin_ref, cfin_ref, full_scr, y_scr):
    cw = cw_ref[...]
    for s in range(GROUP):
        full_scr[8 * s:8 * s + 2, :] = cbuf_ref[s]
        full_scr[8 * s + 2:8 * s + 2 + S_LEN, :] = u_ref[S_LEN * s:S_LEN * (s + 1), :]
    for s in range(GROUP):
        f0 = full_scr[8 * s:8 * s + S_LEN, :]
        f1 = full_scr[8 * s + 1:8 * s + 1 + S_LEN, :]
        f2 = full_scr[8 * s + 2:8 * s + 2 + S_LEN, :]
        y_scr[S_LEN * s:S_LEN * (s + 1), :] = cw[0:1] * f0 + cw[1:2] * f1 + cw[2:3] * f2
        cfin_ref[s] = full_scr[8 * s + S_LEN:8 * s + S_LEN + 2, :]
    out_ref[:, D_A:] = (gb_ref[...].astype(F32) * y_scr[...]).astype(BF16)

    b3 = jnp.dot(lmat_ref[...], _split3(lf_ref[...]), preferred_element_type=F32)
    causal = amask_ref[...] > 0.0
    gn = gn_ref[...]
    sub = 16
    per_sub = sub // S_LEN
    rid = jax.lax.broadcasted_iota(jnp.int32, (sub, DK), 0)
    own = [(rid >= S_LEN * j) & (rid < S_LEN * (j + 1)) for j in range(per_sub)]
    for h in range(H_A):
        sl = slice(h * DK, (h + 1) * DK)
        b = b3[0:CHUNK, sl]
        bm = b3[CHUNK:2 * CHUNK, sl]
        bl = b3[2 * CHUNK:3 * CHUNK, sl]
        q1, k1, q2, k2 = _head_tiles(q_ref[:, sl].astype(F32), k_ref[:, sl].astype(F32),
                                     b, bm, bl)
        vb = v_ref[:, sl]
        a = pl.dot(q1, k1, trans_b=True)
        a = jnp.where(causal, a, 0.0).astype(BF16)
        o1 = jnp.dot(a, vb, preferred_element_type=F32)
        decay = jnp.exp(bl)
        for blk in range(CHUNK // sub):
            rs = slice(blk * sub, (blk + 1) * sub)
            q2b, k2b, vbb = q2[rs], k2[rs], vb[rs]
            acc = o1[rs]
            for j in range(per_sub):
                s = blk * per_sub + j
                st = s_ref[s, h].T
                qm = jnp.where(own[j], q2b, 0.0).astype(BF16)
                km = jnp.where(own[j], k2b, 0.0).astype(BF16)
                acc = acc + pl.dot(qm, st.astype(BF16), trans_b=True)
                d = decay[S_LEN * s:S_LEN * s + 1, :]
                sfin_ref[s, h] = (st * d + pl.dot(vbb, km, trans_a=True)).T
            out_ref[rs, sl] = _gated_rmsnorm(acc, gn[:, sl], sg_ref[rs, sl].astype(F32))


def _group_mats():
    r = np.arange(CHUNK)
    seq, pos = r // S_LEN, r % S_LEN
    same = seq[:, None] == seq[None, :]
    cum = same & (pos[None, :] <= pos[:, None])
    midm = same & (pos[None, :] <= S_MID)
    lmat = np.concatenate([cum, midm, same], axis=0).astype(np.float32)
    lmat = np.concatenate([lmat, lmat, lmat], axis=1)
    return jnp.asarray(lmat, BF16), jnp.asarray(cum.astype(np.float32))


def _mix_group(proj, nseq, s0, cbuf, gn, cw):
    steps = nseq // GROUP
    seg = lambda: pl.BlockSpec((CHUNK, SEG), lambda i: (i, 0))
    const2 = lambda i: (0, 0)
    lmat, amask = _group_mats()
    return pl.pallas_call(
        _mix_group_kernel,
        grid=(steps,),
        in_specs=[seg() for _ in range(7)] + [
                  pl.BlockSpec((GROUP, H_A, DK, DV), lambda i: (i, 0, 0, 0)),
                  pl.BlockSpec((GROUP, CONV_W - 1, D_B), lambda i: (i, 0, 0)),
                  pl.BlockSpec((1, D_A), const2),
                  pl.BlockSpec((CONV_W, D_B), const2),
                  pl.BlockSpec((3 * CHUNK, 3 * CHUNK), const2),
                  pl.BlockSpec((CHUNK, CHUNK), const2)],
        out_specs=[pl.BlockSpec((CHUNK, D_MODEL), lambda i: (i, 0)),
                   pl.BlockSpec((GROUP, H_A, DK, DV), lambda i: (i, 0, 0, 0)),
                   pl.BlockSpec((GROUP, CONV_W - 1, D_B), lambda i: (i, 0, 0))],
        out_shape=[jax.ShapeDtypeStruct((nseq * S_LEN, D_MODEL), BF16),
                   jax.ShapeDtypeStruct((nseq, H_A, DK, DV), F32),
                   jax.ShapeDtypeStruct((nseq, CONV_W - 1, D_B), F32)],
        scratch_shapes=[pltpu.VMEM((8 * GROUP, D_B), F32), pltpu.VMEM((CHUNK, D_B), F32)],
        compiler_params=pltpu.CompilerParams(
            dimension_semantics=("arbitrary",), vmem_limit_bytes=VMEM_LIMIT),
        name="mix_group",
    )(*proj, s0, cbuf, gn.reshape(1, -1), cw, lmat, amask)


def _outproj_kernel(x_ref, mix_ref, wo_ref, g0_ref, b0_ref, g1_ref, b1_ref, h_ref):
    xn = _layernorm(x_ref[...], g0_ref[...], b0_ref[...])
    m = jnp.dot(mix_ref[...], wo_ref[...], preferred_element_type=F32)
    h_ref[...] = _layernorm(ALPHA * xn + m, g1_ref[...], b1_ref[...])


def _outproj(x, mix, w_o, g0, b0, g1, b1, tm):
    rows = x.shape[0]
    assert rows % tm == 0
    row = lambda a: a.reshape(1, -1)
    const = lambda i: (0, 0)
    vec = pl.BlockSpec((1, D_MODEL), const)
    return pl.pallas_call(
        _outproj_kernel,
        grid=(rows // tm,),
        in_specs=[pl.BlockSpec((tm, D_MODEL), lambda i: (i, 0)),
                  pl.BlockSpec((tm, D_MODEL), lambda i: (i, 0)),
                  pl.BlockSpec((D_MODEL, D_MODEL), const),
                  vec, vec, vec, vec],
        out_specs=pl.BlockSpec((tm, D_MODEL), lambda i: (i, 0)),
        out_shape=jax.ShapeDtypeStruct((rows, D_MODEL), F32),
        compiler_params=pltpu.CompilerParams(
            dimension_semantics=("arbitrary",), vmem_limit_bytes=VMEM_LIMIT),
        name="outproj",
    )(x, mix, w_o, row(g0), row(b0), row(g1), row(b1))


def _ffn_kernel(h_ref, wg_ref, wu_ref, wd_ref, g2_ref, b2_ref, o_ref, hb_ref):
    f = pl.program_id(1)

    @pl.when(f == 0)
    def _():
        h = h_ref[...]
        hb_ref[...] = h.astype(BF16)
        o_ref[...] = ALPHA * h

    hb = hb_ref[...]
    g = jnp.dot(hb, wg_ref[...], preferred_element_type=F32)
    u = jnp.dot(hb, wu_ref[...], preferred_element_type=F32)
    a = (_silu(g) * u).astype(BF16)
    o_ref[...] += jnp.dot(a, wd_ref[...], preferred_element_type=F32)

    @pl.when(f == pl.num_programs(1) - 1)
    def _():
        o_ref[...] = _layernorm(o_ref[...], g2_ref[...], b2_ref[...])


def _ffn(h, w_gate, w_up, w_down, g2, b2, tm, tf):
    rows = h.shape[0]
    assert rows % tm == 0 and D_FF % tf == 0
    row = lambda a: a.reshape(1, -1)
    vec = pl.BlockSpec((1, D_MODEL), lambda i, f: (0, 0))
    return pl.pallas_call(
        _ffn_kernel,
        grid=(rows // tm, D_FF // tf),
        in_specs=[pl.BlockSpec((tm, D_MODEL), lambda i, f: (i, 0)),
                  pl.BlockSpec((D_MODEL, tf), lambda i, f: (0, f)),
                  pl.BlockSpec((D_MODEL, tf), lambda i, f: (0, f)),
                  pl.BlockSpec((tf, D_MODEL), lambda i, f: (f, 0)),
                  vec, vec],
        out_specs=pl.BlockSpec((tm, D_MODEL), lambda i, f: (i, 0)),
        out_shape=jax.ShapeDtypeStruct((rows, D_MODEL), F32),
        scratch_shapes=[pltpu.VMEM((tm, D_MODEL), BF16)],
        compiler_params=pltpu.CompilerParams(
            dimension_semantics=("arbitrary", "arbitrary"), vmem_limit_bytes=VMEM_LIMIT),
        name="ffn",
    )(h, w_gate, w_up, w_down, row(g2), row(b2))


def kernel(x_prompt, x_sample, state_hgrn, state_conv, meta_tokens, ln0_g, ln0_b, w_in, b_f, lb_param, gnorm_g, conv_w, w_o, ln1_g, ln1_b, w_gate, w_up, w_down, ln2_g, ln2_b):
    bp, seq, _ = x_prompt.shape
    bs, dseq, _ = x_sample.shape
    assert dseq == S_LEN and seq % CHUNK == 0 and bs % GROUP == 0

    lb = jnp.cumsum(jax.nn.softmax(lb_param.astype(F32), axis=0), axis=0)[0]
    w_in_b, w_o_b = w_in[0].astype(BF16), w_o[0].astype(BF16)
    wg_b, wu_b, wd_b = w_gate[0].astype(BF16), w_up[0].astype(BF16), w_down[0].astype(BF16)
    g0, b0 = ln0_g.astype(F32), ln0_b.astype(F32)

    xp = x_prompt.reshape(bp * seq, D_MODEL)
    xs = x_sample.reshape(bs * dseq, D_MODEL)
    xs_ext = jnp.concatenate(
        [xs, meta_tokens.astype(F32), jnp.zeros((CHUNK - N_META, D_MODEL), F32)], axis=0)

    proj_p = _inproj(xp, g0, b0, w_in_b, b_f[0], lb, tm=256)
    proj_s = _inproj(xs_ext, g0, b0, w_in_b, b_f[0], lb, tm=xs_ext.shape[0] // 3)

    zero_st = jnp.zeros((H_A, DV, DK), F32)
    zero_tail = jnp.zeros((CONV_W - 1, D_B), F32)
    _, st_meta, tail_meta = _mix_seq(
        proj_s, bs * dseq, 1, CHUNK, CHUNK, zero_st, zero_tail, gnorm_g[0], conv_w[0],
        mid=N_META // 2, valid_rows=N_META, zero_init=True, state_transposed_out=True)

    mix_p, hgrn_p, conv_p = _mix_seq(
        proj_p, 0, bp, seq, 256, st_meta[0], tail_meta[0], gnorm_g[0], conv_w[0],
        mid=CHUNK // 2)
    mix_s, hgrn_s, conv_s = _mix_group(
        proj_s, bs, state_hgrn[0], state_conv[0], gnorm_g[0], conv_w[0])

    h_p = _outproj(xp, mix_p, w_o_b, g0, b0, ln1_g[0], ln1_b[0], tm=512)
    h_s = _outproj(xs, mix_s, w_o_b, g0, b0, ln1_g[0], ln1_b[0], tm=512)

    y_p = _ffn(h_p, wg_b, wu_b, wd_b, ln2_g[0], ln2_b[0], tm=1024, tf=512)
    y_s = _ffn(h_s, wg_b, wu_b, wd_b, ln2_g[0], ln2_b[0], tm=512, tf=512)

    return (y_p.reshape(bp, seq, D_MODEL), y_s.reshape(bs, dseq, D_MODEL),
            hgrn_p[None], conv_p[None], hgrn_s[None], conv_s[None])
```

```python
import functools

import numpy as np
import jax
import jax.numpy as jnp
from jax.experimental import pallas as pl
from jax.experimental.pallas import tpu as pltpu

F32 = jnp.float32
BF16 = jnp.bfloat16

D_MODEL = 2048
D_A = 1024
D_B = 1024
DK = 128
DV = 128
H_A = 8
SEG = 1024
N_SEG = 7
N_META = 16
CHUNK = 64
D_FF = 5632
CONV_W = 3
ALPHA = 2.0 ** 0.25
LN_EPS = 1e-5
RMS_EPS = 1e-6

V7X_VMEM_BYTES = 64 * 1024 * 1024
VMEM_LIMIT = V7X_VMEM_BYTES - 2 * 1024 * 1024


def _layernorm(x, g, b):
    mu = jnp.mean(x, axis=-1, keepdims=True)
    xc = x - mu
    var = jnp.mean(xc * xc, axis=-1, keepdims=True)
    return xc * jax.lax.rsqrt(var + LN_EPS) * g + b


def _silu(x):
    return x * (1.0 / (1.0 + jnp.exp(-x)))


def _inproj_kernel(x_ref, g0_ref, b0_ref, w_ref, bf_ref, lb_ref,
                   q_ref, lf_ref, k_ref, v_ref, sg_ref, gb_ref, u_ref):
    xn = _layernorm(x_ref[...], g0_ref[...], b0_ref[...]).astype(BF16)

    def seg(j):
        return jnp.dot(xn, w_ref[:, j * SEG:(j + 1) * SEG], preferred_element_type=F32)

    q_ref[...] = _silu(seg(0)).astype(BF16)

    z = seg(1) + bf_ref[...]
    lb = lb_ref[...]
    e = jnp.exp(-jnp.abs(z))
    r = 1.0 / (1.0 + e)
    er = e * r
    pos = z >= 0.0
    lf_ref[...] = jnp.log(lb + (1.0 - lb) * jnp.where(pos, r, er))
    k_ref[...] = ((1.0 - lb) * jnp.where(pos, er, r)).astype(BF16)

    v_ref[...] = seg(2).astype(BF16)
    sg_ref[...] = _silu(seg(3)).astype(BF16)
    gb_ref[...] = seg(4).astype(BF16)
    u_ref[...] = seg(5) * seg(6)


def _inproj(x, g0, b0, w_in, b_f, lb, tm):
    rows = x.shape[0]
    assert rows % tm == 0
    row = lambda a: a.reshape(1, -1)
    const = lambda i: (0, 0)
    out = lambda: pl.BlockSpec((tm, SEG), lambda i: (i, 0))
    sds = lambda dt: jax.ShapeDtypeStruct((rows, SEG), dt)
    return pl.pallas_call(
        _inproj_kernel,
        grid=(rows // tm,),
        in_specs=[
            pl.BlockSpec((tm, D_MODEL), lambda i: (i, 0)),
            pl.BlockSpec((1, D_MODEL), const),
            pl.BlockSpec((1, D_MODEL), const),
            pl.BlockSpec((D_MODEL, N_SEG * SEG), const, pipeline_mode=pl.Buffered(1)),
            pl.BlockSpec((1, SEG), const),
            pl.BlockSpec((1, SEG), const),
        ],
        out_specs=[out() for _ in range(7)],
        out_shape=[sds(BF16), sds(F32), sds(BF16), sds(BF16), sds(BF16), sds(BF16), sds(F32)],
        compiler_params=pltpu.CompilerParams(
            dimension_semantics=("arbitrary",), vmem_limit_bytes=VMEM_LIMIT),
        name="inproj",
    )(x, row(g0), row(b0), w_in, row(b_f), row(lb))


def _split3(x):
    hi = x.astype(BF16)
    r1 = x - hi.astype(F32)
    mid = r1.astype(BF16)
    lo = (r1 - mid.astype(F32)).astype(BF16)
    return jnp.concatenate([hi, mid, lo], axis=0)


def _head_tiles(qt, kk, b, bm, bl):
    q1 = (qt * jnp.exp(b - bm)).astype(BF16)
    k1 = (kk * jnp.exp(bm - b)).astype(BF16)
    q2 = qt * jnp.exp(b)
    k2 = kk * jnp.exp(bl - b)
    return q1, k1, q2, k2


def _gated_rmsnorm(o, gn, sg):
    ms = jnp.mean(o * o, axis=-1, keepdims=True)
    return (o * jax.lax.rsqrt(ms + RMS_EPS) * gn * sg).astype(BF16)


def _mix_seq_kernel(q_ref, lf_ref, k_ref, v_ref, sg_ref, gb_ref, u_ref,
                    st0_ref, tail0_ref, gn_ref, cw_ref, tmat_ref, amask_ref, *refs,
                    tb, mid, valid_rows, zero_init, state_transposed_out, n_cast):
    cast_in, refs = refs[:n_cast], refs[n_cast:]
    out_ref, sfin_ref, cfin_ref = refs[:3]
    cast_out, (st_scr, ubuf) = refs[3:3 + n_cast], refs[3 + n_cast:]
    for src, dst in zip(cast_in, cast_out, strict=True):
        dst[...] = src[...].astype(BF16)
    t = pl.program_id(1)

    @pl.when(t == 0)
    def _():
        if zero_init:
            st_scr[...] = jnp.zeros_like(st_scr)
            ubuf[6:8, :] = jnp.zeros((2, D_B), F32)
        else:
            st_scr[...] = st0_ref[...]
            ubuf[6:8, :] = tail0_ref[...]

    u = u_ref[...]
    ubuf[8:8 + tb, :] = u
    cw = cw_ref[...]
    conv = cw[0:1] * ubuf[6:6 + tb, :] + cw[1:2] * ubuf[7:7 + tb, :] + cw[2:3] * u
    out_ref[:, D_A:] = (gb_ref[...].astype(F32) * conv).astype(BF16)
    last = valid_rows if valid_rows is not None else tb
    ubuf[6:8, :] = ubuf[6 + last:8 + last, :]

    tmat = tmat_ref[...]
    causal = amask_ref[...] > 0.0
    gn = gn_ref[...]
    if valid_rows is not None:
        rowmask = jax.lax.broadcasted_iota(jnp.int32, (CHUNK, SEG), 0) < valid_rows
        rowmask_h = jax.lax.broadcasted_iota(jnp.int32, (CHUNK, DK), 0) < valid_rows
    for ci in range(tb // CHUNK):
        rows = slice(ci * CHUNK, (ci + 1) * CHUNK)
        lf = lf_ref[rows, :]
        if valid_rows is not None:
            lf = jnp.where(rowmask, lf, 0.0)
        b_all = jnp.dot(tmat, _split3(lf), preferred_element_type=F32)
        for h in range(H_A):
            sl = slice(h * DK, (h + 1) * DK)
            b = b_all[:, sl]
            bm = b[mid:mid + 1, :]
            bl = b[CHUNK - 1:CHUNK, :]
            kk = k_ref[rows, sl].astype(F32)
            if valid_rows is not None:
                kk = jnp.where(rowmask_h, kk, 0.0)
            q1, k1, q2, k2 = _head_tiles(q_ref[rows, sl].astype(F32), kk, b, bm, bl)
            vb = v_ref[rows, sl]
            st = st_scr[h]
            a = pl.dot(q1, k1, trans_b=True)
            a = jnp.where(causal, a, 0.0).astype(BF16)
            o = (jnp.dot(a, vb, preferred_element_type=F32)
                 + pl.dot(q2.astype(BF16), st.astype(BF16), trans_b=True))
            st_scr[h] = st * jnp.exp(bl) + pl.dot(vb, k2.astype(BF16), trans_a=True)
            out_ref[rows, sl] = _gated_rmsnorm(o, gn[:, sl], sg_ref[rows, sl].astype(F32))

    @pl.when(t == pl.num_programs(1) - 1)
    def _():
        for h in range(H_A):
            sfin_ref[0, h] = st_scr[h] if state_transposed_out else st_scr[h].T
        cfin_ref[0] = ubuf[6:8, :]


def _mix_seq(proj, row0, nseq, seq_len, tb, st0, tail0, gn, cw, *, mid, valid_rows=None,
             zero_init=False, state_transposed_out=False, cast=()):
    nt = seq_len // tb
    cast_specs = []
    for w in cast:
        assert w.shape[0] % (16 * nseq * nt) == 0
        cast_specs.append(pl.BlockSpec((w.shape[0] // (nseq * nt), w.shape[1]),
                                       lambda b, t: (b * nt + t, 0)))
    assert seq_len % tb == 0 and row0 % tb == 0 and tb % CHUNK == 0
    rb0 = row0 // tb
    seg = lambda: pl.BlockSpec((tb, SEG), lambda b, t: (rb0 + b * nt + t, 0))
    const2 = lambda b, t: (0, 0)
    tril = np.tril(np.ones((CHUNK, CHUNK), np.float32))
    tmat = jnp.asarray(np.concatenate([tril, tril, tril], axis=1), BF16)
    kern = functools.partial(_mix_seq_kernel, tb=tb, mid=mid, valid_rows=valid_rows,
                             zero_init=zero_init, state_transposed_out=state_transposed_out,
                             n_cast=len(cast))
    res = pl.pallas_call(
        kern,
        grid=(nseq, nt),
        in_specs=[seg() for _ in range(7)] + [
                  pl.BlockSpec((H_A, DV, DK), lambda b, t: (0, 0, 0)),
                  pl.BlockSpec((CONV_W - 1, D_B), const2),
                  pl.BlockSpec((1, D_A), const2),
                  pl.BlockSpec((CONV_W, D_B), const2),
                  pl.BlockSpec((CHUNK, 3 * CHUNK), const2),
                  pl.BlockSpec((CHUNK, CHUNK), const2)] + cast_specs,
        out_specs=[pl.BlockSpec((tb, D_MODEL), lambda b, t: (b * nt + t, 0)),
                   pl.BlockSpec((1, H_A, DK, DV), lambda b, t: (b, 0, 0, 0)),
                   pl.BlockSpec((1, CONV_W - 1, D_B), lambda b, t: (b, 0, 0))] + cast_specs,
        out_shape=[jax.ShapeDtypeStruct((nseq * seq_len, D_MODEL), BF16),
                   jax.ShapeDtypeStruct((nseq, H_A, DK, DV), F32),
                   jax.ShapeDtypeStruct((nseq, CONV_W - 1, D_B), F32)]
        + [jax.ShapeDtypeStruct(w.shape, BF16) for w in cast],
        scratch_shapes=[pltpu.VMEM((H_A, DV, DK), F32), pltpu.VMEM((8 + tb, D_B), F32)],
        compiler_params=pltpu.CompilerParams(
            dimension_semantics=("arbitrary", "arbitrary"), vmem_limit_bytes=VMEM_LIMIT),
        name="mix_seq",
    )(*proj, st0, tail0, gn.reshape(1, -1), cw, tmat, jnp.asarray(tril), *cast)
    return res[:3], res[3:]


GROUP = 16
S_LEN = 4
S_MID = 2


def _mix_group_kernel(q_ref, lf_ref, k_ref, v_ref, sg_ref, gb_ref, u_ref,
                      s_ref, cbuf_ref, gn_ref, cw_ref, lmat_ref, amask_ref,
                      out_ref, sfin_ref, cfin_ref, full_scr, y_scr):
    cw = cw_ref[...]
    for s in range(GROUP):
        full_scr[8 * s:8 * s + 2, :] = cbuf_ref[s]
        full_scr[8 * s + 2:8 * s + 2 + S_LEN, :] = u_ref[S_LEN * s:S_LEN * (s + 1), :]
    for s in range(GROUP):
        f0 = full_scr[8 * s:8 * s + S_LEN, :]
        f1 = full_scr[8 * s + 1:8 * s + 1 + S_LEN, :]
        f2 = full_scr[8 * s + 2:8 * s + 2 + S_LEN, :]
        y_scr[S_LEN * s:S_LEN * (s + 1), :] = cw[0:1] * f0 + cw[1:2] * f1 + cw[2:3] * f2
        cfin_ref[s] = full_scr[8 * s + S_LEN:8 * s + S_LEN + 2, :]
    out_ref[:, D_A:] = (gb_ref[...].astype(F32) * y_scr[...]).astype(BF16)

    b3 = jnp.dot(lmat_ref[...], _split3(lf_ref[...]), preferred_element_type=F32)
    causal = amask_ref[...] > 0.0
    gn = gn_ref[...]
    sub = 16
    per_sub = sub // S_LEN
    rid = jax.lax.broadcasted_iota(jnp.int32, (sub, DK), 0)
    own = [(rid >= S_LEN * j) & (rid < S_LEN * (j + 1)) for j in range(per_sub)]
    ones_blk = jnp.ones((sub, DV), BF16)
    zeros_blk = jnp.zeros((sub, DV), BF16)
    for h in range(H_A):
        sl = slice(h * DK, (h + 1) * DK)
        b = b3[0:CHUNK, sl]
        bm = b3[CHUNK:2 * CHUNK, sl]
        bl = b3[2 * CHUNK:3 * CHUNK, sl]
        q1, k1, q2, k2 = _head_tiles(q_ref[:, sl].astype(F32), k_ref[:, sl].astype(F32),
                                     b, bm, bl)
        vb = v_ref[:, sl]
        a = pl.dot(q1, k1, trans_b=True)
        a = jnp.where(causal, a, 0.0).astype(BF16)
        o1 = jnp.dot(a, vb, preferred_element_type=F32)
        decay = jnp.exp(bl)
        for blk in range(CHUNK // sub):
            rs = slice(blk * sub, (blk + 1) * sub)
            q2b, k2b, vbb = q2[rs], k2[rs], vb[rs]
            rhs = jnp.concatenate([jnp.concatenate([vbb, zeros_blk], axis=1),
                                   jnp.concatenate([zeros_blk, ones_blk], axis=1)], axis=0)
            acc = o1[rs]
            for j in range(per_sub):
                s = blk * per_sub + j
                st = s_ref[s, h]
                qm = jnp.where(own[j], q2b, 0.0).astype(BF16)
                acc = acc + jnp.dot(qm, st.astype(BF16), preferred_element_type=F32)
                d = decay[S_LEN * s:S_LEN * s + 1, :]
                d_hi = d.astype(BF16).astype(F32)
                d_r = d - d_hi
                d_mid = d_r.astype(BF16).astype(F32)
                drows = jnp.where(rid == 0, d_hi, jnp.where(rid == 1, d_mid,
                                  jnp.where(rid == 2, d_r - d_mid, 0.0)))
                lhs = jnp.concatenate([jnp.where(own[j], k2b, 0.0), drows], axis=0).astype(BF16)
                ud = pl.dot(lhs, rhs, trans_a=True)
                sfin_ref[s, h] = ud[:, DV:] * st + ud[:, :DV]
            out_ref[rs, sl] = _gated_rmsnorm(acc, gn[:, sl], sg_ref[rs, sl].astype(F32))


def _group_mats():
    r = np.arange(CHUNK)
    seq, pos = r // S_LEN, r % S_LEN
    same = seq[:, None] == seq[None, :]
    cum = same & (pos[None, :] <= pos[:, None])
    midm = same & (pos[None, :] <= S_MID)
    lmat = np.concatenate([cum, midm, same], axis=0).astype(np.float32)
    lmat = np.concatenate([lmat, lmat, lmat], axis=1)
    return jnp.asarray(lmat, BF16), jnp.asarray(cum.astype(np.float32))


def _mix_group(proj, nseq, s0, cbuf, gn, cw):
    steps = nseq // GROUP
    seg = lambda: pl.BlockSpec((CHUNK, SEG), lambda i: (i, 0))
    const2 = lambda i: (0, 0)
    lmat, amask = _group_mats()
    return pl.pallas_call(
        _mix_group_kernel,
        grid=(steps,),
        in_specs=[seg() for _ in range(7)] + [
                  pl.BlockSpec((GROUP, H_A, DK, DV), lambda i: (i, 0, 0, 0)),
                  pl.BlockSpec((GROUP, CONV_W - 1, D_B), lambda i: (i, 0, 0)),
                  pl.BlockSpec((1, D_A), const2),
                  pl.BlockSpec((CONV_W, D_B), const2),
                  pl.BlockSpec((3 * CHUNK, 3 * CHUNK), const2),
                  pl.BlockSpec((CHUNK, CHUNK), const2)],
        out_specs=[pl.BlockSpec((CHUNK, D_MODEL), lambda i: (i, 0)),
                   pl.BlockSpec((GROUP, H_A, DK, DV), lambda i: (i, 0, 0, 0)),
                   pl.BlockSpec((GROUP, CONV_W - 1, D_B), lambda i: (i, 0, 0))],
        out_shape=[jax.ShapeDtypeStruct((nseq * S_LEN, D_MODEL), BF16),
                   jax.ShapeDtypeStruct((nseq, H_A, DK, DV), F32),
                   jax.ShapeDtypeStruct((nseq, CONV_W - 1, D_B), F32)],
        scratch_shapes=[pltpu.VMEM((8 * GROUP, D_B), F32), pltpu.VMEM((CHUNK, D_B), F32)],
        compiler_params=pltpu.CompilerParams(
            dimension_semantics=("arbitrary",), vmem_limit_bytes=VMEM_LIMIT),
        name="mix_group",
    )(*proj, s0, cbuf, gn.reshape(1, -1), cw, lmat, amask)


OUTPROJ_SUB = 512


def _outproj_kernel(x_ref, mix_ref, wo_ref, g0_ref, b0_ref, g1_ref, b1_ref, h_ref):
    for r in range(x_ref.shape[0] // OUTPROJ_SUB):
        rs = slice(r * OUTPROJ_SUB, (r + 1) * OUTPROJ_SUB)
        xn = _layernorm(x_ref[rs, :], g0_ref[...], b0_ref[...])
        m = jnp.dot(mix_ref[rs, :], wo_ref[...], preferred_element_type=F32)
        h_ref[rs, :] = _layernorm(ALPHA * xn + m, g1_ref[...], b1_ref[...])


def _outproj(x, mix, w_o, g0, b0, g1, b1, tm):
    rows = x.shape[0]
    assert rows % tm == 0 and tm % OUTPROJ_SUB == 0
    row = lambda a: a.reshape(1, -1)
    const = lambda i: (0, 0)
    vec = pl.BlockSpec((1, D_MODEL), const)
    return pl.pallas_call(
        _outproj_kernel,
        grid=(rows // tm,),
        in_specs=[pl.BlockSpec((tm, D_MODEL), lambda i: (i, 0)),
                  pl.BlockSpec((tm, D_MODEL), lambda i: (i, 0)),
                  pl.BlockSpec((D_MODEL, D_MODEL), const),
                  vec, vec, vec, vec],
        out_specs=pl.BlockSpec((tm, D_MODEL), lambda i: (i, 0)),
        out_shape=jax.ShapeDtypeStruct((rows, D_MODEL), F32),
        compiler_params=pltpu.CompilerParams(
            dimension_semantics=("arbitrary",), vmem_limit_bytes=VMEM_LIMIT),
        name="outproj",
    )(x, mix, w_o, row(g0), row(b0), row(g1), row(b1))


def _ffn_kernel(h_ref, wg_ref, wu_ref, wd_ref, g2_ref, b2_ref, o_ref, hb_ref):
    f = pl.program_id(1)

    @pl.when(f == 0)
    def _():
        h = h_ref[...]
        hb_ref[...] = h.astype(BF16)
        o_ref[...] = ALPHA * h

    hb = hb_ref[...]
    g = jnp.dot(hb, wg_ref[...], preferred_element_type=F32)
    u = jnp.dot(hb, wu_ref[...], preferred_element_type=F32)
    a = (_silu(g) * u).astype(BF16)
    o_ref[...] += jnp.dot(a, wd_ref[...], preferred_element_type=F32)

    @pl.when(f == pl.num_programs(1) - 1)
    def _():
        o_ref[...] = _layernorm(o_ref[...], g2_ref[...], b2_ref[...])


def _ffn(h, w_gate, w_up, w_down, g2, b2, tm, tf):
    rows = h.shape[0]
    assert rows % tm == 0 and D_FF % tf == 0
    row = lambda a: a.reshape(1, -1)
    vec = pl.BlockSpec((1, D_MODEL), lambda i, f: (0, 0))
    return pl.pallas_call(
        _ffn_kernel,
        grid=(rows // tm, D_FF // tf),
        in_specs=[pl.BlockSpec((tm, D_MODEL), lambda i, f: (i, 0)),
                  pl.BlockSpec((D_MODEL, tf), lambda i, f: (0, f)),
                  pl.BlockSpec((D_MODEL, tf), lambda i, f: (0, f)),
                  pl.BlockSpec((tf, D_MODEL), lambda i, f: (f, 0)),
                  vec, vec],
        out_specs=pl.BlockSpec((tm, D_MODEL), lambda i, f: (i, 0)),
        out_shape=jax.ShapeDtypeStruct((rows, D_MODEL), F32),
        scratch_shapes=[pltpu.VMEM((tm, D_MODEL), BF16)],
        compiler_params=pltpu.CompilerParams(
            dimension_semantics=("arbitrary", "arbitrary"), vmem_limit_bytes=VMEM_LIMIT),
        name="ffn",
    )(h, w_gate, w_up, w_down, row(g2), row(b2))


def kernel(x_prompt, x_sample, state_hgrn, state_conv, meta_tokens, ln0_g, ln0_b, w_in, b_f, lb_param, gnorm_g, conv_w, w_o, ln1_g, ln1_b, w_gate, w_up, w_down, ln2_g, ln2_b):
    bp, seq, _ = x_prompt.shape
    bs, dseq, _ = x_sample.shape
    assert dseq == S_LEN and seq % CHUNK == 0 and bs % GROUP == 0

    lb = jnp.cumsum(jax.nn.softmax(lb_param.astype(F32), axis=0), axis=0)[0]
    w_in_b = w_in[0].astype(BF16)
    g0, b0 = ln0_g.astype(F32), ln0_b.astype(F32)

    xp = x_prompt.reshape(bp * seq, D_MODEL)
    xs = x_sample.reshape(bs * dseq, D_MODEL)
    xs_ext = jnp.concatenate(
        [xs, meta_tokens.astype(F32), jnp.zeros((CHUNK - N_META, D_MODEL), F32)], axis=0)

    proj_p = _inproj(xp, g0, b0, w_in_b, b_f[0], lb, tm=512)
    proj_s = _inproj(xs_ext, g0, b0, w_in_b, b_f[0], lb, tm=xs_ext.shape[0] // 3)

    zero_st = jnp.zeros((H_A, DV, DK), F32)
    zero_tail = jnp.zeros((CONV_W - 1, D_B), F32)
    (_, st_meta, tail_meta), _ = _mix_seq(
        proj_s, bs * dseq, 1, CHUNK, CHUNK, zero_st, zero_tail, gnorm_g[0], conv_w[0],
        mid=N_META // 2, valid_rows=N_META, zero_init=True, state_transposed_out=True)

    (mix_p, hgrn_p, conv_p), (w_o_b, wg_b, wu_b, wd_b) = _mix_seq(
        proj_p, 0, bp, seq, 256, st_meta[0], tail_meta[0], gnorm_g[0], conv_w[0],
        mid=CHUNK // 2, cast=(w_o[0], w_gate[0], w_up[0], w_down[0]))
    mix_s, hgrn_s, conv_s = _mix_group(
        proj_s, bs, state_hgrn[0], state_conv[0], gnorm_g[0], conv_w[0])

    h_p = _outproj(xp, mix_p, w_o_b, g0, b0, ln1_g[0], ln1_b[0], tm=512)
    h_s = _outproj(xs, mix_s, w_o_b, g0, b0, ln1_g[0], ln1_b[0], tm=512)

    y_p = _ffn(h_p, wg_b, wu_b, wd_b, ln2_g[0], ln2_b[0], tm=1024, tf=512)
    y_s = _ffn(h_s, wg_b, wu_b, wd_b, ln2_g[0], ln2_b[0], tm=512, tf=512)

    return (y_p.reshape(bp, seq, D_MODEL), y_s.reshape(bs, dseq, D_MODEL),
            hgrn_p[None], conv_p[None], hgrn_s[None], conv_s[None])
```

```python
import functools

import numpy as np
import jax
import jax.numpy as jnp
from jax.experimental import pallas as pl
from jax.experimental.pallas import tpu as pltpu

F32 = jnp.float32
BF16 = jnp.bfloat16

D_MODEL = 2048
D_A = 1024
D_B = 1024
DK = 128
DV = 128
H_A = 8
SEG = 1024
N_SEG = 7
N_META = 16
CHUNK = 64
D_FF = 5632
CONV_W = 3
ALPHA = 2.0 ** 0.25
LN_EPS = 1e-5
RMS_EPS = 1e-6

V7X_VMEM_BYTES = 64 * 1024 * 1024
VMEM_LIMIT = V7X_VMEM_BYTES - 2 * 1024 * 1024


def _layernorm(x, g, b):
    mu = jnp.mean(x, axis=-1, keepdims=True)
    xc = x - mu
    var = jnp.mean(xc * xc, axis=-1, keepdims=True)
    return xc * jax.lax.rsqrt(var + LN_EPS) * g + b


def _silu(x):
    return x * (1.0 / (1.0 + jnp.exp(-x)))


def _inproj_kernel(x_ref, g0_ref, b0_ref, w_ref, bf_ref, lb_ref,
                   q_ref, lf_ref, k_ref, v_ref, sg_ref, gb_ref, u_ref):
    xn = _layernorm(x_ref[...], g0_ref[...], b0_ref[...]).astype(BF16)

    def seg(j):
        return jnp.dot(xn, w_ref[:, j * SEG:(j + 1) * SEG], preferred_element_type=F32)

    u_ref[...] = seg(5) * seg(6)
    q_ref[...] = _silu(seg(0)).astype(BF16)

    z = seg(1) + bf_ref[...]
    lb = lb_ref[...]
    e = jnp.exp(-jnp.abs(z))
    r = 1.0 / (1.0 + e)
    er = e * r
    pos = z >= 0.0
    lf_ref[...] = jnp.log(lb + (1.0 - lb) * jnp.where(pos, r, er))
    k_ref[...] = ((1.0 - lb) * jnp.where(pos, er, r)).astype(BF16)

    sg_ref[...] = _silu(seg(3)).astype(BF16)
    gb_ref[...] = seg(4).astype(BF16)
    v_ref[...] = seg(2).astype(BF16)


def _inproj(x, g0, b0, w_in, b_f, lb, tm):
    rows = x.shape[0]
    assert rows % tm == 0
    row = lambda a: a.reshape(1, -1)
    const = lambda i: (0, 0)
    out = lambda: pl.BlockSpec((tm, SEG), lambda i: (i, 0))
    sds = lambda dt: jax.ShapeDtypeStruct((rows, SEG), dt)
    return pl.pallas_call(
        _inproj_kernel,
        grid=(rows // tm,),
        in_specs=[
            pl.BlockSpec((tm, D_MODEL), lambda i: (i, 0)),
            pl.BlockSpec((1, D_MODEL), const),
            pl.BlockSpec((1, D_MODEL), const),
            pl.BlockSpec((D_MODEL, N_SEG * SEG), const, pipeline_mode=pl.Buffered(1)),
            pl.BlockSpec((1, SEG), const),
            pl.BlockSpec((1, SEG), const),
        ],
        out_specs=[out() for _ in range(7)],
        out_shape=[sds(BF16), sds(F32), sds(BF16), sds(BF16), sds(BF16), sds(BF16), sds(F32)],
        compiler_params=pltpu.CompilerParams(
            dimension_semantics=("arbitrary",), vmem_limit_bytes=VMEM_LIMIT),
        name="inproj",
    )(x, row(g0), row(b0), w_in, row(b_f), row(lb))


def _split3(x):
    hi = x.astype(BF16)
    r1 = x - hi.astype(F32)
    mid = r1.astype(BF16)
    lo = (r1 - mid.astype(F32)).astype(BF16)
    return jnp.concatenate([hi, mid, lo], axis=0)


def _head_tiles(qt, kk, b, bm, bl):
    q1 = (qt * jnp.exp(b - bm)).astype(BF16)
    k1 = (kk * jnp.exp(bm - b)).astype(BF16)
    q2 = qt * jnp.exp(b)
    k2 = kk * jnp.exp(bl - b)
    return q1, k1, q2, k2


DECAY_ROWS = 16


def _decay_rows(d, rid):
    d_hi = d.astype(BF16).astype(F32)
    d_r = d - d_hi
    d_mid = d_r.astype(BF16).astype(F32)
    return jnp.where(rid == 0, d_hi, jnp.where(rid == 1, d_mid,
                     jnp.where(rid == 2, d_r - d_mid, 0.0)))


def _gated_rmsnorm(o, gn, sg):
    ms = jnp.mean(o * o, axis=-1, keepdims=True)
    return (o * jax.lax.rsqrt(ms + RMS_EPS) * gn * sg).astype(BF16)


def _mix_seq_kernel(q_ref, lf_ref, k_ref, v_ref, sg_ref, gb_ref, u_ref,
                    st0_ref, tail0_ref, gn_ref, cw_ref, tmat_ref, amask_ref, *refs,
                    tb, mid, valid_rows, zero_init, state_transposed_out, n_cast):
    cast_in, refs = refs[:n_cast], refs[n_cast:]
    out_ref, sfin_ref, cfin_ref = refs[:3]
    cast_out, (st_scr, ubuf) = refs[3:3 + n_cast], refs[3 + n_cast:]
    for src, dst in zip(cast_in, cast_out, strict=True):
        dst[...] = src[...].astype(BF16)
    t = pl.program_id(1)

    @pl.when(t == 0)
    def _():
        if zero_init:
            st_scr[...] = jnp.zeros_like(st_scr)
            ubuf[6:8, :] = jnp.zeros((2, D_B), F32)
        else:
            st_scr[...] = st0_ref[...]
            ubuf[6:8, :] = tail0_ref[...]

    u = u_ref[...]
    ubuf[8:8 + tb, :] = u
    cw = cw_ref[...]
    conv = cw[0:1] * ubuf[6:6 + tb, :] + cw[1:2] * ubuf[7:7 + tb, :] + cw[2:3] * u
    out_ref[:, D_A:] = (gb_ref[...].astype(F32) * conv).astype(BF16)
    last = valid_rows if valid_rows is not None else tb
    ubuf[6:8, :] = ubuf[6 + last:8 + last, :]

    tmat = tmat_ref[...]
    causal = amask_ref[...] > 0.0
    gn = gn_ref[...]
    if valid_rows is not None:
        rowmask = jax.lax.broadcasted_iota(jnp.int32, (CHUNK, SEG), 0) < valid_rows
        rowmask_h = jax.lax.broadcasted_iota(jnp.int32, (CHUNK, DK), 0) < valid_rows
    for ci in range(tb // CHUNK):
        rows = slice(ci * CHUNK, (ci + 1) * CHUNK)
        lf = lf_ref[rows, :]
        if valid_rows is not None:
            lf = jnp.where(rowmask, lf, 0.0)
        b_all = jnp.dot(tmat, _split3(lf), preferred_element_type=F32)
        for h in range(H_A):
            sl = slice(h * DK, (h + 1) * DK)
            b = b_all[:, sl]
            bm = b[mid:mid + 1, :]
            bl = b[CHUNK - 1:CHUNK, :]
            kk = k_ref[rows, sl].astype(F32)
            if valid_rows is not None:
                kk = jnp.where(rowmask_h, kk, 0.0)
            q1, k1, q2, k2 = _head_tiles(q_ref[rows, sl].astype(F32), kk, b, bm, bl)
            vb = v_ref[rows, sl]
            st = st_scr[h]
            a = pl.dot(q1, k1, trans_b=True)
            a = jnp.where(causal, a, 0.0).astype(BF16)
            o = (jnp.dot(a, vb, preferred_element_type=F32)
                 + pl.dot(q2.astype(BF16), st.astype(BF16), trans_b=True))
            st_scr[h] = st * jnp.exp(bl) + pl.dot(vb, k2.astype(BF16), trans_a=True)
            out_ref[rows, sl] = _gated_rmsnorm(o, gn[:, sl], sg_ref[rows, sl].astype(F32))

    @pl.when(t == pl.num_programs(1) - 1)
    def _():
        for h in range(H_A):
            sfin_ref[0, h] = st_scr[h] if state_transposed_out else st_scr[h].T
        cfin_ref[0] = ubuf[6:8, :]


def _mix_seq(proj, row0, nseq, seq_len, tb, st0, tail0, gn, cw, *, mid, valid_rows=None,
             zero_init=False, state_transposed_out=False, cast=()):
    nt = seq_len // tb
    cast_specs = []
    for w in cast:
        assert w.shape[0] % (16 * nseq * nt) == 0
        cast_specs.append(pl.BlockSpec((w.shape[0] // (nseq * nt), w.shape[1]),
                                       lambda b, t: (b * nt + t, 0)))
    assert seq_len % tb == 0 and row0 % tb == 0 and tb % CHUNK == 0
    rb0 = row0 // tb
    seg = lambda: pl.BlockSpec((tb, SEG), lambda b, t: (rb0 + b * nt + t, 0))
    const2 = lambda b, t: (0, 0)
    tril = np.tril(np.ones((CHUNK, CHUNK), np.float32))
    tmat = jnp.asarray(np.concatenate([tril, tril, tril], axis=1), BF16)
    kern = functools.partial(_mix_seq_kernel, tb=tb, mid=mid, valid_rows=valid_rows,
                             zero_init=zero_init, state_transposed_out=state_transposed_out,
                             n_cast=len(cast))
    res = pl.pallas_call(
        kern,
        grid=(nseq, nt),
        in_specs=[seg() for _ in range(7)] + [
                  pl.BlockSpec((H_A, DV, DK), lambda b, t: (0, 0, 0)),
                  pl.BlockSpec((CONV_W - 1, D_B), const2),
                  pl.BlockSpec((1, D_A), const2),
                  pl.BlockSpec((CONV_W, D_B), const2),
                  pl.BlockSpec((CHUNK, 3 * CHUNK), const2),
                  pl.BlockSpec((CHUNK, CHUNK), const2)] + cast_specs,
        out_specs=[pl.BlockSpec((tb, D_MODEL), lambda b, t: (b * nt + t, 0)),
                   pl.BlockSpec((1, H_A, DK, DV), lambda b, t: (b, 0, 0, 0)),
                   pl.BlockSpec((1, CONV_W - 1, D_B), lambda b, t: (b, 0, 0))] + cast_specs,
        out_shape=[jax.ShapeDtypeStruct((nseq * seq_len, D_MODEL), BF16),
                   jax.ShapeDtypeStruct((nseq, H_A, DK, DV), F32),
                   jax.ShapeDtypeStruct((nseq, CONV_W - 1, D_B), F32)]
        + [jax.ShapeDtypeStruct(w.shape, BF16) for w in cast],
        scratch_shapes=[pltpu.VMEM((H_A, DV, DK), F32), pltpu.VMEM((8 + tb, D_B), F32)],
        compiler_params=pltpu.CompilerParams(
            dimension_semantics=("arbitrary", "arbitrary"), vmem_limit_bytes=VMEM_LIMIT),
        name="mix_seq",
    )(*proj, st0, tail0, gn.reshape(1, -1), cw, tmat, jnp.asarray(tril), *cast)
    return res[:3], res[3:]


GROUP = 16
S_LEN = 4
S_MID = 2


def _mix_group_kernel(q_ref, lf_ref, k_ref, v_ref, sg_ref, gb_ref, u_ref,
                      s_ref, cbuf_ref, gn_ref, cw_ref, lmat_ref, amask_ref,
                      out_ref, sfin_ref, cfin_ref, full_scr, y_scr):
    cw = cw_ref[...]
    for s in range(GROUP):
        full_scr[8 * s:8 * s + 2, :] = cbuf_ref[s]
        full_scr[8 * s + 2:8 * s + 2 + S_LEN, :] = u_ref[S_LEN * s:S_LEN * (s + 1), :]
    for s in range(GROUP):
        f0 = full_scr[8 * s:8 * s + S_LEN, :]
        f1 = full_scr[8 * s + 1:8 * s + 1 + S_LEN, :]
        f2 = full_scr[8 * s + 2:8 * s + 2 + S_LEN, :]
        y_scr[S_LEN * s:S_LEN * (s + 1), :] = cw[0:1] * f0 + cw[1:2] * f1 + cw[2:3] * f2
        cfin_ref[s] = full_scr[8 * s + S_LEN:8 * s + S_LEN + 2, :]
    out_ref[:, D_A:] = (gb_ref[...].astype(F32) * y_scr[...]).astype(BF16)

    b3 = jnp.dot(lmat_ref[...], _split3(lf_ref[...]), preferred_element_type=F32)
    causal = amask_ref[...] > 0.0
    gn = gn_ref[...]
    sub = 16
    per_sub = sub // S_LEN
    rid = jax.lax.broadcasted_iota(jnp.int32, (sub, DK), 0)
    own = [(rid >= S_LEN * j) & (rid < S_LEN * (j + 1)) for j in range(per_sub)]
    ones_blk = jnp.ones((sub, DV), BF16)
    zeros_blk = jnp.zeros((sub, DV), BF16)
    for h in range(H_A):
        sl = slice(h * DK, (h + 1) * DK)
        b = b3[0:CHUNK, sl]
        bm = b3[CHUNK:2 * CHUNK, sl]
        bl = b3[2 * CHUNK:3 * CHUNK, sl]
        q1, k1, q2, k2 = _head_tiles(q_ref[:, sl].astype(F32), k_ref[:, sl].astype(F32),
                                     b, bm, bl)
        vb = v_ref[:, sl]
        a = pl.dot(q1, k1, trans_b=True)
        a = jnp.where(causal, a, 0.0).astype(BF16)
        o1 = jnp.dot(a, vb, preferred_element_type=F32)
        decay = jnp.exp(bl)
        for blk in range(CHUNK // sub):
            rs = slice(blk * sub, (blk + 1) * sub)
            q2b, k2b, vbb = q2[rs], k2[rs], vb[rs]
            rhs = jnp.concatenate([jnp.concatenate([vbb, zeros_blk], axis=1),
                                   jnp.concatenate([zeros_blk, ones_blk], axis=1)], axis=0)
            acc = o1[rs]
            for j in range(per_sub):
                s = blk * per_sub + j
                st = s_ref[s, h]
                qm = jnp.where(own[j], q2b, 0.0).astype(BF16)
                acc = acc + jnp.dot(qm, st.astype(BF16), preferred_element_type=F32)
                drows = _decay_rows(decay[S_LEN * s:S_LEN * s + 1, :], rid)
                lhs = jnp.concatenate([jnp.where(own[j], k2b, 0.0), drows], axis=0).astype(BF16)
                ud = pl.dot(lhs, rhs, trans_a=True)
                sfin_ref[s, h] = ud[:, DV:] * st + ud[:, :DV]
            out_ref[rs, sl] = _gated_rmsnorm(acc, gn[:, sl], sg_ref[rs, sl].astype(F32))


def _group_mats():
    r = np.arange(CHUNK)
    seq, pos = r // S_LEN, r % S_LEN
    same = seq[:, None] == seq[None, :]
    cum = same & (pos[None, :] <= pos[:, None])
    midm = same & (pos[None, :] <= S_MID)
    lmat = np.concatenate([cum, midm, same], axis=0).astype(np.float32)
    lmat = np.concatenate([lmat, lmat, lmat], axis=1)
    return jnp.asarray(lmat, BF16), jnp.asarray(cum.astype(np.float32))


def _mix_group(proj, nseq, s0, cbuf, gn, cw):
    steps = nseq // GROUP
    seg = lambda: pl.BlockSpec((CHUNK, SEG), lambda i: (i, 0))
    const2 = lambda i: (0, 0)
    lmat, amask = _group_mats()
    return pl.pallas_call(
        _mix_group_kernel,
        grid=(steps,),
        in_specs=[seg() for _ in range(7)] + [
                  pl.BlockSpec((GROUP, H_A, DK, DV), lambda i: (i, 0, 0, 0)),
                  pl.BlockSpec((GROUP, CONV_W - 1, D_B), lambda i: (i, 0, 0)),
                  pl.BlockSpec((1, D_A), const2),
                  pl.BlockSpec((CONV_W, D_B), const2),
                  pl.BlockSpec((3 * CHUNK, 3 * CHUNK), const2),
                  pl.BlockSpec((CHUNK, CHUNK), const2)],
        out_specs=[pl.BlockSpec((CHUNK, D_MODEL), lambda i: (i, 0)),
                   pl.BlockSpec((GROUP, H_A, DK, DV), lambda i: (i, 0, 0, 0)),
                   pl.BlockSpec((GROUP, CONV_W - 1, D_B), lambda i: (i, 0, 0))],
        out_shape=[jax.ShapeDtypeStruct((nseq * S_LEN, D_MODEL), BF16),
                   jax.ShapeDtypeStruct((nseq, H_A, DK, DV), F32),
                   jax.ShapeDtypeStruct((nseq, CONV_W - 1, D_B), F32)],
        scratch_shapes=[pltpu.VMEM((8 * GROUP, D_B), F32), pltpu.VMEM((CHUNK, D_B), F32)],
        compiler_params=pltpu.CompilerParams(
            dimension_semantics=("arbitrary",), vmem_limit_bytes=VMEM_LIMIT),
        name="mix_group",
    )(*proj, s0, cbuf, gn.reshape(1, -1), cw, lmat, amask)


OUTPROJ_SUB = 512


def _outproj_kernel(x_ref, mix_ref, wo_ref, g0_ref, b0_ref, g1_ref, b1_ref, h_ref):
    for r in range(x_ref.shape[0] // OUTPROJ_SUB):
        rs = slice(r * OUTPROJ_SUB, (r + 1) * OUTPROJ_SUB)
        xn = _layernorm(x_ref[rs, :], g0_ref[...], b0_ref[...])
        m = jnp.dot(mix_ref[rs, :], wo_ref[...], preferred_element_type=F32)
        h_ref[rs, :] = _layernorm(ALPHA * xn + m, g1_ref[...], b1_ref[...])


def _outproj(x, mix, w_o, g0, b0, g1, b1, tm):
    rows = x.shape[0]
    assert rows % tm == 0 and tm % OUTPROJ_SUB == 0
    row = lambda a: a.reshape(1, -1)
    const = lambda i: (0, 0)
    vec = pl.BlockSpec((1, D_MODEL), const)
    return pl.pallas_call(
        _outproj_kernel,
        grid=(rows // tm,),
        in_specs=[pl.BlockSpec((tm, D_MODEL), lambda i: (i, 0)),
                  pl.BlockSpec((tm, D_MODEL), lambda i: (i, 0)),
                  pl.BlockSpec((D_MODEL, D_MODEL), const),
                  vec, vec, vec, vec],
        out_specs=pl.BlockSpec((tm, D_MODEL), lambda i: (i, 0)),
        out_shape=jax.ShapeDtypeStruct((rows, D_MODEL), F32),
        compiler_params=pltpu.CompilerParams(
            dimension_semantics=("arbitrary",), vmem_limit_bytes=VMEM_LIMIT),
        name="outproj",
    )(x, mix, w_o, row(g0), row(b0), row(g1), row(b1))


def _ffn_kernel(h_ref, wg_ref, wu_ref, wd_ref, g2_ref, b2_ref, o_ref, hb_ref):
    f = pl.program_id(1)

    @pl.when(f == 0)
    def _():
        h = h_ref[...]
        hb_ref[...] = h.astype(BF16)
        o_ref[...] = ALPHA * h

    hb = hb_ref[...]
    g = jnp.dot(hb, wg_ref[...], preferred_element_type=F32)
    u = jnp.dot(hb, wu_ref[...], preferred_element_type=F32)
    a = (_silu(g) * u).astype(BF16)
    o_ref[...] += jnp.dot(a, wd_ref[...], preferred_element_type=F32)

    @pl.when(f == pl.num_programs(1) - 1)
    def _():
        o_ref[...] = _layernorm(o_ref[...], g2_ref[...], b2_ref[...])


def _ffn(h, w_gate, w_up, w_down, g2, b2, tm, tf):
    rows = h.shape[0]
    assert rows % tm == 0 and D_FF % tf == 0
    row = lambda a: a.reshape(1, -1)
    vec = pl.BlockSpec((1, D_MODEL), lambda i, f: (0, 0))
    return pl.pallas_call(
        _ffn_kernel,
        grid=(rows // tm, D_FF // tf),
        in_specs=[pl.BlockSpec((tm, D_MODEL), lambda i, f: (i, 0)),
                  pl.BlockSpec((D_MODEL, tf), lambda i, f: (0, f)),
                  pl.BlockSpec((D_MODEL, tf), lambda i, f: (0, f)),
                  pl.BlockSpec((tf, D_MODEL), lambda i, f: (f, 0)),
                  vec, vec],
        out_specs=pl.BlockSpec((tm, D_MODEL), lambda i, f: (i, 0)),
        out_shape=jax.ShapeDtypeStruct((rows, D_MODEL), F32),
        scratch_shapes=[pltpu.VMEM((tm, D_MODEL), BF16)],
        compiler_params=pltpu.CompilerParams(
            dimension_semantics=("arbitrary", "arbitrary"), vmem_limit_bytes=VMEM_LIMIT),
        name="ffn",
    )(h, w_gate, w_up, w_down, row(g2), row(b2))


def kernel(x_prompt, x_sample, state_hgrn, state_conv, meta_tokens, ln0_g, ln0_b, w_in, b_f, lb_param, gnorm_g, conv_w, w_o, ln1_g, ln1_b, w_gate, w_up, w_down, ln2_g, ln2_b):
    bp, seq, _ = x_prompt.shape
    bs, dseq, _ = x_sample.shape
    assert dseq == S_LEN and seq % CHUNK == 0 and bs % GROUP == 0

    lb = jnp.cumsum(jax.nn.softmax(lb_param.astype(F32), axis=0), axis=0)[0]
    w_in_b = w_in[0].astype(BF16)
    g0, b0 = ln0_g.astype(F32), ln0_b.astype(F32)

    xp = x_prompt.reshape(bp * seq, D_MODEL)
    xs = x_sample.reshape(bs * dseq, D_MODEL)
    xs_ext = jnp.concatenate(
        [xs, meta_tokens.astype(F32), jnp.zeros((CHUNK - N_META, D_MODEL), F32)], axis=0)

    proj_p = _inproj(xp, g0, b0, w_in_b, b_f[0], lb, tm=512)
    proj_s = _inproj(xs_ext, g0, b0, w_in_b, b_f[0], lb, tm=xs_ext.shape[0] // 3)

    zero_st = jnp.zeros((H_A, DV, DK), F32)
    zero_tail = jnp.zeros((CONV_W - 1, D_B), F32)
    (_, st_meta, tail_meta), _ = _mix_seq(
        proj_s, bs * dseq, 1, CHUNK, CHUNK, zero_st, zero_tail, gnorm_g[0], conv_w[0],
        mid=N_META // 2, valid_rows=N_META, zero_init=True, state_transposed_out=True)

    (mix_p, hgrn_p, conv_p), (w_o_b, wg_b, wu_b, wd_b) = _mix_seq(
        proj_p, 0, bp, seq, 512, st_meta[0], tail_meta[0], gnorm_g[0], conv_w[0],
        mid=CHUNK // 2, cast=(w_o[0], w_gate[0], w_up[0], w_down[0]))
    mix_s, hgrn_s, conv_s = _mix_group(
        proj_s, bs, state_hgrn[0], state_conv[0], gnorm_g[0], conv_w[0])

    h_p = _outproj(xp, mix_p, w_o_b, g0, b0, ln1_g[0], ln1_b[0], tm=512)
    h_s = _outproj(xs, mix_s, w_o_b, g0, b0, ln1_g[0], ln1_b[0], tm=512)

    y_p = _ffn(h_p, wg_b, wu_b, wd_b, ln2_g[0], ln2_b[0], tm=1024, tf=512)
    y_s = _ffn(h_s, wg_b, wu_b, wd_b, ln2_g[0], ln2_b[0], tm=512, tf=512)

    return (y_p.reshape(bp, seq, D_MODEL), y_s.reshape(bs, dseq, D_MODEL),
            hgrn_p[None], conv_p[None], hgrn_s[None], conv_s[None])
```

```python
import functools

import numpy as np
import jax
import jax.numpy as jnp
from jax.experimental import pallas as pl
from jax.experimental.pallas import tpu as pltpu

F32 = jnp.float32
BF16 = jnp.bfloat16

D_MODEL = 2048
D_A = 1024
D_B = 1024
DK = 128
DV = 128
H_A = 8
SEG = 1024
N_SEG = 7
N_META = 16
CHUNK = 64
D_FF = 5632
CONV_W = 3
ALPHA = 2.0 ** 0.25
LN_EPS = 1e-5
RMS_EPS = 1e-6

V7X_VMEM_BYTES = 64 * 1024 * 1024
VMEM_LIMIT = V7X_VMEM_BYTES - 2 * 1024 * 1024


def _layernorm(x, g, b):
    mu = jnp.mean(x, axis=-1, keepdims=True)
    xc = x - mu
    var = jnp.mean(xc * xc, axis=-1, keepdims=True)
    return xc * jax.lax.rsqrt(var + LN_EPS) * g + b


def _silu(x):
    return x * (1.0 / (1.0 + jnp.exp(-x)))


def _inproj_kernel(x_ref, g0_ref, b0_ref, w_ref, bf_ref, lb_ref,
                   q_ref, lf_ref, k_ref, v_ref, sg_ref, gb_ref, u_ref):
    xn = _layernorm(x_ref[...], g0_ref[...], b0_ref[...]).astype(BF16)

    def seg(j):
        return jnp.dot(xn, w_ref[:, j * SEG:(j + 1) * SEG], preferred_element_type=F32)

    u_ref[...] = seg(5) * seg(6)
    q_ref[...] = _silu(seg(0)).astype(BF16)

    lf, kk = _forget_gate(seg(1) + bf_ref[...], lb_ref[...])
    lf_ref[...] = lf
    k_ref[...] = kk.astype(BF16)

    sg_ref[...] = _silu(seg(3)).astype(BF16)
    gb_ref[...] = seg(4).astype(BF16)
    v_ref[...] = seg(2).astype(BF16)


def _inproj(x, g0, b0, w_in, b_f, lb, tm):
    rows = x.shape[0]
    assert rows % tm == 0
    row = lambda a: a.reshape(1, -1)
    const = lambda i: (0, 0)
    out = lambda: pl.BlockSpec((tm, SEG), lambda i: (i, 0))
    sds = lambda dt: jax.ShapeDtypeStruct((rows, SEG), dt)
    return pl.pallas_call(
        _inproj_kernel,
        grid=(rows // tm,),
        in_specs=[
            pl.BlockSpec((tm, D_MODEL), lambda i: (i, 0)),
            pl.BlockSpec((1, D_MODEL), const),
            pl.BlockSpec((1, D_MODEL), const),
            pl.BlockSpec((D_MODEL, N_SEG * SEG), const, pipeline_mode=pl.Buffered(1)),
            pl.BlockSpec((1, SEG), const),
            pl.BlockSpec((1, SEG), const),
        ],
        out_specs=[out() for _ in range(7)],
        out_shape=[sds(BF16), sds(F32), sds(BF16), sds(BF16), sds(BF16), sds(BF16), sds(F32)],
        compiler_params=pltpu.CompilerParams(
            dimension_semantics=("arbitrary",), vmem_limit_bytes=VMEM_LIMIT),
        name="inproj",
    )(x, row(g0), row(b0), w_in, row(b_f), row(lb))


def _forget_gate(z, lb):
    e = jnp.exp(-jnp.abs(z))
    r = 1.0 / (1.0 + e)
    er = e * r
    pos = z >= 0.0
    return jnp.log(lb + (1.0 - lb) * jnp.where(pos, r, er)), (1.0 - lb) * jnp.where(pos, er, r)


def _inproj_cast_kernel(x_ref, g0_ref, b0_ref, w_ref, bf_ref, lb_ref,
                        q_ref, lf_ref, k_ref, v_ref, sg_ref, gb_ref, u_ref, wb_ref,
                        xn_scr, gc_scr):
    j = pl.program_id(0)

    @pl.when(j == 0)
    def _():
        xn_scr[...] = _layernorm(x_ref[...], g0_ref[...], b0_ref[...]).astype(BF16)

    wb = w_ref[...].astype(BF16)
    wb_ref[...] = wb
    acc = jnp.dot(xn_scr[...], wb, preferred_element_type=F32)

    @pl.when(j == 0)
    def _():
        q_ref[...] = _silu(acc).astype(BF16)

    @pl.when(j == 1)
    def _():
        lf, kk = _forget_gate(acc + bf_ref[...], lb_ref[...])
        lf_ref[...] = lf
        k_ref[...] = kk.astype(BF16)

    @pl.when(j == 2)
    def _():
        v_ref[...] = acc.astype(BF16)

    @pl.when(j == 3)
    def _():
        sg_ref[...] = _silu(acc).astype(BF16)

    @pl.when(j == 4)
    def _():
        gb_ref[...] = acc.astype(BF16)

    @pl.when(j == 5)
    def _():
        gc_scr[...] = acc

    @pl.when(j == 6)
    def _():
        u_ref[...] = gc_scr[...] * acc


def _inproj_cast(x, g0, b0, w_in, b_f, lb):
    rows = x.shape[0]
    row = lambda a: a.reshape(1, -1)
    const = lambda j: (0, 0)
    out = lambda: pl.BlockSpec((rows, SEG), const)
    sds = lambda dt: jax.ShapeDtypeStruct((rows, SEG), dt)
    wseg = lambda: pl.BlockSpec((D_MODEL, SEG), lambda j: (0, j))
    res = pl.pallas_call(
        _inproj_cast_kernel,
        grid=(N_SEG,),
        in_specs=[pl.BlockSpec((rows, D_MODEL), const),
                  pl.BlockSpec((1, D_MODEL), const),
                  pl.BlockSpec((1, D_MODEL), const),
                  wseg(),
                  pl.BlockSpec((1, SEG), const),
                  pl.BlockSpec((1, SEG), const)],
        out_specs=[out() for _ in range(7)] + [wseg()],
        out_shape=[sds(BF16), sds(F32), sds(BF16), sds(BF16), sds(BF16), sds(BF16), sds(F32),
                   jax.ShapeDtypeStruct(w_in.shape, BF16)],
        scratch_shapes=[pltpu.VMEM((rows, D_MODEL), BF16), pltpu.VMEM((rows, SEG), F32)],
        compiler_params=pltpu.CompilerParams(
            dimension_semantics=("arbitrary",), vmem_limit_bytes=VMEM_LIMIT),
        name="inproj_cast",
    )(x, row(g0), row(b0), w_in, row(b_f), row(lb))
    return res[:7], res[7]


def _split3(x):
    hi = x.astype(BF16)
    r1 = x - hi.astype(F32)
    mid = r1.astype(BF16)
    lo = (r1 - mid.astype(F32)).astype(BF16)
    return jnp.concatenate([hi, mid, lo], axis=0)


def _head_tiles(qt, kk, b, bm, bl):
    q1 = (qt * jnp.exp(b - bm)).astype(BF16)
    k1 = (kk * jnp.exp(bm - b)).astype(BF16)
    q2 = qt * jnp.exp(b)
    k2 = kk * jnp.exp(bl - b)
    return q1, k1, q2, k2


DECAY_ROWS = 16


def _decay_rows(d, rid):
    d_hi = d.astype(BF16).astype(F32)
    d_r = d - d_hi
    d_mid = d_r.astype(BF16).astype(F32)
    return jnp.where(rid == 0, d_hi, jnp.where(rid == 1, d_mid,
                     jnp.where(rid == 2, d_r - d_mid, 0.0)))


def _gated_rmsnorm(o, gn, sg):
    ms = jnp.mean(o * o, axis=-1, keepdims=True)
    return (o * jax.lax.rsqrt(ms + RMS_EPS) * gn * sg).astype(BF16)


def _mix_seq_kernel(q_ref, lf_ref, k_ref, v_ref, sg_ref, gb_ref, u_ref,
                    st0_ref, tail0_ref, gn_ref, cw_ref, tmat_ref, amask_ref, *refs,
                    tb, mid, valid_rows, zero_init, state_transposed_out, n_cast):
    cast_in, refs = refs[:n_cast], refs[n_cast:]
    out_ref, sfin_ref, cfin_ref = refs[:3]
    cast_out, (st_scr, ubuf) = refs[3:3 + n_cast], refs[3 + n_cast:]
    for src, dst in zip(cast_in, cast_out, strict=True):
        dst[...] = src[...].astype(BF16)
    t = pl.program_id(1)

    @pl.when(t == 0)
    def _():
        if zero_init:
            st_scr[...] = jnp.zeros_like(st_scr)
            ubuf[6:8, :] = jnp.zeros((2, D_B), F32)
        else:
            st_scr[...] = st0_ref[...]
            ubuf[6:8, :] = tail0_ref[...]

    u = u_ref[...]
    ubuf[8:8 + tb, :] = u
    cw = cw_ref[...]
    conv = cw[0:1] * ubuf[6:6 + tb, :] + cw[1:2] * ubuf[7:7 + tb, :] + cw[2:3] * u
    out_ref[:, D_A:] = (gb_ref[...].astype(F32) * conv).astype(BF16)
    last = valid_rows if valid_rows is not None else tb
    ubuf[6:8, :] = ubuf[6 + last:8 + last, :]

    tmat = tmat_ref[...]
    causal = amask_ref[...] > 0.0
    gn = gn_ref[...]
    if valid_rows is not None:
        rowmask = jax.lax.broadcasted_iota(jnp.int32, (CHUNK, SEG), 0) < valid_rows
        rowmask_h = jax.lax.broadcasted_iota(jnp.int32, (CHUNK, DK), 0) < valid_rows
    for ci in range(tb // CHUNK):
        rows = slice(ci * CHUNK, (ci + 1) * CHUNK)
        lf = lf_ref[rows, :]
        if valid_rows is not None:
            lf = jnp.where(rowmask, lf, 0.0)
        b_all = jnp.dot(tmat, _split3(lf), preferred_element_type=F32)
        for h in range(H_A):
            sl = slice(h * DK, (h + 1) * DK)
            b = b_all[:, sl]
            bm = b[mid:mid + 1, :]
            bl = b[CHUNK - 1:CHUNK, :]
            kk = k_ref[rows, sl].astype(F32)
            if valid_rows is not None:
                kk = jnp.where(rowmask_h, kk, 0.0)
            q1, k1, q2, k2 = _head_tiles(q_ref[rows, sl].astype(F32), kk, b, bm, bl)
            vb = v_ref[rows, sl]
            st = st_scr[h]
            a = pl.dot(q1, k1, trans_b=True)
            a = jnp.where(causal, a, 0.0).astype(BF16)
            o = (jnp.dot(a, vb, preferred_element_type=F32)
                 + pl.dot(q2.astype(BF16), st.astype(BF16), trans_b=True))
            st_scr[h] = st * jnp.exp(bl) + pl.dot(vb, k2.astype(BF16), trans_a=True)
            out_ref[rows, sl] = _gated_rmsnorm(o, gn[:, sl], sg_ref[rows, sl].astype(F32))

    @pl.when(t == pl.num_programs(1) - 1)
    def _():
        for h in range(H_A):
            sfin_ref[0, h] = st_scr[h] if state_transposed_out else st_scr[h].T
        cfin_ref[0] = ubuf[6:8, :]


def _mix_seq(proj, row0, nseq, seq_len, tb, st0, tail0, gn, cw, *, mid, valid_rows=None,
             zero_init=False, state_transposed_out=False, cast=()):
    nt = seq_len // tb
    cast_specs = []
    for w in cast:
        assert w.shape[0] % (16 * nseq * nt) == 0
        cast_specs.append(pl.BlockSpec((w.shape[0] // (nseq * nt), w.shape[1]),
                                       lambda b, t: (b * nt + t, 0)))
    assert seq_len % tb == 0 and row0 % tb == 0 and tb % CHUNK == 0
    rb0 = row0 // tb
    seg = lambda: pl.BlockSpec((tb, SEG), lambda b, t: (rb0 + b * nt + t, 0))
    const2 = lambda b, t: (0, 0)
    tril = np.tril(np.ones((CHUNK, CHUNK), np.float32))
    tmat = jnp.asarray(np.concatenate([tril, tril, tril], axis=1), BF16)
    kern = functools.partial(_mix_seq_kernel, tb=tb, mid=mid, valid_rows=valid_rows,
                             zero_init=zero_init, state_transposed_out=state_transposed_out,
                             n_cast=len(cast))
    res = pl.pallas_call(
        kern,
        grid=(nseq, nt),
        in_specs=[seg() for _ in range(7)] + [
                  pl.BlockSpec((H_A, DV, DK), lambda b, t: (0, 0, 0)),
                  pl.BlockSpec((CONV_W - 1, D_B), const2),
                  pl.BlockSpec((1, D_A), const2),
                  pl.BlockSpec((CONV_W, D_B), const2),
                  pl.BlockSpec((CHUNK, 3 * CHUNK), const2),
                  pl.BlockSpec((CHUNK, CHUNK), const2)] + cast_specs,
        out_specs=[pl.BlockSpec((tb, D_MODEL), lambda b, t: (b * nt + t, 0)),
                   pl.BlockSpec((1, H_A, DK, DV), lambda b, t: (b, 0, 0, 0)),
                   pl.BlockSpec((1, CONV_W - 1, D_B), lambda b, t: (b, 0, 0))] + cast_specs,
        out_shape=[jax.ShapeDtypeStruct((nseq * seq_len, D_MODEL), BF16),
                   jax.ShapeDtypeStruct((nseq, H_A, DK, DV), F32),
                   jax.ShapeDtypeStruct((nseq, CONV_W - 1, D_B), F32)]
        + [jax.ShapeDtypeStruct(w.shape, BF16) for w in cast],
        scratch_shapes=[pltpu.VMEM((H_A, DV, DK), F32), pltpu.VMEM((8 + tb, D_B), F32)],
        compiler_params=pltpu.CompilerParams(
            dimension_semantics=("arbitrary", "arbitrary"), vmem_limit_bytes=VMEM_LIMIT),
        name="mix_seq",
    )(*proj, st0, tail0, gn.reshape(1, -1), cw, tmat, jnp.asarray(tril), *cast)
    return res[:3], res[3:]


GROUP = 16
S_LEN = 4
S_MID = 2


def _mix_group_kernel(q_ref, lf_ref, k_ref, v_ref, sg_ref, gb_ref, u_ref,
                      s_ref, cbuf_ref, gn_ref, cw_ref, lmat_ref, amask_ref,
                      out_ref, sfin_ref, cfin_ref, full_scr, y_scr):
    cw = cw_ref[...]
    for s in range(GROUP):
        full_scr[8 * s:8 * s + 2, :] = cbuf_ref[s]
        full_scr[8 * s + 2:8 * s + 2 + S_LEN, :] = u_ref[S_LEN * s:S_LEN * (s + 1), :]
    for s in range(GROUP):
        f0 = full_scr[8 * s:8 * s + S_LEN, :]
        f1 = full_scr[8 * s + 1:8 * s + 1 + S_LEN, :]
        f2 = full_scr[8 * s + 2:8 * s + 2 + S_LEN, :]
        y_scr[S_LEN * s:S_LEN * (s + 1), :] = cw[0:1] * f0 + cw[1:2] * f1 + cw[2:3] * f2
        cfin_ref[s] = full_scr[8 * s + S_LEN:8 * s + S_LEN + 2, :]
    out_ref[:, D_A:] = (gb_ref[...].astype(F32) * y_scr[...]).astype(BF16)

    b3 = jnp.dot(lmat_ref[...], _split3(lf_ref[...]), preferred_element_type=F32)
    causal = amask_ref[...] > 0.0
    gn = gn_ref[...]
    sub = 16
    per_sub = sub // S_LEN
    rid = jax.lax.broadcasted_iota(jnp.int32, (sub, DK), 0)
    own = [(rid >= S_LEN * j) & (rid < S_LEN * (j + 1)) for j in range(per_sub)]
    ones_blk = jnp.ones((sub, DV), BF16)
    zeros_blk = jnp.zeros((sub, DV), BF16)
    for h in range(H_A):
        sl = slice(h * DK, (h + 1) * DK)
        b = b3[0:CHUNK, sl]
        bm = b3[CHUNK:2 * CHUNK, sl]
        bl = b3[2 * CHUNK:3 * CHUNK, sl]
        q1, k1, q2, k2 = _head_tiles(q_ref[:, sl].astype(F32), k_ref[:, sl].astype(F32),
                                     b, bm, bl)
        vb = v_ref[:, sl]
        a = pl.dot(q1, k1, trans_b=True)
        a = jnp.where(causal, a, 0.0).astype(BF16)
        o1 = jnp.dot(a, vb, preferred_element_type=F32)
        decay = jnp.exp(bl)
        for blk in range(CHUNK // sub):
            rs = slice(blk * sub, (blk + 1) * sub)
            q2b, k2b, vbb = q2[rs], k2[rs], vb[rs]
            rhs = jnp.concatenate([jnp.concatenate([vbb, zeros_blk], axis=1),
                                   jnp.concatenate([zeros_blk, ones_blk], axis=1)], axis=0)
            acc = o1[rs]
            for j in range(per_sub):
                s = blk * per_sub + j
                st = s_ref[s, h]
                qm = jnp.where(own[j], q2b, 0.0).astype(BF16)
                acc = acc + jnp.dot(qm, st.astype(BF16), preferred_element_type=F32)
                drows = _decay_rows(decay[S_LEN * s:S_LEN * s + 1, :], rid)
                lhs = jnp.concatenate([jnp.where(own[j], k2b, 0.0), drows], axis=0).astype(BF16)
                ud = pl.dot(lhs, rhs, trans_a=True)
                sfin_ref[s, h] = ud[:, DV:] * st + ud[:, :DV]
            out_ref[rs, sl] = _gated_rmsnorm(acc, gn[:, sl], sg_ref[rs, sl].astype(F32))


def _group_mats():
    r = np.arange(CHUNK)
    seq, pos = r // S_LEN, r % S_LEN
    same = seq[:, None] == seq[None, :]
    cum = same & (pos[None, :] <= pos[:, None])
    midm = same & (pos[None, :] <= S_MID)
    lmat = np.concatenate([cum, midm, same], axis=0).astype(np.float32)
    lmat = np.concatenate([lmat, lmat, lmat], axis=1)
    return jnp.asarray(lmat, BF16), jnp.asarray(cum.astype(np.float32))


def _mix_group(proj, nseq, s0, cbuf, gn, cw):
    steps = nseq // GROUP
    seg = lambda: pl.BlockSpec((CHUNK, SEG), lambda i: (i, 0))
    const2 = lambda i: (0, 0)
    lmat, amask = _group_mats()
    return pl.pallas_call(
        _mix_group_kernel,
        grid=(steps,),
        in_specs=[seg() for _ in range(7)] + [
                  pl.BlockSpec((GROUP, H_A, DK, DV), lambda i: (i, 0, 0, 0)),
                  pl.BlockSpec((GROUP, CONV_W - 1, D_B), lambda i: (i, 0, 0)),
                  pl.BlockSpec((1, D_A), const2),
                  pl.BlockSpec((CONV_W, D_B), const2),
                  pl.BlockSpec((3 * CHUNK, 3 * CHUNK), const2),
                  pl.BlockSpec((CHUNK, CHUNK), const2)],
        out_specs=[pl.BlockSpec((CHUNK, D_MODEL), lambda i: (i, 0)),
                   pl.BlockSpec((GROUP, H_A, DK, DV), lambda i: (i, 0, 0, 0)),
                   pl.BlockSpec((GROUP, CONV_W - 1, D_B), lambda i: (i, 0, 0))],
        out_shape=[jax.ShapeDtypeStruct((nseq * S_LEN, D_MODEL), BF16),
                   jax.ShapeDtypeStruct((nseq, H_A, DK, DV), F32),
                   jax.ShapeDtypeStruct((nseq, CONV_W - 1, D_B), F32)],
        scratch_shapes=[pltpu.VMEM((8 * GROUP, D_B), F32), pltpu.VMEM((CHUNK, D_B), F32)],
        compiler_params=pltpu.CompilerParams(
            dimension_semantics=("arbitrary",), vmem_limit_bytes=VMEM_LIMIT),
        name="mix_group",
    )(*proj, s0, cbuf, gn.reshape(1, -1), cw, lmat, amask)


OUTPROJ_SUB = 512


def _outproj_kernel(x_ref, mix_ref, wo_ref, g0_ref, b0_ref, g1_ref, b1_ref, h_ref):
    for r in range(x_ref.shape[0] // OUTPROJ_SUB):
        rs = slice(r * OUTPROJ_SUB, (r + 1) * OUTPROJ_SUB)
        xn = _layernorm(x_ref[rs, :], g0_ref[...], b0_ref[...])
        m = jnp.dot(mix_ref[rs, :], wo_ref[...], preferred_element_type=F32)
        h_ref[rs, :] = _layernorm(ALPHA * xn + m, g1_ref[...], b1_ref[...])


def _outproj(x, mix, w_o, g0, b0, g1, b1, tm):
    rows = x.shape[0]
    assert rows % tm == 0 and tm % OUTPROJ_SUB == 0
    row = lambda a: a.reshape(1, -1)
    const = lambda i: (0, 0)
    vec = pl.BlockSpec((1, D_MODEL), const)
    return pl.pallas_call(
        _outproj_kernel,
        grid=(rows // tm,),
        in_specs=[pl.BlockSpec((tm, D_MODEL), lambda i: (i, 0)),
                  pl.BlockSpec((tm, D_MODEL), lambda i: (i, 0)),
                  pl.BlockSpec((D_MODEL, D_MODEL), const),
                  vec, vec, vec, vec],
        out_specs=pl.BlockSpec((tm, D_MODEL), lambda i: (i, 0)),
        out_shape=jax.ShapeDtypeStruct((rows, D_MODEL), F32),
        compiler_params=pltpu.CompilerParams(
            dimension_semantics=("arbitrary",), vmem_limit_bytes=VMEM_LIMIT),
        name="outproj",
    )(x, mix, w_o, row(g0), row(b0), row(g1), row(b1))


def _ffn_kernel(h_ref, wg_ref, wu_ref, wd_ref, g2_ref, b2_ref, o_ref, hb_ref):
    f = pl.program_id(1)

    @pl.when(f == 0)
    def _():
        h = h_ref[...]
        hb_ref[...] = h.astype(BF16)
        o_ref[...] = ALPHA * h

    hb = hb_ref[...]
    g = jnp.dot(hb, wg_ref[...], preferred_element_type=F32)
    u = jnp.dot(hb, wu_ref[...], preferred_element_type=F32)
    a = (_silu(g) * u).astype(BF16)
    o_ref[...] += jnp.dot(a, wd_ref[...], preferred_element_type=F32)

    @pl.when(f == pl.num_programs(1) - 1)
    def _():
        o_ref[...] = _layernorm(o_ref[...], g2_ref[...], b2_ref[...])


def _ffn(h, w_gate, w_up, w_down, g2, b2, tm, tf):
    rows = h.shape[0]
    assert rows % tm == 0 and D_FF % tf == 0
    row = lambda a: a.reshape(1, -1)
    vec = pl.BlockSpec((1, D_MODEL), lambda i, f: (0, 0))
    return pl.pallas_call(
        _ffn_kernel,
        grid=(rows // tm, D_FF // tf),
        in_specs=[pl.BlockSpec((tm, D_MODEL), lambda i, f: (i, 0)),
                  pl.BlockSpec((D_MODEL, tf), lambda i, f: (0, f)),
                  pl.BlockSpec((D_MODEL, tf), lambda i, f: (0, f)),
                  pl.BlockSpec((tf, D_MODEL), lambda i, f: (f, 0)),
                  vec, vec],
        out_specs=pl.BlockSpec((tm, D_MODEL), lambda i, f: (i, 0)),
        out_shape=jax.ShapeDtypeStruct((rows, D_MODEL), F32),
        scratch_shapes=[pltpu.VMEM((tm, D_MODEL), BF16)],
        compiler_params=pltpu.CompilerParams(
            dimension_semantics=("arbitrary", "arbitrary"), vmem_limit_bytes=VMEM_LIMIT),
        name="ffn",
    )(h, w_gate, w_up, w_down, row(g2), row(b2))


def kernel(x_prompt, x_sample, state_hgrn, state_conv, meta_tokens, ln0_g, ln0_b, w_in, b_f, lb_param, gnorm_g, conv_w, w_o, ln1_g, ln1_b, w_gate, w_up, w_down, ln2_g, ln2_b):
    bp, seq, _ = x_prompt.shape
    bs, dseq, _ = x_sample.shape
    assert dseq == S_LEN and seq % CHUNK == 0 and bs % GROUP == 0

    lb = jnp.cumsum(jax.nn.softmax(lb_param.astype(F32), axis=0), axis=0)[0]
    g0, b0 = ln0_g.astype(F32), ln0_b.astype(F32)

    xp = x_prompt.reshape(bp * seq, D_MODEL)
    xs = x_sample.reshape(bs * dseq, D_MODEL)
    xs_ext = jnp.concatenate(
        [xs, meta_tokens.astype(F32), jnp.zeros((CHUNK - N_META, D_MODEL), F32)], axis=0)

    proj_s, w_in_b = _inproj_cast(xs_ext, g0, b0, w_in[0], b_f[0], lb)
    proj_p = _inproj(xp, g0, b0, w_in_b, b_f[0], lb, tm=512)

    zero_st = jnp.zeros((H_A, DV, DK), F32)
    zero_tail = jnp.zeros((CONV_W - 1, D_B), F32)
    (_, st_meta, tail_meta), _ = _mix_seq(
        proj_s, bs * dseq, 1, CHUNK, CHUNK, zero_st, zero_tail, gnorm_g[0], conv_w[0],
        mid=N_META // 2, valid_rows=N_META, zero_init=True, state_transposed_out=True)

    (mix_p, hgrn_p, conv_p), (w_o_b, wg_b, wu_b, wd_b) = _mix_seq(
        proj_p, 0, bp, seq, 512, st_meta[0], tail_meta[0], gnorm_g[0], conv_w[0],
        mid=CHUNK // 2, cast=(w_o[0], w_gate[0], w_up[0], w_down[0]))
    mix_s, hgrn_s, conv_s = _mix_group(
        proj_s, bs, state_hgrn[0], state_conv[0], gnorm_g[0], conv_w[0])

    h_p = _outproj(xp, mix_p, w_o_b, g0, b0, ln1_g[0], ln1_b[0], tm=512)
    h_s = _outproj(xs, mix_s, w_o_b, g0, b0, ln1_g[0], ln1_b[0], tm=512)

    y_p = _ffn(h_p, wg_b, wu_b, wd_b, ln2_g[0], ln2_b[0], tm=1024, tf=512)
    y_s = _ffn(h_s, wg_b, wu_b, wd_b, ln2_g[0], ln2_b[0], tm=512, tf=512)

    return (y_p.reshape(bp, seq, D_MODEL), y_s.reshape(bs, dseq, D_MODEL),
            hgrn_p[None], conv_p[None], hgrn_s[None], conv_s[None])
```

```python
import functools

import numpy as np
import jax
import jax.numpy as jnp
from jax.experimental import pallas as pl
from jax.experimental.pallas import tpu as pltpu

F32 = jnp.float32
BF16 = jnp.bfloat16

D_MODEL = 2048
D_A = 1024
D_B = 1024
DK = 128
DV = 128
H_A = 8
SEG = 1024
N_SEG = 7
N_META = 16
CHUNK = 64
D_FF = 5632
CONV_W = 3
ALPHA = 2.0 ** 0.25
LN_EPS = 1e-5
RMS_EPS = 1e-6

V7X_VMEM_BYTES = 64 * 1024 * 1024
VMEM_LIMIT = V7X_VMEM_BYTES - 2 * 1024 * 1024


def _layernorm(x, g, b):
    mu = jnp.mean(x, axis=-1, keepdims=True)
    xc = x - mu
    var = jnp.mean(xc * xc, axis=-1, keepdims=True)
    return xc * jax.lax.rsqrt(var + LN_EPS) * g + b


def _silu(x):
    return x * (1.0 / (1.0 + jnp.exp(-x)))


INPROJ_SUB = 256


def _inproj_kernel(x_ref, g0_ref, b0_ref, w_ref, bf_ref, lb_ref,
                   q_ref, lf_ref, k_ref, v_ref, sg_ref, gb_ref, u_ref):
    for r in range(x_ref.shape[0] // INPROJ_SUB):
        rs = slice(r * INPROJ_SUB, (r + 1) * INPROJ_SUB)
        xn = _layernorm(x_ref[rs, :], g0_ref[...], b0_ref[...]).astype(BF16)

        def seg(j, xn=xn):
            return jnp.dot(xn, w_ref[:, j * SEG:(j + 1) * SEG], preferred_element_type=F32)

        u_ref[rs, :] = seg(5) * seg(6)
        q_ref[rs, :] = _silu(seg(0)).astype(BF16)

        lf, kk = _forget_gate(seg(1) + bf_ref[...], lb_ref[...])
        lf_ref[rs, :] = lf
        k_ref[rs, :] = kk.astype(BF16)

        sg_ref[rs, :] = _silu(seg(3)).astype(BF16)
        gb_ref[rs, :] = seg(4).astype(BF16)
        v_ref[rs, :] = seg(2).astype(BF16)


def _inproj(x, g0, b0, w_in, b_f, lb, tm):
    rows = x.shape[0]
    assert rows % tm == 0 and tm % INPROJ_SUB == 0
    row = lambda a: a.reshape(1, -1)
    const = lambda i: (0, 0)
    out = lambda: pl.BlockSpec((tm, SEG), lambda i: (i, 0))
    sds = lambda dt: jax.ShapeDtypeStruct((rows, SEG), dt)
    return pl.pallas_call(
        _inproj_kernel,
        grid=(rows // tm,),
        in_specs=[
            pl.BlockSpec((tm, D_MODEL), lambda i: (i, 0)),
            pl.BlockSpec((1, D_MODEL), const),
            pl.BlockSpec((1, D_MODEL), const),
            pl.BlockSpec((D_MODEL, N_SEG * SEG), const, pipeline_mode=pl.Buffered(1)),
            pl.BlockSpec((1, SEG), const),
            pl.BlockSpec((1, SEG), const),
        ],
        out_specs=[out() for _ in range(7)],
        out_shape=[sds(BF16), sds(F32), sds(BF16), sds(BF16), sds(BF16), sds(BF16), sds(F32)],
        compiler_params=pltpu.CompilerParams(
            dimension_semantics=("arbitrary",), vmem_limit_bytes=VMEM_LIMIT),
        name="inproj",
    )(x, row(g0), row(b0), w_in, row(b_f), row(lb))


def _forget_gate(z, lb):
    e = jnp.exp(-jnp.abs(z))
    r = 1.0 / (1.0 + e)
    er = e * r
    pos = z >= 0.0
    return jnp.log(lb + (1.0 - lb) * jnp.where(pos, r, er)), (1.0 - lb) * jnp.where(pos, er, r)


def _inproj_cast_kernel(x_ref, g0_ref, b0_ref, w_ref, bf_ref, lb_ref,
                        q_ref, lf_ref, k_ref, v_ref, sg_ref, gb_ref, u_ref, wb_ref,
                        xn_scr, gc_scr):
    j = pl.program_id(0)

    @pl.when(j == 0)
    def _():
        xn_scr[...] = _layernorm(x_ref[...], g0_ref[...], b0_ref[...]).astype(BF16)

    wb = w_ref[...].astype(BF16)
    wb_ref[...] = wb
    acc = jnp.dot(xn_scr[...], wb, preferred_element_type=F32)

    @pl.when(j == 0)
    def _():
        q_ref[...] = _silu(acc).astype(BF16)

    @pl.when(j == 1)
    def _():
        lf, kk = _forget_gate(acc + bf_ref[...], lb_ref[...])
        lf_ref[...] = lf
        k_ref[...] = kk.astype(BF16)

    @pl.when(j == 2)
    def _():
        v_ref[...] = acc.astype(BF16)

    @pl.when(j == 3)
    def _():
        sg_ref[...] = _silu(acc).astype(BF16)

    @pl.when(j == 4)
    def _():
        gb_ref[...] = acc.astype(BF16)

    @pl.when(j == 5)
    def _():
        gc_scr[...] = acc

    @pl.when(j == 6)
    def _():
        u_ref[...] = gc_scr[...] * acc


def _inproj_cast(x, g0, b0, w_in, b_f, lb):
    rows = x.shape[0]
    row = lambda a: a.reshape(1, -1)
    const = lambda j: (0, 0)
    out = lambda: pl.BlockSpec((rows, SEG), const)
    sds = lambda dt: jax.ShapeDtypeStruct((rows, SEG), dt)
    wseg = lambda: pl.BlockSpec((D_MODEL, SEG), lambda j: (0, j))
    res = pl.pallas_call(
        _inproj_cast_kernel,
        grid=(N_SEG,),
        in_specs=[pl.BlockSpec((rows, D_MODEL), const),
                  pl.BlockSpec((1, D_MODEL), const),
                  pl.BlockSpec((1, D_MODEL), const),
                  wseg(),
                  pl.BlockSpec((1, SEG), const),
                  pl.BlockSpec((1, SEG), const)],
        out_specs=[out() for _ in range(7)] + [wseg()],
        out_shape=[sds(BF16), sds(F32), sds(BF16), sds(BF16), sds(BF16), sds(BF16), sds(F32),
                   jax.ShapeDtypeStruct(w_in.shape, BF16)],
        scratch_shapes=[pltpu.VMEM((rows, D_MODEL), BF16), pltpu.VMEM((rows, SEG), F32)],
        compiler_params=pltpu.CompilerParams(
            dimension_semantics=("arbitrary",), vmem_limit_bytes=VMEM_LIMIT),
        name="inproj_cast",
    )(x, row(g0), row(b0), w_in, row(b_f), row(lb))
    return res[:7], res[7]


def _split3(x):
    hi = x.astype(BF16)
    r1 = x - hi.astype(F32)
    mid = r1.astype(BF16)
    lo = (r1 - mid.astype(F32)).astype(BF16)
    return jnp.concatenate([hi, mid, lo], axis=0)


def _head_tiles(qt, kk, b, bm, bl):
    q1 = (qt * jnp.exp(b - bm)).astype(BF16)
    k1 = (kk * jnp.exp(bm - b)).astype(BF16)
    q2 = qt * jnp.exp(b)
    k2 = kk * jnp.exp(bl - b)
    return q1, k1, q2, k2


DECAY_ROWS = 16


def _decay_rows(d, rid):
    d_hi = d.astype(BF16).astype(F32)
    d_r = d - d_hi
    d_mid = d_r.astype(BF16).astype(F32)
    return jnp.where(rid == 0, d_hi, jnp.where(rid == 1, d_mid,
                     jnp.where(rid == 2, d_r - d_mid, 0.0)))


def _gated_rmsnorm(o, gn, sg):
    ms = jnp.mean(o * o, axis=-1, keepdims=True)
    return (o * jax.lax.rsqrt(ms + RMS_EPS) * gn * sg).astype(BF16)


def _mix_seq_kernel(q_ref, lf_ref, k_ref, v_ref, sg_ref, gb_ref, u_ref,
                    st0_ref, tail0_ref, gn_ref, cw_ref, tmat_ref, amask_ref, *refs,
                    tb, mid, valid_rows, zero_init, state_transposed_out, n_cast):
    cast_in, refs = refs[:n_cast], refs[n_cast:]
    out_ref, sfin_ref, cfin_ref = refs[:3]
    cast_out, (st_scr, ubuf) = refs[3:3 + n_cast], refs[3 + n_cast:]
    for src, dst in zip(cast_in, cast_out, strict=True):
        dst[...] = src[...].astype(BF16)
    t = pl.program_id(1)

    @pl.when(t == 0)
    def _():
        if zero_init:
            st_scr[...] = jnp.zeros_like(st_scr)
            ubuf[6:8, :] = jnp.zeros((2, D_B), F32)
        else:
            st_scr[...] = st0_ref[...]
            ubuf[6:8, :] = tail0_ref[...]

    u = u_ref[...]
    ubuf[8:8 + tb, :] = u
    cw = cw_ref[...]
    conv = cw[0:1] * ubuf[6:6 + tb, :] + cw[1:2] * ubuf[7:7 + tb, :] + cw[2:3] * u
    out_ref[:, D_A:] = (gb_ref[...].astype(F32) * conv).astype(BF16)
    last = valid_rows if valid_rows is not None else tb
    ubuf[6:8, :] = ubuf[6 + last:8 + last, :]

    tmat = tmat_ref[...]
    causal = amask_ref[...] > 0.0
    gn = gn_ref[...]
    if valid_rows is not None:
        rowmask = jax.lax.broadcasted_iota(jnp.int32, (CHUNK, SEG), 0) < valid_rows
        rowmask_h = jax.lax.broadcasted_iota(jnp.int32, (CHUNK, DK), 0) < valid_rows
    for ci in range(tb // CHUNK):
        rows = slice(ci * CHUNK, (ci + 1) * CHUNK)
        lf = lf_ref[rows, :]
        if valid_rows is not None:
            lf = jnp.where(rowmask, lf, 0.0)
        b_all = jnp.dot(tmat, _split3(lf), preferred_element_type=F32)
        for h in range(H_A):
            sl = slice(h * DK, (h + 1) * DK)
            b = b_all[:, sl]
            bm = b[mid:mid + 1, :]
            bl = b[CHUNK - 1:CHUNK, :]
            kk = k_ref[rows, sl].astype(F32)
            if valid_rows is not None:
                kk = jnp.where(rowmask_h, kk, 0.0)
            q1, k1, q2, k2 = _head_tiles(q_ref[rows, sl].astype(F32), kk, b, bm, bl)
            vb = v_ref[rows, sl]
            st = st_scr[h]
            a = pl.dot(q1, k1, trans_b=True)
            a = jnp.where(causal, a, 0.0).astype(BF16)
            o = (jnp.dot(a, vb, preferred_element_type=F32)
                 + pl.dot(q2.astype(BF16), st.astype(BF16), trans_b=True))
            st_scr[h] = st * jnp.exp(bl) + pl.dot(vb, k2.astype(BF16), trans_a=True)
            out_ref[rows, sl] = _gated_rmsnorm(o, gn[:, sl], sg_ref[rows, sl].astype(F32))

    @pl.when(t == pl.num_programs(1) - 1)
    def _():
        for h in range(H_A):
            sfin_ref[0, h] = st_scr[h] if state_transposed_out else st_scr[h].T
        cfin_ref[0] = ubuf[6:8, :]


def _mix_seq(proj, row0, nseq, seq_len, tb, st0, tail0, gn, cw, *, mid, valid_rows=None,
             zero_init=False, state_transposed_out=False, cast=()):
    nt = seq_len // tb
    cast_specs = []
    for w in cast:
        assert w.shape[0] % (16 * nseq * nt) == 0
        cast_specs.append(pl.BlockSpec((w.shape[0] // (nseq * nt), w.shape[1]),
                                       lambda b, t: (b * nt + t, 0)))
    assert seq_len % tb == 0 and row0 % tb == 0 and tb % CHUNK == 0
    rb0 = row0 // tb
    seg = lambda: pl.BlockSpec((tb, SEG), lambda b, t: (rb0 + b * nt + t, 0))
    const2 = lambda b, t: (0, 0)
    tril = np.tril(np.ones((CHUNK, CHUNK), np.float32))
    tmat = jnp.asarray(np.concatenate([tril, tril, tril], axis=1), BF16)
    kern = functools.partial(_mix_seq_kernel, tb=tb, mid=mid, valid_rows=valid_rows,
                             zero_init=zero_init, state_transposed_out=state_transposed_out,
                             n_cast=len(cast))
    res = pl.pallas_call(
        kern,
        grid=(nseq, nt),
        in_specs=[seg() for _ in range(7)] + [
                  pl.BlockSpec((H_A, DV, DK), lambda b, t: (0, 0, 0)),
                  pl.BlockSpec((CONV_W - 1, D_B), const2),
                  pl.BlockSpec((1, D_A), const2),
                  pl.BlockSpec((CONV_W, D_B), const2),
                  pl.BlockSpec((CHUNK, 3 * CHUNK), const2),
                  pl.BlockSpec((CHUNK, CHUNK), const2)] + cast_specs,
        out_specs=[pl.BlockSpec((tb, D_MODEL), lambda b, t: (b * nt + t, 0)),
                   pl.BlockSpec((1, H_A, DK, DV), lambda b, t: (b, 0, 0, 0)),
                   pl.BlockSpec((1, CONV_W - 1, D_B), lambda b, t: (b, 0, 0))] + cast_specs,
        out_shape=[jax.ShapeDtypeStruct((nseq * seq_len, D_MODEL), BF16),
                   jax.ShapeDtypeStruct((nseq, H_A, DK, DV), F32),
                   jax.ShapeDtypeStruct((nseq, CONV_W - 1, D_B), F32)]
        + [jax.ShapeDtypeStruct(w.shape, BF16) for w in cast],
        scratch_shapes=[pltpu.VMEM((H_A, DV, DK), F32), pltpu.VMEM((8 + tb, D_B), F32)],
        compiler_params=pltpu.CompilerParams(
            dimension_semantics=("arbitrary", "arbitrary"), vmem_limit_bytes=VMEM_LIMIT),
        name="mix_seq",
    )(*proj, st0, tail0, gn.reshape(1, -1), cw, tmat, jnp.asarray(tril), *cast)
    return res[:3], res[3:]


GROUP = 16
S_LEN = 4
S_MID = 2


def _mix_group_kernel(q_ref, lf_ref, k_ref, v_ref, sg_ref, gb_ref, u_ref,
                      s_ref, cbuf_ref, gn_ref, cw_ref, lmat_ref, amask_ref,
                      out_ref, sfin_ref, cfin_ref, full_scr, y_scr):
    cw = cw_ref[...]
    for s in range(GROUP):
        full_scr[8 * s:8 * s + 2, :] = cbuf_ref[s]
        full_scr[8 * s + 2:8 * s + 2 + S_LEN, :] = u_ref[S_LEN * s:S_LEN * (s + 1), :]
    for s in range(GROUP):
        f0 = full_scr[8 * s:8 * s + S_LEN, :]
        f1 = full_scr[8 * s + 1:8 * s + 1 + S_LEN, :]
        f2 = full_scr[8 * s + 2:8 * s + 2 + S_LEN, :]
        y_scr[S_LEN * s:S_LEN * (s + 1), :] = cw[0:1] * f0 + cw[1:2] * f1 + cw[2:3] * f2
        cfin_ref[s] = full_scr[8 * s + S_LEN:8 * s + S_LEN + 2, :]
    out_ref[:, D_A:] = (gb_ref[...].astype(F32) * y_scr[...]).astype(BF16)

    b3 = jnp.dot(lmat_ref[...], _split3(lf_ref[...]), preferred_element_type=F32)
    causal = amask_ref[...] > 0.0
    gn = gn_ref[...]
    sub = 16
    per_sub = sub // S_LEN
    rid = jax.lax.broadcasted_iota(jnp.int32, (sub, DK), 0)
    own = [(rid >= S_LEN * j) & (rid < S_LEN * (j + 1)) for j in range(per_sub)]
    ones_blk = jnp.ones((sub, DV), BF16)
    zeros_blk = jnp.zeros((sub, DV), BF16)
    for h in range(H_A):
        sl = slice(h * DK, (h + 1) * DK)
        b = b3[0:CHUNK, sl]
        bm = b3[CHUNK:2 * CHUNK, sl]
        bl = b3[2 * CHUNK:3 * CHUNK, sl]
        q1, k1, q2, k2 = _head_tiles(q_ref[:, sl].astype(F32), k_ref[:, sl].astype(F32),
                                     b, bm, bl)
        vb = v_ref[:, sl]
        a = pl.dot(q1, k1, trans_b=True)
        a = jnp.where(causal, a, 0.0).astype(BF16)
        o1 = jnp.dot(a, vb, preferred_element_type=F32)
        decay = jnp.exp(bl)
        for blk in range(CHUNK // sub):
            rs = slice(blk * sub, (blk + 1) * sub)
            q2b, k2b, vbb = q2[rs], k2[rs], vb[rs]
            rhs = jnp.concatenate([jnp.concatenate([vbb, zeros_blk], axis=1),
                                   jnp.concatenate([zeros_blk, ones_blk], axis=1)], axis=0)
            acc = o1[rs]
            for j in range(per_sub):
                s = blk * per_sub + j
                st = s_ref[s, h]
                qm = jnp.where(own[j], q2b, 0.0).astype(BF16)
                acc = acc + jnp.dot(qm, st.astype(BF16), preferred_element_type=F32)
                drows = _decay_rows(decay[S_LEN * s:S_LEN * s + 1, :], rid)
                lhs = jnp.concatenate([jnp.where(own[j], k2b, 0.0), drows], axis=0).astype(BF16)
                ud = pl.dot(lhs, rhs, trans_a=True)
                sfin_ref[s, h] = ud[:, DV:] * st + ud[:, :DV]
            out_ref[rs, sl] = _gated_rmsnorm(acc, gn[:, sl], sg_ref[rs, sl].astype(F32))


def _group_mats():
    r = np.arange(CHUNK)
    seq, pos = r // S_LEN, r % S_LEN
    same = seq[:, None] == seq[None, :]
    cum = same & (pos[None, :] <= pos[:, None])
    midm = same & (pos[None, :] <= S_MID)
    lmat = np.concatenate([cum, midm, same], axis=0).astype(np.float32)
    lmat = np.concatenate([lmat, lmat, lmat], axis=1)
    return jnp.asarray(lmat, BF16), jnp.asarray(cum.astype(np.float32))


def _mix_group(proj, nseq, s0, cbuf, gn, cw):
    steps = nseq // GROUP
    seg = lambda: pl.BlockSpec((CHUNK, SEG), lambda i: (i, 0))
    const2 = lambda i: (0, 0)
    lmat, amask = _group_mats()
    return pl.pallas_call(
        _mix_group_kernel,
        grid=(steps,),
        in_specs=[seg() for _ in range(7)] + [
                  pl.BlockSpec((GROUP, H_A, DK, DV), lambda i: (i, 0, 0, 0)),
                  pl.BlockSpec((GROUP, CONV_W - 1, D_B), lambda i: (i, 0, 0)),
                  pl.BlockSpec((1, D_A), const2),
                  pl.BlockSpec((CONV_W, D_B), const2),
                  pl.BlockSpec((3 * CHUNK, 3 * CHUNK), const2),
                  pl.BlockSpec((CHUNK, CHUNK), const2)],
        out_specs=[pl.BlockSpec((CHUNK, D_MODEL), lambda i: (i, 0)),
                   pl.BlockSpec((GROUP, H_A, DK, DV), lambda i: (i, 0, 0, 0)),
                   pl.BlockSpec((GROUP, CONV_W - 1, D_B), lambda i: (i, 0, 0))],
        out_shape=[jax.ShapeDtypeStruct((nseq * S_LEN, D_MODEL), BF16),
                   jax.ShapeDtypeStruct((nseq, H_A, DK, DV), F32),
                   jax.ShapeDtypeStruct((nseq, CONV_W - 1, D_B), F32)],
        scratch_shapes=[pltpu.VMEM((8 * GROUP, D_B), F32), pltpu.VMEM((CHUNK, D_B), F32)],
        compiler_params=pltpu.CompilerParams(
            dimension_semantics=("arbitrary",), vmem_limit_bytes=VMEM_LIMIT),
        name="mix_group",
    )(*proj, s0, cbuf, gn.reshape(1, -1), cw, lmat, amask)


OUTPROJ_SUB = 512


def _outproj_kernel(x_ref, mix_ref, wo_ref, g0_ref, b0_ref, g1_ref, b1_ref, h_ref):
    for r in range(x_ref.shape[0] // OUTPROJ_SUB):
        rs = slice(r * OUTPROJ_SUB, (r + 1) * OUTPROJ_SUB)
        xn = _layernorm(x_ref[rs, :], g0_ref[...], b0_ref[...])
        m = jnp.dot(mix_ref[rs, :], wo_ref[...], preferred_element_type=F32)
        h_ref[rs, :] = _layernorm(ALPHA * xn + m, g1_ref[...], b1_ref[...])


def _outproj(x, mix, w_o, g0, b0, g1, b1, tm):
    rows = x.shape[0]
    assert rows % tm == 0 and tm % OUTPROJ_SUB == 0
    row = lambda a: a.reshape(1, -1)
    const = lambda i: (0, 0)
    vec = pl.BlockSpec((1, D_MODEL), const)
    return pl.pallas_call(
        _outproj_kernel,
        grid=(rows // tm,),
        in_specs=[pl.BlockSpec((tm, D_MODEL), lambda i: (i, 0)),
                  pl.BlockSpec((tm, D_MODEL), lambda i: (i, 0)),
                  pl.BlockSpec((D_MODEL, D_MODEL), const),
                  vec, vec, vec, vec],
        out_specs=pl.BlockSpec((tm, D_MODEL), lambda i: (i, 0)),
        out_shape=jax.ShapeDtypeStruct((rows, D_MODEL), F32),
        compiler_params=pltpu.CompilerParams(
            dimension_semantics=("arbitrary",), vmem_limit_bytes=VMEM_LIMIT),
        name="outproj",
    )(x, mix, w_o, row(g0), row(b0), row(g1), row(b1))


def _ffn_kernel(h_ref, wg_ref, wu_ref, wd_ref, g2_ref, b2_ref, o_ref, hb_ref):
    f = pl.program_id(1)

    @pl.when(f == 0)
    def _():
        h = h_ref[...]
        hb_ref[...] = h.astype(BF16)
        o_ref[...] = ALPHA * h

    hb = hb_ref[...]
    g = jnp.dot(hb, wg_ref[...], preferred_element_type=F32)
    u = jnp.dot(hb, wu_ref[...], preferred_element_type=F32)
    a = (_silu(g) * u).astype(BF16)
    o_ref[...] += jnp.dot(a, wd_ref[...], preferred_element_type=F32)

    @pl.when(f == pl.num_programs(1) - 1)
    def _():
        o_ref[...] = _layernorm(o_ref[...], g2_ref[...], b2_ref[...])


def _ffn(h, w_gate, w_up, w_down, g2, b2, tm, tf):
    rows = h.shape[0]
    assert rows % tm == 0 and D_FF % tf == 0
    row = lambda a: a.reshape(1, -1)
    vec = pl.BlockSpec((1, D_MODEL), lambda i, f: (0, 0))
    return pl.pallas_call(
        _ffn_kernel,
        grid=(rows // tm, D_FF // tf),
        in_specs=[pl.BlockSpec((tm, D_MODEL), lambda i, f: (i, 0)),
                  pl.BlockSpec((D_MODEL, tf), lambda i, f: (0, f)),
                  pl.BlockSpec((D_MODEL, tf), lambda i, f: (0, f)),
                  pl.BlockSpec((tf, D_MODEL), lambda i, f: (f, 0)),
                  vec, vec],
        out_specs=pl.BlockSpec((tm, D_MODEL), lambda i, f: (i, 0)),
        out_shape=jax.ShapeDtypeStruct((rows, D_MODEL), F32),
        scratch_shapes=[pltpu.VMEM((tm, D_MODEL), BF16)],
        compiler_params=pltpu.CompilerParams(
            dimension_semantics=("arbitrary", "arbitrary"), vmem_limit_bytes=VMEM_LIMIT),
        name="ffn",
    )(h, w_gate, w_up, w_down, row(g2), row(b2))


def kernel(x_prompt, x_sample, state_hgrn, state_conv, meta_tokens, ln0_g, ln0_b, w_in, b_f, lb_param, gnorm_g, conv_w, w_o, ln1_g, ln1_b, w_gate, w_up, w_down, ln2_g, ln2_b):
    bp, seq, _ = x_prompt.shape
    bs, dseq, _ = x_sample.shape
    assert dseq == S_LEN and seq % CHUNK == 0 and bs % GROUP == 0

    lb = jnp.cumsum(jax.nn.softmax(lb_param.astype(F32), axis=0), axis=0)[0]
    g0, b0 = ln0_g.astype(F32), ln0_b.astype(F32)

    xp = x_prompt.reshape(bp * seq, D_MODEL)
    xs = x_sample.reshape(bs * dseq, D_MODEL)
    xs_ext = jnp.concatenate(
        [xs, meta_tokens.astype(F32), jnp.zeros((CHUNK - N_META, D_MODEL), F32)], axis=0)

    proj_s, w_in_b = _inproj_cast(xs_ext, g0, b0, w_in[0], b_f[0], lb)
    proj_p = _inproj(xp, g0, b0, w_in_b, b_f[0], lb, tm=512)

    zero_st = jnp.zeros((H_A, DV, DK), F32)
    zero_tail = jnp.zeros((CONV_W - 1, D_B), F32)
    (_, st_meta, tail_meta), _ = _mix_seq(
        proj_s, bs * dseq, 1, CHUNK, CHUNK, zero_st, zero_tail, gnorm_g[0], conv_w[0],
        mid=N_META // 2, valid_rows=N_META, zero_init=True, state_transposed_out=True)

    (mix_p, hgrn_p, conv_p), (w_o_b, wg_b, wu_b, wd_b) = _mix_seq(
        proj_p, 0, bp, seq, 512, st_meta[0], tail_meta[0], gnorm_g[0], conv_w[0],
        mid=CHUNK // 2, cast=(w_o[0], w_gate[0], w_up[0], w_down[0]))
    mix_s, hgrn_s, conv_s = _mix_group(
        proj_s, bs, state_hgrn[0], state_conv[0], gnorm_g[0], conv_w[0])

    h_p = _outproj(xp, mix_p, w_o_b, g0, b0, ln1_g[0], ln1_b[0], tm=512)
    h_s = _outproj(xs, mix_s, w_o_b, g0, b0, ln1_g[0], ln1_b[0], tm=512)

    y_p = _ffn(h_p, wg_b, wu_b, wd_b, ln2_g[0], ln2_b[0], tm=1024, tf=512)
    y_s = _ffn(h_s, wg_b, wu_b, wd_b, ln2_g[0], ln2_b[0], tm=512, tf=512)

    return (y_p.reshape(bp, seq, D_MODEL), y_s.reshape(bs, dseq, D_MODEL),
            hgrn_p[None], conv_p[None], hgrn_s[None], conv_s[None])
```

```python
import functools

import numpy as np
import jax
import jax.numpy as jnp
from jax.experimental import pallas as pl
from jax.experimental.pallas import tpu as pltpu

F32 = jnp.float32
BF16 = jnp.bfloat16

D_MODEL = 2048
D_A = 1024
D_B = 1024
DK = 128
DV = 128
H_A = 8
SEG = 1024
N_SEG = 7
N_META = 16
CHUNK = 64
D_FF = 5632
CONV_W = 3
ALPHA = 2.0 ** 0.25
LN_EPS = 1e-5
RMS_EPS = 1e-6

V7X_VMEM_BYTES = 64 * 1024 * 1024
SUBLANES = 8
BF16_ROWS = 16
VMEM_LIMIT = V7X_VMEM_BYTES - 2 * 1024 * 1024

INPROJ_TM = 512
MIX_TB = 512
OUTPROJ_TM = 512
FFN_TM = 1024
FFN_TM_SMALL = 512
FFN_TF = 512
FFN_TF_SMALL = 1408

CONV_TAIL = SUBLANES - (CONV_W - 1)


def _layernorm(x, g, b):
    mu = jnp.mean(x, axis=-1, keepdims=True)
    xc = x - mu
    var = jnp.mean(xc * xc, axis=-1, keepdims=True)
    return xc * jax.lax.rsqrt(var + LN_EPS) * g + b


def _silu(x):
    return x * (1.0 / (1.0 + jnp.exp(-x)))


INPROJ_SUB = 256


def _inproj_kernel(x_ref, g0_ref, b0_ref, w_ref, bf_ref, lb_ref,
                   q_ref, lf_ref, k_ref, v_ref, sg_ref, gb_ref, u_ref):
    for r in range(x_ref.shape[0] // INPROJ_SUB):
        rs = slice(r * INPROJ_SUB, (r + 1) * INPROJ_SUB)
        xn = _layernorm(x_ref[rs, :], g0_ref[...], b0_ref[...]).astype(BF16)

        def seg(j, xn=xn):
            return jnp.dot(xn, w_ref[:, j * SEG:(j + 1) * SEG], preferred_element_type=F32)

        u_ref[rs, :] = seg(5) * seg(6)
        q_ref[rs, :] = _silu(seg(0)).astype(BF16)

        lf, kk = _forget_gate(seg(1) + bf_ref[...], lb_ref[...])
        lf_ref[rs, :] = lf
        k_ref[rs, :] = kk.astype(BF16)

        sg_ref[rs, :] = _silu(seg(3)).astype(BF16)
        gb_ref[rs, :] = seg(4).astype(BF16)
        v_ref[rs, :] = seg(2).astype(BF16)


def _inproj(x, g0, b0, w_in, b_f, lb, tm):
    rows = x.shape[0]
    assert rows % tm == 0 and tm % INPROJ_SUB == 0
    row = lambda a: a.reshape(1, -1)
    const = lambda i: (0, 0)
    out = lambda: pl.BlockSpec((tm, SEG), lambda i: (i, 0))
    sds = lambda dt: jax.ShapeDtypeStruct((rows, SEG), dt)
    return pl.pallas_call(
        _inproj_kernel,
        grid=(rows // tm,),
        in_specs=[
            pl.BlockSpec((tm, D_MODEL), lambda i: (i, 0)),
            pl.BlockSpec((1, D_MODEL), const),
            pl.BlockSpec((1, D_MODEL), const),
            pl.BlockSpec((D_MODEL, N_SEG * SEG), const, pipeline_mode=pl.Buffered(1)),
            pl.BlockSpec((1, SEG), const),
            pl.BlockSpec((1, SEG), const),
        ],
        out_specs=[out() for _ in range(7)],
        out_shape=[sds(BF16), sds(F32), sds(BF16), sds(BF16), sds(BF16), sds(BF16), sds(F32)],
        compiler_params=pltpu.CompilerParams(
            dimension_semantics=("arbitrary",), vmem_limit_bytes=VMEM_LIMIT),
        name="inproj",
    )(x, row(g0), row(b0), w_in, row(b_f), row(lb))


def _forget_gate(z, lb):
    e = jnp.exp(-jnp.abs(z))
    r = 1.0 / (1.0 + e)
    er = e * r
    pos = z >= 0.0
    return jnp.log(lb + (1.0 - lb) * jnp.where(pos, r, er)), (1.0 - lb) * jnp.where(pos, er, r)


def _inproj_cast_kernel(x_ref, g0_ref, b0_ref, w_ref, bf_ref, lb_ref,
                        q_ref, lf_ref, k_ref, v_ref, sg_ref, gb_ref, u_ref, wb_ref,
                        xn_scr, gc_scr):
    j = pl.program_id(0)

    @pl.when(j == 0)
    def _():
        xn_scr[...] = _layernorm(x_ref[...], g0_ref[...], b0_ref[...]).astype(BF16)

    wb = w_ref[...].astype(BF16)
    wb_ref[...] = wb
    acc = jnp.dot(xn_scr[...], wb, preferred_element_type=F32)

    @pl.when(j == 0)
    def _():
        q_ref[...] = _silu(acc).astype(BF16)

    @pl.when(j == 1)
    def _():
        lf, kk = _forget_gate(acc + bf_ref[...], lb_ref[...])
        lf_ref[...] = lf
        k_ref[...] = kk.astype(BF16)

    @pl.when(j == 2)
    def _():
        v_ref[...] = acc.astype(BF16)

    @pl.when(j == 3)
    def _():
        sg_ref[...] = _silu(acc).astype(BF16)

    @pl.when(j == 4)
    def _():
        gb_ref[...] = acc.astype(BF16)

    @pl.when(j == 5)
    def _():
        gc_scr[...] = acc

    @pl.when(j == 6)
    def _():
        u_ref[...] = gc_scr[...] * acc


def _inproj_cast(x, g0, b0, w_in, b_f, lb):
    rows = x.shape[0]
    row = lambda a: a.reshape(1, -1)
    const = lambda j: (0, 0)
    out = lambda: pl.BlockSpec((rows, SEG), const)
    sds = lambda dt: jax.ShapeDtypeStruct((rows, SEG), dt)
    wseg = lambda: pl.BlockSpec((D_MODEL, SEG), lambda j: (0, j))
    res = pl.pallas_call(
        _inproj_cast_kernel,
        grid=(N_SEG,),
        in_specs=[pl.BlockSpec((rows, D_MODEL), const),
                  pl.BlockSpec((1, D_MODEL), const),
                  pl.BlockSpec((1, D_MODEL), const),
                  wseg(),
                  pl.BlockSpec((1, SEG), const),
                  pl.BlockSpec((1, SEG), const)],
        out_specs=[out() for _ in range(7)] + [wseg()],
        out_shape=[sds(BF16), sds(F32), sds(BF16), sds(BF16), sds(BF16), sds(BF16), sds(F32),
                   jax.ShapeDtypeStruct(w_in.shape, BF16)],
        scratch_shapes=[pltpu.VMEM((rows, D_MODEL), BF16), pltpu.VMEM((rows, SEG), F32)],
        compiler_params=pltpu.CompilerParams(
            dimension_semantics=("arbitrary",), vmem_limit_bytes=VMEM_LIMIT),
        name="inproj_cast",
    )(x, row(g0), row(b0), w_in, row(b_f), row(lb))
    return res[:7], res[7]


def _split3(x):
    hi = x.astype(BF16)
    r1 = x - hi.astype(F32)
    mid = r1.astype(BF16)
    lo = (r1 - mid.astype(F32)).astype(BF16)
    return jnp.concatenate([hi, mid, lo], axis=0)


def _head_tiles(qt, kk, b, bm, bl):
    q1 = (qt * jnp.exp(b - bm)).astype(BF16)
    k1 = (kk * jnp.exp(bm - b)).astype(BF16)
    q2 = qt * jnp.exp(b)
    k2 = kk * jnp.exp(bl - b)
    return q1, k1, q2, k2


DECAY_ROWS = 16


def _decay_rows(d, rid):
    d_hi = d.astype(BF16).astype(F32)
    d_r = d - d_hi
    d_mid = d_r.astype(BF16).astype(F32)
    return jnp.where(rid == 0, d_hi, jnp.where(rid == 1, d_mid,
                     jnp.where(rid == 2, d_r - d_mid, 0.0)))


def _gated_rmsnorm(o, gn, sg):
    ms = jnp.mean(o * o, axis=-1, keepdims=True)
    return (o * jax.lax.rsqrt(ms + RMS_EPS) * gn * sg).astype(BF16)


def _mix_seq_kernel(q_ref, lf_ref, k_ref, v_ref, sg_ref, gb_ref, u_ref,
                    st0_ref, tail0_ref, gn_ref, cw_ref, tmat_ref, amask_ref, *refs,
                    tb, mid, valid_rows, state_transposed_out, n_cast):
    cast_in, refs = refs[:n_cast], refs[n_cast:]
    out_ref, sfin_ref, cfin_ref = refs[:3]
    cast_out, (st_scr, ubuf) = refs[3:3 + n_cast], refs[3 + n_cast:]
    for src, dst in zip(cast_in, cast_out, strict=True):
        dst[...] = src[...].astype(BF16)
    t = pl.program_id(1)

    @pl.when(t == 0)
    def _():
        st_scr[...] = st0_ref[...]
        ubuf[CONV_TAIL:SUBLANES, :] = tail0_ref[...]

    u = u_ref[...]
    ubuf[SUBLANES:SUBLANES + tb, :] = u
    cw = cw_ref[...]
    conv = (cw[0:1] * ubuf[CONV_TAIL:CONV_TAIL + tb, :]
            + cw[1:2] * ubuf[CONV_TAIL + 1:CONV_TAIL + 1 + tb, :] + cw[2:3] * u)
    out_ref[:, D_A:] = (gb_ref[...].astype(F32) * conv).astype(BF16)
    last = valid_rows if valid_rows is not None else tb
    ubuf[CONV_TAIL:SUBLANES, :] = ubuf[CONV_TAIL + last:SUBLANES + last, :]

    tmat = tmat_ref[...]
    causal = amask_ref[...] > 0.0
    gn = gn_ref[...]
    if valid_rows is not None:
        rowmask = jax.lax.broadcasted_iota(jnp.int32, (CHUNK, SEG), 0) < valid_rows
        rowmask_h = jax.lax.broadcasted_iota(jnp.int32, (CHUNK, DK), 0) < valid_rows
    for ci in range(tb // CHUNK):
        rows = slice(ci * CHUNK, (ci + 1) * CHUNK)
        lf = lf_ref[rows, :]
        if valid_rows is not None:
            lf = jnp.where(rowmask, lf, 0.0)
        b_all = jnp.dot(tmat, _split3(lf), preferred_element_type=F32)
        for h in range(H_A):
            sl = slice(h * DK, (h + 1) * DK)
            b = b_all[:, sl]
            bm = b[mid:mid + 1, :]
            bl = b[CHUNK - 1:CHUNK, :]
            kk = k_ref[rows, sl].astype(F32)
            if valid_rows is not None:
                kk = jnp.where(rowmask_h, kk, 0.0)
            q1, k1, q2, k2 = _head_tiles(q_ref[rows, sl].astype(F32), kk, b, bm, bl)
            vb = v_ref[rows, sl]
            st = st_scr[h]
            a = pl.dot(q1, k1, trans_b=True)
            a = jnp.where(causal, a, 0.0).astype(BF16)
            o = (jnp.dot(a, vb, preferred_element_type=F32)
                 + pl.dot(q2.astype(BF16), st.astype(BF16), trans_b=True))
            st_scr[h] = st * jnp.exp(bl) + pl.dot(vb, k2.astype(BF16), trans_a=True)
            out_ref[rows, sl] = _gated_rmsnorm(o, gn[:, sl], sg_ref[rows, sl].astype(F32))

    @pl.when(t == pl.num_programs(1) - 1)
    def _():
        for h in range(H_A):
            sfin_ref[0, h] = st_scr[h] if state_transposed_out else st_scr[h].T
        cfin_ref[0] = ubuf[CONV_TAIL:SUBLANES, :]


def _mix_seq(proj, row0, nseq, seq_len, tb, st0, tail0, gn, cw, *, mid, valid_rows=None,
             state_transposed_out=False, cast=()):
    nt = seq_len // tb
    cast_specs = []
    for w in cast:
        assert w.shape[0] % (BF16_ROWS * nseq * nt) == 0
        cast_specs.append(pl.BlockSpec((w.shape[0] // (nseq * nt), w.shape[1]),
                                       lambda b, t: (b * nt + t, 0)))
    assert seq_len % tb == 0 and row0 % tb == 0 and tb % CHUNK == 0
    rb0 = row0 // tb
    seg = lambda: pl.BlockSpec((tb, SEG), lambda b, t: (rb0 + b * nt + t, 0))
    const2 = lambda b, t: (0, 0)
    tril = np.tril(np.ones((CHUNK, CHUNK), np.float32))
    tmat = jnp.asarray(np.concatenate([tril, tril, tril], axis=1), BF16)
    kern = functools.partial(_mix_seq_kernel, tb=tb, mid=mid, valid_rows=valid_rows,
                             state_transposed_out=state_transposed_out, n_cast=len(cast))
    res = pl.pallas_call(
        kern,
        grid=(nseq, nt),
        in_specs=[seg() for _ in range(7)] + [
                  pl.BlockSpec((H_A, DV, DK), lambda b, t: (0, 0, 0)),
                  pl.BlockSpec((CONV_W - 1, D_B), const2),
                  pl.BlockSpec((1, D_A), const2),
                  pl.BlockSpec((CONV_W, D_B), const2),
                  pl.BlockSpec((CHUNK, 3 * CHUNK), const2),
                  pl.BlockSpec((CHUNK, CHUNK), const2)] + cast_specs,
        out_specs=[pl.BlockSpec((tb, D_MODEL), lambda b, t: (b * nt + t, 0)),
                   pl.BlockSpec((1, H_A, DK, DV), lambda b, t: (b, 0, 0, 0)),
                   pl.BlockSpec((1, CONV_W - 1, D_B), lambda b, t: (b, 0, 0))] + cast_specs,
        out_shape=[jax.ShapeDtypeStruct((nseq * seq_len, D_MODEL), BF16),
                   jax.ShapeDtypeStruct((nseq, H_A, DK, DV), F32),
                   jax.ShapeDtypeStruct((nseq, CONV_W - 1, D_B), F32)]
        + [jax.ShapeDtypeStruct(w.shape, BF16) for w in cast],
        scratch_shapes=[pltpu.VMEM((H_A, DV, DK), F32), pltpu.VMEM((SUBLANES + tb, D_B), F32)],
        compiler_params=pltpu.CompilerParams(
            dimension_semantics=("arbitrary", "arbitrary"), vmem_limit_bytes=VMEM_LIMIT),
        name="mix_seq",
    )(*proj, st0, tail0, gn.reshape(1, -1), cw, tmat, jnp.asarray(tril), *cast)
    return res[:3], res[3:]


GROUP = 16
S_LEN = 4
S_MID = 2


def _mix_group_kernel(q_ref, lf_ref, k_ref, v_ref, sg_ref, gb_ref, u_ref,
                      s_ref, cbuf_ref, gn_ref, cw_ref, lmat_ref, amask_ref,
                      out_ref, sfin_ref, cfin_ref, full_scr, y_scr):
    cw = cw_ref[...]
    nb = CONV_W - 1
    for s in range(GROUP):
        r0 = SUBLANES * s
        full_scr[r0:r0 + nb, :] = cbuf_ref[s]
        full_scr[r0 + nb:r0 + nb + S_LEN, :] = u_ref[S_LEN * s:S_LEN * (s + 1), :]
    for s in range(GROUP):
        r0 = SUBLANES * s
        f0 = full_scr[r0:r0 + S_LEN, :]
        f1 = full_scr[r0 + 1:r0 + 1 + S_LEN, :]
        f2 = full_scr[r0 + 2:r0 + 2 + S_LEN, :]
        y_scr[S_LEN * s:S_LEN * (s + 1), :] = cw[0:1] * f0 + cw[1:2] * f1 + cw[2:3] * f2
        cfin_ref[s] = full_scr[r0 + S_LEN:r0 + S_LEN + nb, :]
    out_ref[:, D_A:] = (gb_ref[...].astype(F32) * y_scr[...]).astype(BF16)

    b3 = jnp.dot(lmat_ref[...], _split3(lf_ref[...]), preferred_element_type=F32)
    causal = amask_ref[...] > 0.0
    gn = gn_ref[...]
    sub = BF16_ROWS
    per_sub = sub // S_LEN
    rid = jax.lax.broadcasted_iota(jnp.int32, (sub, DK), 0)
    own = [(rid >= S_LEN * j) & (rid < S_LEN * (j + 1)) for j in range(per_sub)]
    ones_blk = jnp.ones((sub, DV), BF16)
    zeros_blk = jnp.zeros((sub, DV), BF16)
    for h in range(H_A):
        sl = slice(h * DK, (h + 1) * DK)
        b = b3[0:CHUNK, sl]
        bm = b3[CHUNK:2 * CHUNK, sl]
        bl = b3[2 * CHUNK:3 * CHUNK, sl]
        q1, k1, q2, k2 = _head_tiles(q_ref[:, sl].astype(F32), k_ref[:, sl].astype(F32),
                                     b, bm, bl)
        vb = v_ref[:, sl]
        a = pl.dot(q1, k1, trans_b=True)
        a = jnp.where(causal, a, 0.0).astype(BF16)
        o1 = jnp.dot(a, vb, preferred_element_type=F32)
        decay = jnp.exp(bl)
        for blk in range(CHUNK // sub):
            rs = slice(blk * sub, (blk + 1) * sub)
            q2b, k2b, vbb = q2[rs], k2[rs], vb[rs]
            rhs = jnp.concatenate([jnp.concatenate([vbb, zeros_blk], axis=1),
                                   jnp.concatenate([zeros_blk, ones_blk], axis=1)], axis=0)
            acc = o1[rs]
            for j in range(per_sub):
                s = blk * per_sub + j
                st = s_ref[s, h]
                qm = jnp.where(own[j], q2b, 0.0).astype(BF16)
                acc = acc + jnp.dot(qm, st.astype(BF16), preferred_element_type=F32)
                drows = _decay_rows(decay[S_LEN * s:S_LEN * s + 1, :], rid)
                lhs = jnp.concatenate([jnp.where(own[j], k2b, 0.0), drows], axis=0).astype(BF16)
                ud = pl.dot(lhs, rhs, trans_a=True)
                sfin_ref[s, h] = ud[:, DV:] * st + ud[:, :DV]
            out_ref[rs, sl] = _gated_rmsnorm(acc, gn[:, sl], sg_ref[rs, sl].astype(F32))


def _group_mats():
    r = np.arange(CHUNK)
    seq, pos = r // S_LEN, r % S_LEN
    same = seq[:, None] == seq[None, :]
    cum = same & (pos[None, :] <= pos[:, None])
    midm = same & (pos[None, :] <= S_MID)
    lmat = np.concatenate([cum, midm, same], axis=0).astype(np.float32)
    lmat = np.concatenate([lmat, lmat, lmat], axis=1)
    return jnp.asarray(lmat, BF16), jnp.asarray(cum.astype(np.float32))


def _mix_group(proj, nseq, s0, cbuf, gn, cw):
    steps = nseq // GROUP
    seg = lambda: pl.BlockSpec((CHUNK, SEG), lambda i: (i, 0))
    const2 = lambda i: (0, 0)
    lmat, amask = _group_mats()
    return pl.pallas_call(
        _mix_group_kernel,
        grid=(steps,),
        in_specs=[seg() for _ in range(7)] + [
                  pl.BlockSpec((GROUP, H_A, DK, DV), lambda i: (i, 0, 0, 0)),
                  pl.BlockSpec((GROUP, CONV_W - 1, D_B), lambda i: (i, 0, 0)),
                  pl.BlockSpec((1, D_A), const2),
                  pl.BlockSpec((CONV_W, D_B), const2),
                  pl.BlockSpec((3 * CHUNK, 3 * CHUNK), const2),
                  pl.BlockSpec((CHUNK, CHUNK), const2)],
        out_specs=[pl.BlockSpec((CHUNK, D_MODEL), lambda i: (i, 0)),
                   pl.BlockSpec((GROUP, H_A, DK, DV), lambda i: (i, 0, 0, 0)),
                   pl.BlockSpec((GROUP, CONV_W - 1, D_B), lambda i: (i, 0, 0))],
        out_shape=[jax.ShapeDtypeStruct((nseq * S_LEN, D_MODEL), BF16),
                   jax.ShapeDtypeStruct((nseq, H_A, DK, DV), F32),
                   jax.ShapeDtypeStruct((nseq, CONV_W - 1, D_B), F32)],
        scratch_shapes=[pltpu.VMEM((SUBLANES * GROUP, D_B), F32), pltpu.VMEM((CHUNK, D_B), F32)],
        compiler_params=pltpu.CompilerParams(
            dimension_semantics=("arbitrary",), vmem_limit_bytes=VMEM_LIMIT),
        name="mix_group",
    )(*proj, s0, cbuf, gn.reshape(1, -1), cw, lmat, amask)


OUTPROJ_SUB = 512


def _outproj_kernel(x_ref, mix_ref, wo_ref, g0_ref, b0_ref, g1_ref, b1_ref, h_ref):
    for r in range(x_ref.shape[0] // OUTPROJ_SUB):
        rs = slice(r * OUTPROJ_SUB, (r + 1) * OUTPROJ_SUB)
        xn = _layernorm(x_ref[rs, :], g0_ref[...], b0_ref[...])
        m = jnp.dot(mix_ref[rs, :], wo_ref[...], preferred_element_type=F32)
        h_ref[rs, :] = _layernorm(ALPHA * xn + m, g1_ref[...], b1_ref[...])


def _outproj(x, mix, w_o, g0, b0, g1, b1, tm):
    rows = x.shape[0]
    assert rows % tm == 0 and tm % OUTPROJ_SUB == 0
    row = lambda a: a.reshape(1, -1)
    const = lambda i: (0, 0)
    vec = pl.BlockSpec((1, D_MODEL), const)
    return pl.pallas_call(
        _outproj_kernel,
        grid=(rows // tm,),
        in_specs=[pl.BlockSpec((tm, D_MODEL), lambda i: (i, 0)),
                  pl.BlockSpec((tm, D_MODEL), lambda i: (i, 0)),
                  pl.BlockSpec((D_MODEL, D_MODEL), const),
                  vec, vec, vec, vec],
        out_specs=pl.BlockSpec((tm, D_MODEL), lambda i: (i, 0)),
        out_shape=jax.ShapeDtypeStruct((rows, D_MODEL), F32),
        compiler_params=pltpu.CompilerParams(
            dimension_semantics=("arbitrary",), vmem_limit_bytes=VMEM_LIMIT),
        name="outproj",
    )(x, mix, w_o, row(g0), row(b0), row(g1), row(b1))


def _ffn_kernel(h_ref, wg_ref, wu_ref, wd_ref, g2_ref, b2_ref, o_ref, hb_ref):
    f = pl.program_id(1)

    @pl.when(f == 0)
    def _():
        h = h_ref[...]
        hb_ref[...] = h.astype(BF16)
        o_ref[...] = ALPHA * h

    hb = hb_ref[...]
    g = jnp.dot(hb, wg_ref[...], preferred_element_type=F32)
    u = jnp.dot(hb, wu_ref[...], preferred_element_type=F32)
    a = (_silu(g) * u).astype(BF16)
    o_ref[...] += jnp.dot(a, wd_ref[...], preferred_element_type=F32)

    @pl.when(f == pl.num_programs(1) - 1)
    def _():
        o_ref[...] = _layernorm(o_ref[...], g2_ref[...], b2_ref[...])


def _ffn(h, w_gate, w_up, w_down, g2, b2, tm, tf):
    rows = h.shape[0]
    assert rows % tm == 0 and D_FF % tf == 0
    row = lambda a: a.reshape(1, -1)
    vec = pl.BlockSpec((1, D_MODEL), lambda i, f: (0, 0))
    return pl.pallas_call(
        _ffn_kernel,
        grid=(rows // tm, D_FF // tf),
        in_specs=[pl.BlockSpec((tm, D_MODEL), lambda i, f: (i, 0)),
                  pl.BlockSpec((D_MODEL, tf), lambda i, f: (0, f)),
                  pl.BlockSpec((D_MODEL, tf), lambda i, f: (0, f)),
                  pl.BlockSpec((tf, D_MODEL), lambda i, f: (f, 0)),
                  vec, vec],
        out_specs=pl.BlockSpec((tm, D_MODEL), lambda i, f: (i, 0)),
        out_shape=jax.ShapeDtypeStruct((rows, D_MODEL), F32),
        scratch_shapes=[pltpu.VMEM((tm, D_MODEL), BF16)],
        compiler_params=pltpu.CompilerParams(
            dimension_semantics=("arbitrary", "arbitrary"), vmem_limit_bytes=VMEM_LIMIT),
        name="ffn",
    )(h, w_gate, w_up, w_down, row(g2), row(b2))


def kernel(x_prompt, x_sample, state_hgrn, state_conv, meta_tokens, ln0_g, ln0_b, w_in, b_f, lb_param, gnorm_g, conv_w, w_o, ln1_g, ln1_b, w_gate, w_up, w_down, ln2_g, ln2_b):
    bp, seq, _ = x_prompt.shape
    bs, dseq, _ = x_sample.shape
    assert dseq == S_LEN and seq % CHUNK == 0 and bs % GROUP == 0

    lb = jnp.cumsum(jax.nn.softmax(lb_param.astype(F32), axis=0), axis=0)[0]
    g0, b0 = ln0_g.astype(F32), ln0_b.astype(F32)

    xp = x_prompt.reshape(bp * seq, D_MODEL)
    xs = x_sample.reshape(bs * dseq, D_MODEL)
    xs_ext = jnp.concatenate(
        [xs, meta_tokens.astype(F32), jnp.zeros((CHUNK - N_META, D_MODEL), F32)], axis=0)

    proj_s, w_in_b = _inproj_cast(xs_ext, g0, b0, w_in[0], b_f[0], lb)
    proj_p = _inproj(xp, g0, b0, w_in_b, b_f[0], lb, tm=INPROJ_TM)

    zero_st = jnp.zeros((H_A, DV, DK), F32)
    zero_tail = jnp.zeros((CONV_W - 1, D_B), F32)
    (_, st_meta, tail_meta), _ = _mix_seq(
        proj_s, bs * dseq, 1, CHUNK, CHUNK, zero_st, zero_tail, gnorm_g[0], conv_w[0],
        mid=N_META // 2, valid_rows=N_META, state_transposed_out=True)

    (mix_p, hgrn_p, conv_p), (w_o_b, wg_b, wu_b, wd_b) = _mix_seq(
        proj_p, 0, bp, seq, MIX_TB, st_meta[0], tail_meta[0], gnorm_g[0], conv_w[0],
        mid=CHUNK // 2, cast=(w_o[0], w_gate[0], w_up[0], w_down[0]))
    mix_s, hgrn_s, conv_s = _mix_group(
        proj_s, bs, state_hgrn[0], state_conv[0], gnorm_g[0], conv_w[0])

    h_p = _outproj(xp, mix_p, w_o_b, g0, b0, ln1_g[0], ln1_b[0], tm=OUTPROJ_TM)
    h_s = _outproj(xs, mix_s, w_o_b, g0, b0, ln1_g[0], ln1_b[0], tm=OUTPROJ_TM)

    y_p = _ffn(h_p, wg_b, wu_b, wd_b, ln2_g[0], ln2_b[0], tm=FFN_TM, tf=FFN_TF)
    y_s = _ffn(h_s, wg_b, wu_b, wd_b, ln2_g[0], ln2_b[0], tm=FFN_TM_SMALL, tf=FFN_TF_SMALL)

    return (y_p.reshape(bp, seq, D_MODEL), y_s.reshape(bs, dseq, D_MODEL),
            hgrn_p[None], conv_p[None], hgrn_s[None], conv_s[None])
```

```python
import functools

import numpy as np
import jax
import jax.numpy as jnp
from jax.experimental import pallas as pl
from jax.experimental.pallas import tpu as pltpu

F32 = jnp.float32
BF16 = jnp.bfloat16

D_MODEL = 2048
D_A = 1024
D_B = 1024
DK = 128
DV = 128
H_A = 8
SEG = 1024
N_SEG = 7
N_META = 16
CHUNK = 64
D_FF = 5632
CONV_W = 3
ALPHA = 2.0 ** 0.25
LN_EPS = 1e-5
RMS_EPS = 1e-6

V7X_VMEM_BYTES = 64 * 1024 * 1024
SUBLANES = 8
BF16_ROWS = 16
VMEM_LIMIT = V7X_VMEM_BYTES - 2 * 1024 * 1024

INPROJ_TM = 512
MIX_TB = 512
OUTPROJ_TM = 1024
FFN_TM = 1024
FFN_TM_SMALL = 512
FFN_TF = 512

CONV_TAIL = SUBLANES - (CONV_W - 1)


def _layernorm(x, g, b):
    mu = jnp.mean(x, axis=-1, keepdims=True)
    xc = x - mu
    var = jnp.mean(xc * xc, axis=-1, keepdims=True)
    return xc * jax.lax.rsqrt(var + LN_EPS) * g + b


def _silu(x):
    return x * (1.0 / (1.0 + jnp.exp(-x)))


INPROJ_SUB = 256


def _inproj_kernel(x_ref, g0_ref, b0_ref, w_ref, bf_ref, lb_ref,
                   q_ref, lf_ref, k_ref, v_ref, sg_ref, gb_ref, u_ref):
    for r in range(x_ref.shape[0] // INPROJ_SUB):
        rs = slice(r * INPROJ_SUB, (r + 1) * INPROJ_SUB)
        xn = _layernorm(x_ref[rs, :], g0_ref[...], b0_ref[...]).astype(BF16)

        def seg(j, xn=xn):
            return jnp.dot(xn, w_ref[:, j * SEG:(j + 1) * SEG], preferred_element_type=F32)

        u_ref[rs, :] = seg(5) * seg(6)
        q_ref[rs, :] = _silu(seg(0)).astype(BF16)

        lf, kk = _forget_gate(seg(1) + bf_ref[...], lb_ref[...])
        lf_ref[rs, :] = lf
        k_ref[rs, :] = kk.astype(BF16)

        sg_ref[rs, :] = _silu(seg(3)).astype(BF16)
        gb_ref[rs, :] = seg(4).astype(BF16)
        v_ref[rs, :] = seg(2).astype(BF16)


def _inproj(x, g0, b0, w_in, b_f, lb, tm):
    rows = x.shape[0]
    assert rows % tm == 0 and tm % INPROJ_SUB == 0
    row = lambda a: a.reshape(1, -1)
    const = lambda i: (0, 0)
    out = lambda: pl.BlockSpec((tm, SEG), lambda i: (i, 0))
    sds = lambda dt: jax.ShapeDtypeStruct((rows, SEG), dt)
    return pl.pallas_call(
        _inproj_kernel,
        grid=(rows // tm,),
        in_specs=[
            pl.BlockSpec((tm, D_MODEL), lambda i: (i, 0)),
            pl.BlockSpec((1, D_MODEL), const),
            pl.BlockSpec((1, D_MODEL), const),
            pl.BlockSpec((D_MODEL, N_SEG * SEG), const, pipeline_mode=pl.Buffered(1)),
            pl.BlockSpec((1, SEG), const),
            pl.BlockSpec((1, SEG), const),
        ],
        out_specs=[out() for _ in range(7)],
        out_shape=[sds(BF16), sds(F32), sds(BF16), sds(BF16), sds(BF16), sds(BF16), sds(F32)],
        compiler_params=pltpu.CompilerParams(
            dimension_semantics=("arbitrary",), vmem_limit_bytes=VMEM_LIMIT),
        name="inproj",
    )(x, row(g0), row(b0), w_in, row(b_f), row(lb))


def _forget_gate(z, lb):
    e = jnp.exp(-jnp.abs(z))
    r = 1.0 / (1.0 + e)
    er = e * r
    pos = z >= 0.0
    return jnp.log(lb + (1.0 - lb) * jnp.where(pos, r, er)), (1.0 - lb) * jnp.where(pos, er, r)


def _inproj_cast_kernel(x_ref, g0_ref, b0_ref, w_ref, bf_ref, lb_ref,
                        q_ref, lf_ref, k_ref, v_ref, sg_ref, gb_ref, u_ref, wb_ref,
                        xn_scr, gc_scr):
    j = pl.program_id(0)

    @pl.when(j == 0)
    def _():
        xn_scr[...] = _layernorm(x_ref[...], g0_ref[...], b0_ref[...]).astype(BF16)

    wb = w_ref[...].astype(BF16)
    wb_ref[...] = wb
    acc = jnp.dot(xn_scr[...], wb, preferred_element_type=F32)

    @pl.when(j == 0)
    def _():
        q_ref[...] = _silu(acc).astype(BF16)

    @pl.when(j == 1)
    def _():
        lf, kk = _forget_gate(acc + bf_ref[...], lb_ref[...])
        lf_ref[...] = lf
        k_ref[...] = kk.astype(BF16)

    @pl.when(j == 2)
    def _():
        v_ref[...] = acc.astype(BF16)

    @pl.when(j == 3)
    def _():
        sg_ref[...] = _silu(acc).astype(BF16)

    @pl.when(j == 4)
    def _():
        gb_ref[...] = acc.astype(BF16)

    @pl.when(j == 5)
    def _():
        gc_scr[...] = acc

    @pl.when(j == 6)
    def _():
        u_ref[...] = gc_scr[...] * acc


def _inproj_cast(x, g0, b0, w_in, b_f, lb):
    rows = x.shape[0]
    row = lambda a: a.reshape(1, -1)
    const = lambda j: (0, 0)
    out = lambda: pl.BlockSpec((rows, SEG), const)
    sds = lambda dt: jax.ShapeDtypeStruct((rows, SEG), dt)
    wseg = lambda: pl.BlockSpec((D_MODEL, SEG), lambda j: (0, j))
    res = pl.pallas_call(
        _inproj_cast_kernel,
        grid=(N_SEG,),
        in_specs=[pl.BlockSpec((rows, D_MODEL), const),
                  pl.BlockSpec((1, D_MODEL), const),
                  pl.BlockSpec((1, D_MODEL), const),
                  wseg(),
                  pl.BlockSpec((1, SEG), const),
                  pl.BlockSpec((1, SEG), const)],
        out_specs=[out() for _ in range(7)] + [wseg()],
        out_shape=[sds(BF16), sds(F32), sds(BF16), sds(BF16), sds(BF16), sds(BF16), sds(F32),
                   jax.ShapeDtypeStruct(w_in.shape, BF16)],
        scratch_shapes=[pltpu.VMEM((rows, D_MODEL), BF16), pltpu.VMEM((rows, SEG), F32)],
        compiler_params=pltpu.CompilerParams(
            dimension_semantics=("arbitrary",), vmem_limit_bytes=VMEM_LIMIT),
        name="inproj_cast",
    )(x, row(g0), row(b0), w_in, row(b_f), row(lb))
    return res[:7], res[7]


def _split3(x):
    hi = x.astype(BF16)
    r1 = x - hi.astype(F32)
    mid = r1.astype(BF16)
    lo = (r1 - mid.astype(F32)).astype(BF16)
    return jnp.concatenate([hi, mid, lo], axis=0)


def _head_tiles(qt, kk, b, bm, bl):
    q1 = (qt * jnp.exp(b - bm)).astype(BF16)
    k1 = (kk * jnp.exp(bm - b)).astype(BF16)
    q2 = qt * jnp.exp(b)
    k2 = kk * jnp.exp(bl - b)
    return q1, k1, q2, k2


DECAY_ROWS = 16


def _decay_rows(d, rid):
    d_hi = d.astype(BF16).astype(F32)
    d_r = d - d_hi
    d_mid = d_r.astype(BF16).astype(F32)
    return jnp.where(rid == 0, d_hi, jnp.where(rid == 1, d_mid,
                     jnp.where(rid == 2, d_r - d_mid, 0.0)))


def _gated_rmsnorm(o, gn, sg):
    ms = jnp.mean(o * o, axis=-1, keepdims=True)
    return (o * jax.lax.rsqrt(ms + RMS_EPS) * gn * sg).astype(BF16)


def _mix_seq_kernel(q_ref, lf_ref, k_ref, v_ref, sg_ref, gb_ref, u_ref,
                    st0_ref, tail0_ref, gn_ref, cw_ref, tmat_ref, amask_ref, *refs,
                    tb, mid, valid_rows, state_transposed_out, n_cast):
    cast_in, refs = refs[:n_cast], refs[n_cast:]
    out_ref, sfin_ref, cfin_ref = refs[:3]
    cast_out, (st_scr, ubuf) = refs[3:3 + n_cast], refs[3 + n_cast:]
    for src, dst in zip(cast_in, cast_out, strict=True):
        dst[...] = src[...].astype(BF16)
    t = pl.program_id(1)

    @pl.when(t == 0)
    def _():
        st_scr[...] = st0_ref[...]
        ubuf[CONV_TAIL:SUBLANES, :] = tail0_ref[...]

    u = u_ref[...]
    ubuf[SUBLANES:SUBLANES + tb, :] = u
    cw = cw_ref[...]
    conv = (cw[0:1] * ubuf[CONV_TAIL:CONV_TAIL + tb, :]
            + cw[1:2] * ubuf[CONV_TAIL + 1:CONV_TAIL + 1 + tb, :] + cw[2:3] * u)
    out_ref[:, D_A:] = (gb_ref[...].astype(F32) * conv).astype(BF16)
    last = valid_rows if valid_rows is not None else tb
    ubuf[CONV_TAIL:SUBLANES, :] = ubuf[CONV_TAIL + last:SUBLANES + last, :]

    tmat = tmat_ref[...]
    causal = amask_ref[...] > 0.0
    gn = gn_ref[...]
    if valid_rows is not None:
        rowmask = jax.lax.broadcasted_iota(jnp.int32, (CHUNK, SEG), 0) < valid_rows
        rowmask_h = jax.lax.broadcasted_iota(jnp.int32, (CHUNK, DK), 0) < valid_rows
    for ci in range(tb // CHUNK):
        rows = slice(ci * CHUNK, (ci + 1) * CHUNK)
        lf = lf_ref[rows, :]
        if valid_rows is not None:
            lf = jnp.where(rowmask, lf, 0.0)
        b_all = jnp.dot(tmat, _split3(lf), preferred_element_type=F32)
        for h in range(H_A):
            sl = slice(h * DK, (h + 1) * DK)
            b = b_all[:, sl]
            bm = b[mid:mid + 1, :]
            bl = b[CHUNK - 1:CHUNK, :]
            kk = k_ref[rows, sl].astype(F32)
            if valid_rows is not None:
                kk = jnp.where(rowmask_h, kk, 0.0)
            q1, k1, q2, k2 = _head_tiles(q_ref[rows, sl].astype(F32), kk, b, bm, bl)
            vb = v_ref[rows, sl]
            st = st_scr[h]
            a = pl.dot(q1, k1, trans_b=True)
            a = jnp.where(causal, a, 0.0).astype(BF16)
            o = (jnp.dot(a, vb, preferred_element_type=F32)
                 + pl.dot(q2.astype(BF16), st.astype(BF16), trans_b=True))
            st_scr[h] = st * jnp.exp(bl) + pl.dot(vb, k2.astype(BF16), trans_a=True)
            out_ref[rows, sl] = _gated_rmsnorm(o, gn[:, sl], sg_ref[rows, sl].astype(F32))

    @pl.when(t == pl.num_programs(1) - 1)
    def _():
        for h in range(H_A):
            sfin_ref[0, h] = st_scr[h] if state_transposed_out else st_scr[h].T
        cfin_ref[0] = ubuf[CONV_TAIL:SUBLANES, :]


def _mix_seq(proj, row0, nseq, seq_len, tb, st0, tail0, gn, cw, *, mid, valid_rows=None,
             state_transposed_out=False, cast=()):
    nt = seq_len // tb
    cast_specs = []
    for w in cast:
        assert w.shape[0] % (BF16_ROWS * nseq * nt) == 0
        cast_specs.append(pl.BlockSpec((w.shape[0] // (nseq * nt), w.shape[1]),
                                       lambda b, t: (b * nt + t, 0)))
    assert seq_len % tb == 0 and row0 % tb == 0 and tb % CHUNK == 0
    rb0 = row0 // tb
    seg = lambda: pl.BlockSpec((tb, SEG), lambda b, t: (rb0 + b * nt + t, 0))
    const2 = lambda b, t: (0, 0)
    tril = np.tril(np.ones((CHUNK, CHUNK), np.float32))
    tmat = jnp.asarray(np.concatenate([tril, tril, tril], axis=1), BF16)
    kern = functools.partial(_mix_seq_kernel, tb=tb, mid=mid, valid_rows=valid_rows,
                             state_transposed_out=state_transposed_out, n_cast=len(cast))
    res = pl.pallas_call(
        kern,
        grid=(nseq, nt),
        in_specs=[seg() for _ in range(7)] + [
                  pl.BlockSpec((H_A, DV, DK), lambda b, t: (0, 0, 0)),
                  pl.BlockSpec((CONV_W - 1, D_B), const2),
                  pl.BlockSpec((1, D_A), const2),
                  pl.BlockSpec((CONV_W, D_B), const2),
                  pl.BlockSpec((CHUNK, 3 * CHUNK), const2),
                  pl.BlockSpec((CHUNK, CHUNK), const2)] + cast_specs,
        out_specs=[pl.BlockSpec((tb, D_MODEL), lambda b, t: (b * nt + t, 0)),
                   pl.BlockSpec((1, H_A, DK, DV), lambda b, t: (b, 0, 0, 0)),
                   pl.BlockSpec((1, CONV_W - 1, D_B), lambda b, t: (b, 0, 0))] + cast_specs,
        out_shape=[jax.ShapeDtypeStruct((nseq * seq_len, D_MODEL), BF16),
                   jax.ShapeDtypeStruct((nseq, H_A, DK, DV), F32),
                   jax.ShapeDtypeStruct((nseq, CONV_W - 1, D_B), F32)]
        + [jax.ShapeDtypeStruct(w.shape, BF16) for w in cast],
        scratch_shapes=[pltpu.VMEM((H_A, DV, DK), F32), pltpu.VMEM((SUBLANES + tb, D_B), F32)],
        compiler_params=pltpu.CompilerParams(
            dimension_semantics=("arbitrary", "arbitrary"), vmem_limit_bytes=VMEM_LIMIT),
        name="mix_seq",
    )(*proj, st0, tail0, gn.reshape(1, -1), cw, tmat, jnp.asarray(tril), *cast)
    return res[:3], res[3:]


GROUP = 16
S_LEN = 4
S_MID = 2


def _mix_group_kernel(q_ref, lf_ref, k_ref, v_ref, sg_ref, gb_ref, u_ref,
                      s_ref, cbuf_ref, gn_ref, cw_ref, lmat_ref, amask_ref,
                      out_ref, sfin_ref, cfin_ref, full_scr, y_scr):
    cw = cw_ref[...]
    nb = CONV_W - 1
    for s in range(GROUP):
        r0 = SUBLANES * s
        full_scr[r0:r0 + nb, :] = cbuf_ref[s]
        full_scr[r0 + nb:r0 + nb + S_LEN, :] = u_ref[S_LEN * s:S_LEN * (s + 1), :]
    for s in range(GROUP):
        r0 = SUBLANES * s
        f0 = full_scr[r0:r0 + S_LEN, :]
        f1 = full_scr[r0 + 1:r0 + 1 + S_LEN, :]
        f2 = full_scr[r0 + 2:r0 + 2 + S_LEN, :]
        y_scr[S_LEN * s:S_LEN * (s + 1), :] = cw[0:1] * f0 + cw[1:2] * f1 + cw[2:3] * f2
        cfin_ref[s] = full_scr[r0 + S_LEN:r0 + S_LEN + nb, :]
    out_ref[:, D_A:] = (gb_ref[...].astype(F32) * y_scr[...]).astype(BF16)

    b3 = jnp.dot(lmat_ref[...], _split3(lf_ref[...]), preferred_element_type=F32)
    causal = amask_ref[...] > 0.0
    gn = gn_ref[...]
    sub = BF16_ROWS
    per_sub = sub // S_LEN
    rid = jax.lax.broadcasted_iota(jnp.int32, (sub, DK), 0)
    own = [(rid >= S_LEN * j) & (rid < S_LEN * (j + 1)) for j in range(per_sub)]
    ones_blk = jnp.ones((sub, DV), BF16)
    zeros_blk = jnp.zeros((sub, DV), BF16)
    for h in range(H_A):
        sl = slice(h * DK, (h + 1) * DK)
        b = b3[0:CHUNK, sl]
        bm = b3[CHUNK:2 * CHUNK, sl]
        bl = b3[2 * CHUNK:3 * CHUNK, sl]
        q1, k1, q2, k2 = _head_tiles(q_ref[:, sl].astype(F32), k_ref[:, sl].astype(F32),
                                     b, bm, bl)
        vb = v_ref[:, sl]
        a = pl.dot(q1, k1, trans_b=True)
        a = jnp.where(causal, a, 0.0).astype(BF16)
        o1 = jnp.dot(a, vb, preferred_element_type=F32)
        decay = jnp.exp(bl)
        for blk in range(CHUNK // sub):
            rs = slice(blk * sub, (blk + 1) * sub)
            q2b, k2b, vbb = q2[rs], k2[rs], vb[rs]
            rhs = jnp.concatenate([jnp.concatenate([vbb, zeros_blk], axis=1),
                                   jnp.concatenate([zeros_blk, ones_blk], axis=1)], axis=0)
            acc = o1[rs]
            for j in range(per_sub):
                s = blk * per_sub + j
                st = s_ref[s, h]
                qm = jnp.where(own[j], q2b, 0.0).astype(BF16)
                acc = acc + jnp.dot(qm, st.astype(BF16), preferred_element_type=F32)
                drows = _decay_rows(decay[S_LEN * s:S_LEN * s + 1, :], rid)
                lhs = jnp.concatenate([jnp.where(own[j], k2b, 0.0), drows], axis=0).astype(BF16)
                ud = pl.dot(lhs, rhs, trans_a=True)
                sfin_ref[s, h] = ud[:, DV:] * st + ud[:, :DV]
            out_ref[rs, sl] = _gated_rmsnorm(acc, gn[:, sl], sg_ref[rs, sl].astype(F32))


def _group_mats():
    r = np.arange(CHUNK)
    seq, pos = r // S_LEN, r % S_LEN
    same = seq[:, None] == seq[None, :]
    cum = same & (pos[None, :] <= pos[:, None])
    midm = same & (pos[None, :] <= S_MID)
    lmat = np.concatenate([cum, midm, same], axis=0).astype(np.float32)
    lmat = np.concatenate([lmat, lmat, lmat], axis=1)
    return jnp.asarray(lmat, BF16), jnp.asarray(cum.astype(np.float32))


def _mix_group(proj, nseq, s0, cbuf, gn, cw):
    steps = nseq // GROUP
    seg = lambda: pl.BlockSpec((CHUNK, SEG), lambda i: (i, 0))
    const2 = lambda i: (0, 0)
    lmat, amask = _group_mats()
    return pl.pallas_call(
        _mix_group_kernel,
        grid=(steps,),
        in_specs=[seg() for _ in range(7)] + [
                  pl.BlockSpec((GROUP, H_A, DK, DV), lambda i: (i, 0, 0, 0)),
                  pl.BlockSpec((GROUP, CONV_W - 1, D_B), lambda i: (i, 0, 0)),
                  pl.BlockSpec((1, D_A), const2),
                  pl.BlockSpec((CONV_W, D_B), const2),
                  pl.BlockSpec((3 * CHUNK, 3 * CHUNK), const2),
                  pl.BlockSpec((CHUNK, CHUNK), const2)],
        out_specs=[pl.BlockSpec((CHUNK, D_MODEL), lambda i: (i, 0)),
                   pl.BlockSpec((GROUP, H_A, DK, DV), lambda i: (i, 0, 0, 0)),
                   pl.BlockSpec((GROUP, CONV_W - 1, D_B), lambda i: (i, 0, 0))],
        out_shape=[jax.ShapeDtypeStruct((nseq * S_LEN, D_MODEL), BF16),
                   jax.ShapeDtypeStruct((nseq, H_A, DK, DV), F32),
                   jax.ShapeDtypeStruct((nseq, CONV_W - 1, D_B), F32)],
        scratch_shapes=[pltpu.VMEM((SUBLANES * GROUP, D_B), F32), pltpu.VMEM((CHUNK, D_B), F32)],
        compiler_params=pltpu.CompilerParams(
            dimension_semantics=("arbitrary",), vmem_limit_bytes=VMEM_LIMIT),
        name="mix_group",
    )(*proj, s0, cbuf, gn.reshape(1, -1), cw, lmat, amask)


OUTPROJ_SUB = 512


def _outproj_kernel(x_ref, mix_ref, wo_ref, g0_ref, b0_ref, g1_ref, b1_ref, h_ref):
    for r in range(x_ref.shape[0] // OUTPROJ_SUB):
        rs = slice(r * OUTPROJ_SUB, (r + 1) * OUTPROJ_SUB)
        xn = _layernorm(x_ref[rs, :], g0_ref[...], b0_ref[...])
        m = jnp.dot(mix_ref[rs, :], wo_ref[...], preferred_element_type=F32)
        h_ref[rs, :] = _layernorm(ALPHA * xn + m, g1_ref[...], b1_ref[...])


def _outproj(x, mix, w_o, g0, b0, g1, b1, tm):
    rows = x.shape[0]
    tm = min(tm, rows)
    assert rows % tm == 0 and tm % OUTPROJ_SUB == 0
    row = lambda a: a.reshape(1, -1)
    const = lambda i: (0, 0)
    vec = pl.BlockSpec((1, D_MODEL), const)
    return pl.pallas_call(
        _outproj_kernel,
        grid=(rows // tm,),
        in_specs=[pl.BlockSpec((tm, D_MODEL), lambda i: (i, 0)),
                  pl.BlockSpec((tm, D_MODEL), lambda i: (i, 0)),
                  pl.BlockSpec((D_MODEL, D_MODEL), const, pipeline_mode=pl.Buffered(1)),
                  vec, vec, vec, vec],
        out_specs=pl.BlockSpec((tm, D_MODEL), lambda i: (i, 0)),
        out_shape=jax.ShapeDtypeStruct((rows, D_MODEL), F32),
        compiler_params=pltpu.CompilerParams(
            dimension_semantics=("arbitrary",), vmem_limit_bytes=VMEM_LIMIT),
        name="outproj",
    )(x, mix, w_o, row(g0), row(b0), row(g1), row(b1))


def _ffn_kernel(h_ref, wg_ref, wu_ref, wd_ref, g2_ref, b2_ref, o_ref, hb_ref):
    f = pl.program_id(1)

    @pl.when(f == 0)
    def _():
        h = h_ref[...]
        hb_ref[...] = h.astype(BF16)
        o_ref[...] = ALPHA * h

    hb = hb_ref[...]
    g = jnp.dot(hb, wg_ref[...], preferred_element_type=F32)
    u = jnp.dot(hb, wu_ref[...], preferred_element_type=F32)
    a = (_silu(g) * u).astype(BF16)
    o_ref[...] += jnp.dot(a, wd_ref[...], preferred_element_type=F32)

    @pl.when(f == pl.num_programs(1) - 1)
    def _():
        o_ref[...] = _layernorm(o_ref[...], g2_ref[...], b2_ref[...])


def _ffn(h, w_gate, w_up, w_down, g2, b2, tm, tf):
    rows = h.shape[0]
    assert rows % tm == 0 and D_FF % tf == 0
    row = lambda a: a.reshape(1, -1)
    vec = pl.BlockSpec((1, D_MODEL), lambda i, f: (0, 0))
    return pl.pallas_call(
        _ffn_kernel,
        grid=(rows // tm, D_FF // tf),
        in_specs=[pl.BlockSpec((tm, D_MODEL), lambda i, f: (i, 0)),
                  pl.BlockSpec((D_MODEL, tf), lambda i, f: (0, f)),
                  pl.BlockSpec((D_MODEL, tf), lambda i, f: (0, f)),
                  pl.BlockSpec((tf, D_MODEL), lambda i, f: (f, 0)),
                  vec, vec],
        out_specs=pl.BlockSpec((tm, D_MODEL), lambda i, f: (i, 0)),
        out_shape=jax.ShapeDtypeStruct((rows, D_MODEL), F32),
        scratch_shapes=[pltpu.VMEM((tm, D_MODEL), BF16)],
        compiler_params=pltpu.CompilerParams(
            dimension_semantics=("arbitrary", "arbitrary"), vmem_limit_bytes=VMEM_LIMIT),
        name="ffn",
    )(h, w_gate, w_up, w_down, row(g2), row(b2))


def kernel(x_prompt, x_sample, state_hgrn, state_conv, meta_tokens, ln0_g, ln0_b, w_in, b_f, lb_param, gnorm_g, conv_w, w_o, ln1_g, ln1_b, w_gate, w_up, w_down, ln2_g, ln2_b):
    bp, seq, _ = x_prompt.shape
    bs, dseq, _ = x_sample.shape
    assert dseq == S_LEN and seq % CHUNK == 0 and bs % GROUP == 0

    lb = jnp.cumsum(jax.nn.softmax(lb_param.astype(F32), axis=0), axis=0)[0]
    g0, b0 = ln0_g.astype(F32), ln0_b.astype(F32)

    xp = x_prompt.reshape(bp * seq, D_MODEL)
    xs = x_sample.reshape(bs * dseq, D_MODEL)
    xs_ext = jnp.concatenate(
        [xs, meta_tokens.astype(F32), jnp.zeros((CHUNK - N_META, D_MODEL), F32)], axis=0)

    proj_s, w_in_b = _inproj_cast(xs_ext, g0, b0, w_in[0], b_f[0], lb)
    proj_p = _inproj(xp, g0, b0, w_in_b, b_f[0], lb, tm=INPROJ_TM)

    zero_st = jnp.zeros((H_A, DV, DK), F32)
    zero_tail = jnp.zeros((CONV_W - 1, D_B), F32)
    (_, st_meta, tail_meta), _ = _mix_seq(
        proj_s, bs * dseq, 1, CHUNK, CHUNK, zero_st, zero_tail, gnorm_g[0], conv_w[0],
        mid=N_META // 2, valid_rows=N_META, state_transposed_out=True)

    (mix_p, hgrn_p, conv_p), (w_o_b, wg_b, wu_b, wd_b) = _mix_seq(
        proj_p, 0, bp, seq, MIX_TB, st_meta[0], tail_meta[0], gnorm_g[0], conv_w[0],
        mid=CHUNK // 2, cast=(w_o[0], w_gate[0], w_up[0], w_down[0]))
    mix_s, hgrn_s, conv_s = _mix_group(
        proj_s, bs, state_hgrn[0], state_conv[0], gnorm_g[0], conv_w[0])

    h_p = _outproj(xp, mix_p, w_o_b, g0, b0, ln1_g[0], ln1_b[0], tm=OUTPROJ_TM)
    h_s = _outproj(xs, mix_s, w_o_b, g0, b0, ln1_g[0], ln1_b[0], tm=OUTPROJ_TM)

    y_p = _ffn(h_p, wg_b, wu_b, wd_b, ln2_g[0], ln2_b[0], tm=FFN_TM, tf=FFN_TF)
    y_s = _ffn(h_s, wg_b, wu_b, wd_b, ln2_g[0], ln2_b[0], tm=FFN_TM_SMALL, tf=FFN_TF)

    return (y_p.reshape(bp, seq, D_MODEL), y_s.reshape(bs, dseq, D_MODEL),
            hgrn_p[None], conv_p[None], hgrn_s[None], conv_s[None])
```

```python
import functools

import numpy as np
import jax
import jax.numpy as jnp
from jax.experimental import pallas as pl
from jax.experimental.pallas import tpu as pltpu

F32 = jnp.float32
BF16 = jnp.bfloat16

D_MODEL = 2048
D_A = 1024
D_B = 1024
DK = 128
DV = 128
H_A = 8
SEG = 1024
N_SEG = 7
N_META = 16
CHUNK = 64
D_FF = 5632
CONV_W = 3
ALPHA = 2.0 ** 0.25
LN_EPS = 1e-5
RMS_EPS = 1e-6

V7X_VMEM_BYTES = 64 * 1024 * 1024
SUBLANES = 8
BF16_ROWS = 16
VMEM_LIMIT = V7X_VMEM_BYTES - 2 * 1024 * 1024

INPROJ_TM = 512
MIX_TB = 512
OUTPROJ_TM = 512
FFN_TM = 1024
FFN_TM_SMALL = 512
FFN_TF = 512

CONV_TAIL = SUBLANES - (CONV_W - 1)


def _layernorm(x, g, b):
    mu = jnp.mean(x, axis=-1, keepdims=True)
    xc = x - mu
    var = jnp.mean(xc * xc, axis=-1, keepdims=True)
    return xc * jax.lax.rsqrt(var + LN_EPS) * g + b


def _silu(x):
    return x * (1.0 / (1.0 + jnp.exp(-x)))


INPROJ_SUB = 256


def _inproj_kernel(x_ref, g0_ref, b0_ref, w_ref, bf_ref, lb_ref,
                   q_ref, lf_ref, k_ref, v_ref, sg_ref, gb_ref, u_ref):
    for r in range(x_ref.shape[0] // INPROJ_SUB):
        rs = slice(r * INPROJ_SUB, (r + 1) * INPROJ_SUB)
        xn = _layernorm(x_ref[rs, :], g0_ref[...], b0_ref[...]).astype(BF16)

        def seg(j, xn=xn):
            return jnp.dot(xn, w_ref[:, j * SEG:(j + 1) * SEG], preferred_element_type=F32)

        u_ref[rs, :] = seg(5) * seg(6)
        q_ref[rs, :] = _silu(seg(0)).astype(BF16)

        lf, kk = _forget_gate(seg(1) + bf_ref[...], lb_ref[...])
        lf_ref[rs, :] = lf
        k_ref[rs, :] = kk.astype(BF16)

        sg_ref[rs, :] = _silu(seg(3)).astype(BF16)
        gb_ref[rs, :] = seg(4).astype(BF16)
        v_ref[rs, :] = seg(2).astype(BF16)


def _inproj(x, g0, b0, w_in, b_f, lb, tm):
    rows = x.shape[0]
    assert rows % tm == 0 and tm % INPROJ_SUB == 0
    row = lambda a: a.reshape(1, -1)
    const = lambda i: (0, 0)
    out = lambda: pl.BlockSpec((tm, SEG), lambda i: (i, 0))
    sds = lambda dt: jax.ShapeDtypeStruct((rows, SEG), dt)
    return pl.pallas_call(
        _inproj_kernel,
        grid=(rows // tm,),
        in_specs=[
            pl.BlockSpec((tm, D_MODEL), lambda i: (i, 0)),
            pl.BlockSpec((1, D_MODEL), const),
            pl.BlockSpec((1, D_MODEL), const),
            pl.BlockSpec((D_MODEL, N_SEG * SEG), const, pipeline_mode=pl.Buffered(1)),
            pl.BlockSpec((1, SEG), const),
            pl.BlockSpec((1, SEG), const),
        ],
        out_specs=[out() for _ in range(7)],
        out_shape=[sds(BF16), sds(F32), sds(BF16), sds(BF16), sds(BF16), sds(BF16), sds(F32)],
        compiler_params=pltpu.CompilerParams(
            dimension_semantics=("arbitrary",), vmem_limit_bytes=VMEM_LIMIT),
        name="inproj",
    )(x, row(g0), row(b0), w_in, row(b_f), row(lb))


def _forget_gate(z, lb):
    e = jnp.exp(-jnp.abs(z))
    r = 1.0 / (1.0 + e)
    er = e * r
    pos = z >= 0.0
    return jnp.log(lb + (1.0 - lb) * jnp.where(pos, r, er)), (1.0 - lb) * jnp.where(pos, er, r)


def _inproj_cast_kernel(x_ref, g0_ref, b0_ref, w_ref, bf_ref, lb_ref,
                        q_ref, lf_ref, k_ref, v_ref, sg_ref, gb_ref, u_ref, wb_ref,
                        xn_scr, gc_scr):
    j = pl.program_id(0)

    @pl.when(j == 0)
    def _():
        xn_scr[...] = _layernorm(x_ref[...], g0_ref[...], b0_ref[...]).astype(BF16)

    wb = w_ref[...].astype(BF16)
    wb_ref[...] = wb
    acc = jnp.dot(xn_scr[...], wb, preferred_element_type=F32)

    @pl.when(j == 0)
    def _():
        q_ref[...] = _silu(acc).astype(BF16)

    @pl.when(j == 1)
    def _():
        lf, kk = _forget_gate(acc + bf_ref[...], lb_ref[...])
        lf_ref[...] = lf
        k_ref[...] = kk.astype(BF16)

    @pl.when(j == 2)
    def _():
        v_ref[...] = acc.astype(BF16)

    @pl.when(j == 3)
    def _():
        sg_ref[...] = _silu(acc).astype(BF16)

    @pl.when(j == 4)
    def _():
        gb_ref[...] = acc.astype(BF16)

    @pl.when(j == 5)
    def _():
        gc_scr[...] = acc

    @pl.when(j == 6)
    def _():
        u_ref[...] = gc_scr[...] * acc


def _inproj_cast(x, g0, b0, w_in, b_f, lb):
    rows = x.shape[0]
    row = lambda a: a.reshape(1, -1)
    const = lambda j: (0, 0)
    out = lambda: pl.BlockSpec((rows, SEG), const)
    sds = lambda dt: jax.ShapeDtypeStruct((rows, SEG), dt)
    wseg = lambda: pl.BlockSpec((D_MODEL, SEG), lambda j: (0, j))
    res = pl.pallas_call(
        _inproj_cast_kernel,
        grid=(N_SEG,),
        in_specs=[pl.BlockSpec((rows, D_MODEL), const),
                  pl.BlockSpec((1, D_MODEL), const),
                  pl.BlockSpec((1, D_MODEL), const),
                  wseg(),
                  pl.BlockSpec((1, SEG), const),
                  pl.BlockSpec((1, SEG), const)],
        out_specs=[out() for _ in range(7)] + [wseg()],
        out_shape=[sds(BF16), sds(F32), sds(BF16), sds(BF16), sds(BF16), sds(BF16), sds(F32),
                   jax.ShapeDtypeStruct(w_in.shape, BF16)],
        scratch_shapes=[pltpu.VMEM((rows, D_MODEL), BF16), pltpu.VMEM((rows, SEG), F32)],
        compiler_params=pltpu.CompilerParams(
            dimension_semantics=("arbitrary",), vmem_limit_bytes=VMEM_LIMIT),
        name="inproj_cast",
    )(x, row(g0), row(b0), w_in, row(b_f), row(lb))
    return res[:7], res[7]


def _split3(x):
    hi = x.astype(BF16)
    r1 = x - hi.astype(F32)
    mid = r1.astype(BF16)
    lo = (r1 - mid.astype(F32)).astype(BF16)
    return jnp.concatenate([hi, mid, lo], axis=0)


def _cumsum_rows(x):
    rows, lanes = x.shape
    n = rows // SUBLANES
    pos = jax.lax.broadcasted_iota(jnp.int32, x.shape, 0) % SUBLANES
    s = 1
    while s < SUBLANES:
        x = x + jnp.where(pos >= s, pltpu.roll(x, s, 0), 0.0)
        s *= 2
    x3 = x.reshape(n, SUBLANES, lanes)
    carry = jnp.zeros((1, lanes), F32)
    out = []
    for g in range(n):
        out.append(x3[g] + carry)
        carry = carry + x3[g, SUBLANES - 1:SUBLANES, :]
    return jnp.concatenate(out, axis=0)


def _head_tiles(qt, kk, b, bm, bl):
    q1 = (qt * jnp.exp(b - bm)).astype(BF16)
    k1 = (kk * jnp.exp(bm - b)).astype(BF16)
    q2 = qt * jnp.exp(b)
    k2 = kk * jnp.exp(bl - b)
    return q1, k1, q2, k2


DECAY_ROWS = 16


def _decay_rows(d, rid):
    d_hi = d.astype(BF16).astype(F32)
    d_r = d - d_hi
    d_mid = d_r.astype(BF16).astype(F32)
    return jnp.where(rid == 0, d_hi, jnp.where(rid == 1, d_mid,
                     jnp.where(rid == 2, d_r - d_mid, 0.0)))


def _gated_rmsnorm(o, gn, sg):
    ms = jnp.mean(o * o, axis=-1, keepdims=True)
    return (o * jax.lax.rsqrt(ms + RMS_EPS) * gn * sg).astype(BF16)


def _mix_seq_kernel(q_ref, lf_ref, k_ref, v_ref, sg_ref, gb_ref, u_ref,
                    st0_ref, tail0_ref, gn_ref, cw_ref, amask_ref, *refs,
                    tb, mid, valid_rows, state_transposed_out, n_cast):
    cast_in, refs = refs[:n_cast], refs[n_cast:]
    out_ref, sfin_ref, cfin_ref = refs[:3]
    cast_out, (st_scr, ubuf) = refs[3:3 + n_cast], refs[3 + n_cast:]
    for src, dst in zip(cast_in, cast_out, strict=True):
        dst[...] = src[...].astype(BF16)
    t = pl.program_id(1)

    @pl.when(t == 0)
    def _():
        st_scr[...] = st0_ref[...]
        ubuf[CONV_TAIL:SUBLANES, :] = tail0_ref[...]

    u = u_ref[...]
    ubuf[SUBLANES:SUBLANES + tb, :] = u
    cw = cw_ref[...]
    conv = (cw[0:1] * ubuf[CONV_TAIL:CONV_TAIL + tb, :]
            + cw[1:2] * ubuf[CONV_TAIL + 1:CONV_TAIL + 1 + tb, :] + cw[2:3] * u)
    out_ref[:, D_A:] = (gb_ref[...].astype(F32) * conv).astype(BF16)
    last = valid_rows if valid_rows is not None else tb
    ubuf[CONV_TAIL:SUBLANES, :] = ubuf[CONV_TAIL + last:SUBLANES + last, :]

    causal = amask_ref[...] > 0.0
    gn = gn_ref[...]
    if valid_rows is not None:
        rowmask = jax.lax.broadcasted_iota(jnp.int32, (CHUNK, SEG), 0) < valid_rows
        rowmask_h = jax.lax.broadcasted_iota(jnp.int32, (CHUNK, DK), 0) < valid_rows
    for ci in range(tb // CHUNK):
        rows = slice(ci * CHUNK, (ci + 1) * CHUNK)
        lf = lf_ref[rows, :]
        if valid_rows is not None:
            lf = jnp.where(rowmask, lf, 0.0)
        b_all = _cumsum_rows(lf)
        for h in range(H_A):
            sl = slice(h * DK, (h + 1) * DK)
            b = b_all[:, sl]
            bm = b[mid:mid + 1, :]
            bl = b[CHUNK - 1:CHUNK, :]
            kk = k_ref[rows, sl].astype(F32)
            if valid_rows is not None:
                kk = jnp.where(rowmask_h, kk, 0.0)
            q1, k1, q2, k2 = _head_tiles(q_ref[rows, sl].astype(F32), kk, b, bm, bl)
            vb = v_ref[rows, sl]
            st = st_scr[h]
            a = pl.dot(q1, k1, trans_b=True)
            a = jnp.where(causal, a, 0.0).astype(BF16)
            o = (jnp.dot(a, vb, preferred_element_type=F32)
                 + pl.dot(q2.astype(BF16), st.astype(BF16), trans_b=True))
            st_scr[h] = st * jnp.exp(bl) + pl.dot(vb, k2.astype(BF16), trans_a=True)
            out_ref[rows, sl] = _gated_rmsnorm(o, gn[:, sl], sg_ref[rows, sl].astype(F32))

    @pl.when(t == pl.num_programs(1) - 1)
    def _():
        for h in range(H_A):
            sfin_ref[0, h] = st_scr[h] if state_transposed_out else st_scr[h].T
        cfin_ref[0] = ubuf[CONV_TAIL:SUBLANES, :]


def _mix_seq(proj, row0, nseq, seq_len, tb, st0, tail0, gn, cw, *, mid, valid_rows=None,
             state_transposed_out=False, cast=()):
    nt = seq_len // tb
    cast_specs = []
    for w in cast:
        assert w.shape[0] % (BF16_ROWS * nseq * nt) == 0
        cast_specs.append(pl.BlockSpec((w.shape[0] // (nseq * nt), w.shape[1]),
                                       lambda b, t: (b * nt + t, 0)))
    assert seq_len % tb == 0 and row0 % tb == 0 and tb % CHUNK == 0
    rb0 = row0 // tb
    seg = lambda: pl.BlockSpec((tb, SEG), lambda b, t: (rb0 + b * nt + t, 0))
    const2 = lambda b, t: (0, 0)
    tril = np.tril(np.ones((CHUNK, CHUNK), np.float32))
    kern = functools.partial(_mix_seq_kernel, tb=tb, mid=mid, valid_rows=valid_rows,
                             state_transposed_out=state_transposed_out, n_cast=len(cast))
    res = pl.pallas_call(
        kern,
        grid=(nseq, nt),
        in_specs=[seg() for _ in range(7)] + [
                  pl.BlockSpec((H_A, DV, DK), lambda b, t: (0, 0, 0)),
                  pl.BlockSpec((CONV_W - 1, D_B), const2),
                  pl.BlockSpec((1, D_A), const2),
                  pl.BlockSpec((CONV_W, D_B), const2),
                  pl.BlockSpec((CHUNK, CHUNK), const2)] + cast_specs,
        out_specs=[pl.BlockSpec((tb, D_MODEL), lambda b, t: (b * nt + t, 0)),
                   pl.BlockSpec((1, H_A, DK, DV), lambda b, t: (b, 0, 0, 0)),
                   pl.BlockSpec((1, CONV_W - 1, D_B), lambda b, t: (b, 0, 0))] + cast_specs,
        out_shape=[jax.ShapeDtypeStruct((nseq * seq_len, D_MODEL), BF16),
                   jax.ShapeDtypeStruct((nseq, H_A, DK, DV), F32),
                   jax.ShapeDtypeStruct((nseq, CONV_W - 1, D_B), F32)]
        + [jax.ShapeDtypeStruct(w.shape, BF16) for w in cast],
        scratch_shapes=[pltpu.VMEM((H_A, DV, DK), F32), pltpu.VMEM((SUBLANES + tb, D_B), F32)],
        compiler_params=pltpu.CompilerParams(
            dimension_semantics=("arbitrary", "arbitrary"), vmem_limit_bytes=VMEM_LIMIT),
        name="mix_seq",
    )(*proj, st0, tail0, gn.reshape(1, -1), cw, jnp.asarray(tril), *cast)
    return res[:3], res[3:]


GROUP = 16
S_LEN = 4
S_MID = 2


def _mix_group_kernel(q_ref, lf_ref, k_ref, v_ref, sg_ref, gb_ref, u_ref,
                      s_ref, cbuf_ref, gn_ref, cw_ref, lmat_ref, amask_ref,
                      out_ref, sfin_ref, cfin_ref, full_scr, y_scr):
    cw = cw_ref[...]
    nb = CONV_W - 1
    for s in range(GROUP):
        r0 = SUBLANES * s
        full_scr[r0:r0 + nb, :] = cbuf_ref[s]
        full_scr[r0 + nb:r0 + nb + S_LEN, :] = u_ref[S_LEN * s:S_LEN * (s + 1), :]
    for s in range(GROUP):
        r0 = SUBLANES * s
        f0 = full_scr[r0:r0 + S_LEN, :]
        f1 = full_scr[r0 + 1:r0 + 1 + S_LEN, :]
        f2 = full_scr[r0 + 2:r0 + 2 + S_LEN, :]
        y_scr[S_LEN * s:S_LEN * (s + 1), :] = cw[0:1] * f0 + cw[1:2] * f1 + cw[2:3] * f2
        cfin_ref[s] = full_scr[r0 + S_LEN:r0 + S_LEN + nb, :]
    out_ref[:, D_A:] = (gb_ref[...].astype(F32) * y_scr[...]).astype(BF16)

    b3 = jnp.dot(lmat_ref[...], _split3(lf_ref[...]), preferred_element_type=F32)
    causal = amask_ref[...] > 0.0
    gn = gn_ref[...]
    sub = BF16_ROWS
    per_sub = sub // S_LEN
    rid = jax.lax.broadcasted_iota(jnp.int32, (sub, DK), 0)
    own = [(rid >= S_LEN * j) & (rid < S_LEN * (j + 1)) for j in range(per_sub)]
    ones_blk = jnp.ones((sub, DV), BF16)
    zeros_blk = jnp.zeros((sub, DV), BF16)
    for h in range(H_A):
        sl = slice(h * DK, (h + 1) * DK)
        b = b3[0:CHUNK, sl]
        bm = b3[CHUNK:2 * CHUNK, sl]
        bl = b3[2 * CHUNK:3 * CHUNK, sl]
        q1, k1, q2, k2 = _head_tiles(q_ref[:, sl].astype(F32), k_ref[:, sl].astype(F32),
                                     b, bm, bl)
        vb = v_ref[:, sl]
        a = pl.dot(q1, k1, trans_b=True)
        a = jnp.where(causal, a, 0.0).astype(BF16)
        o1 = jnp.dot(a, vb, preferred_element_type=F32)
        decay = jnp.exp(bl)
        for blk in range(CHUNK // sub):
            rs = slice(blk * sub, (blk + 1) * sub)
            q2b, k2b, vbb = q2[rs], k2[rs], vb[rs]
            rhs = jnp.concatenate([jnp.concatenate([vbb, zeros_blk], axis=1),
                                   jnp.concatenate([zeros_blk, ones_blk], axis=1)], axis=0)
            acc = o1[rs]
            for j in range(per_sub):
                s = blk * per_sub + j
                st = s_ref[s, h]
                qm = jnp.where(own[j], q2b, 0.0).astype(BF16)
                acc = acc + jnp.dot(qm, st.astype(BF16), preferred_element_type=F32)
                drows = _decay_rows(decay[S_LEN * s:S_LEN * s + 1, :], rid)
                lhs = jnp.concatenate([jnp.where(own[j], k2b, 0.0), drows], axis=0).astype(BF16)
                ud = pl.dot(lhs, rhs, trans_a=True)
                sfin_ref[s, h] = ud[:, DV:] * st + ud[:, :DV]
            out_ref[rs, sl] = _gated_rmsnorm(acc, gn[:, sl], sg_ref[rs, sl].astype(F32))


def _group_mats():
    r = np.arange(CHUNK)
    seq, pos = r // S_LEN, r % S_LEN
    same = seq[:, None] == seq[None, :]
    cum = same & (pos[None, :] <= pos[:, None])
    midm = same & (pos[None, :] <= S_MID)
    lmat = np.concatenate([cum, midm, same], axis=0).astype(np.float32)
    lmat = np.concatenate([lmat, lmat, lmat], axis=1)
    return jnp.asarray(lmat, BF16), jnp.asarray(cum.astype(np.float32))


def _mix_group(proj, nseq, s0, cbuf, gn, cw):
    steps = nseq // GROUP
    seg = lambda: pl.BlockSpec((CHUNK, SEG), lambda i: (i, 0))
    const2 = lambda i: (0, 0)
    lmat, amask = _group_mats()
    return pl.pallas_call(
        _mix_group_kernel,
        grid=(steps,),
        in_specs=[seg() for _ in range(7)] + [
                  pl.BlockSpec((GROUP, H_A, DK, DV), lambda i: (i, 0, 0, 0)),
                  pl.BlockSpec((GROUP, CONV_W - 1, D_B), lambda i: (i, 0, 0)),
                  pl.BlockSpec((1, D_A), const2),
                  pl.BlockSpec((CONV_W, D_B), const2),
                  pl.BlockSpec((3 * CHUNK, 3 * CHUNK), const2),
                  pl.BlockSpec((CHUNK, CHUNK), const2)],
        out_specs=[pl.BlockSpec((CHUNK, D_MODEL), lambda i: (i, 0)),
                   pl.BlockSpec((GROUP, H_A, DK, DV), lambda i: (i, 0, 0, 0)),
                   pl.BlockSpec((GROUP, CONV_W - 1, D_B), lambda i: (i, 0, 0))],
        out_shape=[jax.ShapeDtypeStruct((nseq * S_LEN, D_MODEL), BF16),
                   jax.ShapeDtypeStruct((nseq, H_A, DK, DV), F32),
                   jax.ShapeDtypeStruct((nseq, CONV_W - 1, D_B), F32)],
        scratch_shapes=[pltpu.VMEM((SUBLANES * GROUP, D_B), F32), pltpu.VMEM((CHUNK, D_B), F32)],
        compiler_params=pltpu.CompilerParams(
            dimension_semantics=("arbitrary",), vmem_limit_bytes=VMEM_LIMIT),
        name="mix_group",
    )(*proj, s0, cbuf, gn.reshape(1, -1), cw, lmat, amask)


OUTPROJ_SUB = 512


def _outproj_kernel(x_ref, mix_ref, wo_ref, g0_ref, b0_ref, g1_ref, b1_ref, h_ref):
    for r in range(x_ref.shape[0] // OUTPROJ_SUB):
        rs = slice(r * OUTPROJ_SUB, (r + 1) * OUTPROJ_SUB)
        xn = _layernorm(x_ref[rs, :], g0_ref[...], b0_ref[...])
        m = jnp.dot(mix_ref[rs, :], wo_ref[...], preferred_element_type=F32)
        h_ref[rs, :] = _layernorm(ALPHA * xn + m, g1_ref[...], b1_ref[...])


def _outproj(x, mix, w_o, g0, b0, g1, b1, tm):
    rows = x.shape[0]
    tm = min(tm, rows)
    assert rows % tm == 0 and tm % OUTPROJ_SUB == 0
    row = lambda a: a.reshape(1, -1)
    const = lambda i: (0, 0)
    vec = pl.BlockSpec((1, D_MODEL), const)
    return pl.pallas_call(
        _outproj_kernel,
        grid=(rows // tm,),
        in_specs=[pl.BlockSpec((tm, D_MODEL), lambda i: (i, 0)),
                  pl.BlockSpec((tm, D_MODEL), lambda i: (i, 0)),
                  pl.BlockSpec((D_MODEL, D_MODEL), const),
                  vec, vec, vec, vec],
        out_specs=pl.BlockSpec((tm, D_MODEL), lambda i: (i, 0)),
        out_shape=jax.ShapeDtypeStruct((rows, D_MODEL), F32),
        compiler_params=pltpu.CompilerParams(
            dimension_semantics=("arbitrary",), vmem_limit_bytes=VMEM_LIMIT),
        name="outproj",
    )(x, mix, w_o, row(g0), row(b0), row(g1), row(b1))


def _ffn_kernel(h_ref, wg_ref, wu_ref, wd_ref, g2_ref, b2_ref, o_ref, hb_ref):
    f = pl.program_id(1)

    @pl.when(f == 0)
    def _():
        h = h_ref[...]
        hb_ref[...] = h.astype(BF16)
        o_ref[...] = ALPHA * h

    hb = hb_ref[...]
    g = jnp.dot(hb, wg_ref[...], preferred_element_type=F32)
    u = jnp.dot(hb, wu_ref[...], preferred_element_type=F32)
    a = (_silu(g) * u).astype(BF16)
    o_ref[...] += jnp.dot(a, wd_ref[...], preferred_element_type=F32)

    @pl.when(f == pl.num_programs(1) - 1)
    def _():
        o_ref[...] = _layernorm(o_ref[...], g2_ref[...], b2_ref[...])


def _ffn(h, w_gate, w_up, w_down, g2, b2, tm, tf):
    rows = h.shape[0]
    assert rows % tm == 0 and D_FF % tf == 0
    row = lambda a: a.reshape(1, -1)
    vec = pl.BlockSpec((1, D_MODEL), lambda i, f: (0, 0))
    return pl.pallas_call(
        _ffn_kernel,
        grid=(rows // tm, D_FF // tf),
        in_specs=[pl.BlockSpec((tm, D_MODEL), lambda i, f: (i, 0)),
                  pl.BlockSpec((D_MODEL, tf), lambda i, f: (0, f)),
                  pl.BlockSpec((D_MODEL, tf), lambda i, f: (0, f)),
                  pl.BlockSpec((tf, D_MODEL), lambda i, f: (f, 0)),
                  vec, vec],
        out_specs=pl.BlockSpec((tm, D_MODEL), lambda i, f: (i, 0)),
        out_shape=jax.ShapeDtypeStruct((rows, D_MODEL), F32),
        scratch_shapes=[pltpu.VMEM((tm, D_MODEL), BF16)],
        compiler_params=pltpu.CompilerParams(
            dimension_semantics=("arbitrary", "arbitrary"), vmem_limit_bytes=VMEM_LIMIT),
        name="ffn",
    )(h, w_gate, w_up, w_down, row(g2), row(b2))


def kernel(x_prompt, x_sample, state_hgrn, state_conv, meta_tokens, ln0_g, ln0_b, w_in, b_f, lb_param, gnorm_g, conv_w, w_o, ln1_g, ln1_b, w_gate, w_up, w_down, ln2_g, ln2_b):
    bp, seq, _ = x_prompt.shape
    bs, dseq, _ = x_sample.shape
    assert dseq == S_LEN and seq % CHUNK == 0 and bs % GROUP == 0

    lb = jnp.cumsum(jax.nn.softmax(lb_param.astype(F32), axis=0), axis=0)[0]
    g0, b0 = ln0_g.astype(F32), ln0_b.astype(F32)

    xp = x_prompt.reshape(bp * seq, D_MODEL)
    xs = x_sample.reshape(bs * dseq, D_MODEL)
    xs_ext = jnp.concatenate(
        [xs, meta_tokens.astype(F32), jnp.zeros((CHUNK - N_META, D_MODEL), F32)], axis=0)

    proj_s, w_in_b = _inproj_cast(xs_ext, g0, b0, w_in[0], b_f[0], lb)
    proj_p = _inproj(xp, g0, b0, w_in_b, b_f[0], lb, tm=INPROJ_TM)

    zero_st = jnp.zeros((H_A, DV, DK), F32)
    zero_tail = jnp.zeros((CONV_W - 1, D_B), F32)
    (_, st_meta, tail_meta), _ = _mix_seq(
        proj_s, bs * dseq, 1, CHUNK, CHUNK, zero_st, zero_tail, gnorm_g[0], conv_w[0],
        mid=N_META // 2, valid_rows=N_META, state_transposed_out=True)

    (mix_p, hgrn_p, conv_p), (w_o_b, wg_b, wu_b, wd_b) = _mix_seq(
        proj_p, 0, bp, seq, MIX_TB, st_meta[0], tail_meta[0], gnorm_g[0], conv_w[0],
        mid=CHUNK // 2, cast=(w_o[0], w_gate[0], w_up[0], w_down[0]))
    mix_s, hgrn_s, conv_s = _mix_group(
        proj_s, bs, state_hgrn[0], state_conv[0], gnorm_g[0], conv_w[0])

    h_p = _outproj(xp, mix_p, w_o_b, g0, b0, ln1_g[0], ln1_b[0], tm=OUTPROJ_TM)
    h_s = _outproj(xs, mix_s, w_o_b, g0, b0, ln1_g[0], ln1_b[0], tm=OUTPROJ_TM)

    y_p = _ffn(h_p, wg_b, wu_b, wd_b, ln2_g[0], ln2_b[0], tm=FFN_TM, tf=FFN_TF)
    y_s = _ffn(h_s, wg_b, wu_b, wd_b, ln2_g[0], ln2_b[0], tm=FFN_TM_SMALL, tf=FFN_TF)

    return (y_p.reshape(bp, seq, D_MODEL), y_s.reshape(bs, dseq, D_MODEL),
            hgrn_p[None], conv_p[None], hgrn_s[None], conv_s[None])
```

```python
import functools

import numpy as np
import jax
import jax.numpy as jnp
from jax.experimental import pallas as pl
from jax.experimental.pallas import tpu as pltpu

F32 = jnp.float32
BF16 = jnp.bfloat16

D_MODEL = 2048
D_A = 1024
D_B = 1024
DK = 128
DV = 128
H_A = 8
SEG = 1024
N_SEG = 7
N_META = 16
CHUNK = 64
D_FF = 5632
CONV_W = 3
ALPHA = 2.0 ** 0.25
LN_EPS = 1e-5
RMS_EPS = 1e-6

V7X_VMEM_BYTES = 64 * 1024 * 1024
SUBLANES = 8
BF16_ROWS = 16
VMEM_LIMIT = V7X_VMEM_BYTES - 2 * 1024 * 1024

INPROJ_TM = 512
MIX_TB = 512
OUTPROJ_TM = 512
FFN_TM = 1024
FFN_TM_SMALL = 512
FFN_TF = 512

CONV_TAIL = SUBLANES - (CONV_W - 1)


def _layernorm(x, g, b):
    mu = jnp.mean(x, axis=-1, keepdims=True)
    xc = x - mu
    var = jnp.mean(xc * xc, axis=-1, keepdims=True)
    return xc * jax.lax.rsqrt(var + LN_EPS) * g + b


def _silu(x):
    return x * (1.0 / (1.0 + jnp.exp(-x)))


INPROJ_SUB = 256


def _inproj_kernel(x_ref, g0_ref, b0_ref, w_ref, bf_ref, lb_ref,
                   q_ref, lf_ref, k_ref, v_ref, sg_ref, gb_ref, u_ref):
    for r in range(x_ref.shape[0] // INPROJ_SUB):
        rs = slice(r * INPROJ_SUB, (r + 1) * INPROJ_SUB)
        xn = _layernorm(x_ref[rs, :], g0_ref[...], b0_ref[...]).astype(BF16)

        def seg(j, xn=xn):
            return jnp.dot(xn, w_ref[:, j * SEG:(j + 1) * SEG], preferred_element_type=F32)

        u_ref[rs, :] = seg(5) * seg(6)
        q_ref[rs, :] = _silu(seg(0)).astype(BF16)

        lf, kk = _forget_gate(seg(1) + bf_ref[...], lb_ref[...])
        lf_ref[rs, :] = lf
        k_ref[rs, :] = kk.astype(BF16)

        sg_ref[rs, :] = _silu(seg(3)).astype(BF16)
        gb_ref[rs, :] = seg(4).astype(BF16)
        v_ref[rs, :] = seg(2).astype(BF16)


def _inproj(x, g0, b0, w_in, b_f, lb, tm):
    rows = x.shape[0]
    assert rows % tm == 0 and tm % INPROJ_SUB == 0
    row = lambda a: a.reshape(1, -1)
    const = lambda i: (0, 0)
    out = lambda: pl.BlockSpec((tm, SEG), lambda i: (i, 0))
    sds = lambda dt: jax.ShapeDtypeStruct((rows, SEG), dt)
    return pl.pallas_call(
        _inproj_kernel,
        grid=(rows // tm,),
        in_specs=[
            pl.BlockSpec((tm, D_MODEL), lambda i: (i, 0)),
            pl.BlockSpec((1, D_MODEL), const),
            pl.BlockSpec((1, D_MODEL), const),
            pl.BlockSpec((D_MODEL, N_SEG * SEG), const, pipeline_mode=pl.Buffered(1)),
            pl.BlockSpec((1, SEG), const),
            pl.BlockSpec((1, SEG), const),
        ],
        out_specs=[out() for _ in range(7)],
        out_shape=[sds(BF16), sds(F32), sds(BF16), sds(BF16), sds(BF16), sds(BF16), sds(F32)],
        compiler_params=pltpu.CompilerParams(
            dimension_semantics=("arbitrary",), vmem_limit_bytes=VMEM_LIMIT),
        name="inproj",
    )(x, row(g0), row(b0), w_in, row(b_f), row(lb))


def _forget_gate(z, lb):
    e = jnp.exp(-jnp.abs(z))
    r = 1.0 / (1.0 + e)
    er = e * r
    pos = z >= 0.0
    return jnp.log(lb + (1.0 - lb) * jnp.where(pos, r, er)), (1.0 - lb) * jnp.where(pos, er, r)


def _inproj_cast_kernel(x_ref, g0_ref, b0_ref, w_ref, bf_ref, lb_ref,
                        q_ref, lf_ref, k_ref, v_ref, sg_ref, gb_ref, u_ref, wb_ref,
                        xn_scr, gc_scr):
    j = pl.program_id(0)

    @pl.when(j == 0)
    def _():
        xn_scr[...] = _layernorm(x_ref[...], g0_ref[...], b0_ref[...]).astype(BF16)

    wb = w_ref[...].astype(BF16)
    wb_ref[...] = wb
    acc = jnp.dot(xn_scr[...], wb, preferred_element_type=F32)

    @pl.when(j == 0)
    def _():
        q_ref[...] = _silu(acc).astype(BF16)

    @pl.when(j == 1)
    def _():
        lf, kk = _forget_gate(acc + bf_ref[...], lb_ref[...])
        lf_ref[...] = lf
        k_ref[...] = kk.astype(BF16)

    @pl.when(j == 2)
    def _():
        v_ref[...] = acc.astype(BF16)

    @pl.when(j == 3)
    def _():
        sg_ref[...] = _silu(acc).astype(BF16)

    @pl.when(j == 4)
    def _():
        gb_ref[...] = acc.astype(BF16)

    @pl.when(j == 5)
    def _():
        gc_scr[...] = acc

    @pl.when(j == 6)
    def _():
        u_ref[...] = gc_scr[...] * acc


def _inproj_cast(x, g0, b0, w_in, b_f, lb):
    rows = x.shape[0]
    row = lambda a: a.reshape(1, -1)
    const = lambda j: (0, 0)
    out = lambda: pl.BlockSpec((rows, SEG), const)
    sds = lambda dt: jax.ShapeDtypeStruct((rows, SEG), dt)
    wseg = lambda: pl.BlockSpec((D_MODEL, SEG), lambda j: (0, j))
    res = pl.pallas_call(
        _inproj_cast_kernel,
        grid=(N_SEG,),
        in_specs=[pl.BlockSpec((rows, D_MODEL), const),
                  pl.BlockSpec((1, D_MODEL), const),
                  pl.BlockSpec((1, D_MODEL), const),
                  wseg(),
                  pl.BlockSpec((1, SEG), const),
                  pl.BlockSpec((1, SEG), const)],
        out_specs=[out() for _ in range(7)] + [wseg()],
        out_shape=[sds(BF16), sds(F32), sds(BF16), sds(BF16), sds(BF16), sds(BF16), sds(F32),
                   jax.ShapeDtypeStruct(w_in.shape, BF16)],
        scratch_shapes=[pltpu.VMEM((rows, D_MODEL), BF16), pltpu.VMEM((rows, SEG), F32)],
        compiler_params=pltpu.CompilerParams(
            dimension_semantics=("arbitrary",), vmem_limit_bytes=VMEM_LIMIT),
        name="inproj_cast",
    )(x, row(g0), row(b0), w_in, row(b_f), row(lb))
    return res[:7], res[7]


def _split3(x):
    hi = x.astype(BF16)
    r1 = x - hi.astype(F32)
    mid = r1.astype(BF16)
    lo = (r1 - mid.astype(F32)).astype(BF16)
    return jnp.concatenate([hi, mid, lo], axis=0)


def _cumsum_rows(x):
    rows, lanes = x.shape
    n = rows // SUBLANES
    x3 = x.reshape(n, SUBLANES, lanes)
    pos = jax.lax.broadcasted_iota(jnp.int32, x3.shape, 1)
    s = 1
    while s < SUBLANES:
        x3 = x3 + jnp.where(pos >= s, pltpu.roll(x3, s, 1), 0.0)
        s *= 2
    carry = jnp.zeros((1, lanes), F32)
    out = []
    for g in range(n):
        out.append(x3[g] + carry)
        carry = carry + x3[g, SUBLANES - 1:SUBLANES, :]
    return jnp.concatenate(out, axis=0)


LOG2E = 1.4426950408889634


def _head_tiles(qt, kk, b, bm, bl):
    q1 = (qt * jnp.exp2(b - bm)).astype(BF16)
    k1 = (kk * jnp.exp2(bm - b)).astype(BF16)
    q2 = qt * jnp.exp2(b)
    k2 = kk * jnp.exp2(bl - b)
    return q1, k1, q2, k2


DECAY_ROWS = 16


def _decay_rows(d, rid):
    d_hi = d.astype(BF16).astype(F32)
    d_r = d - d_hi
    d_mid = d_r.astype(BF16).astype(F32)
    return jnp.where(rid == 0, d_hi, jnp.where(rid == 1, d_mid,
                     jnp.where(rid == 2, d_r - d_mid, 0.0)))


def _gated_rmsnorm(o, gn, sg):
    ms = jnp.mean(o * o, axis=-1, keepdims=True)
    return (o * jax.lax.rsqrt(ms + RMS_EPS) * gn * sg).astype(BF16)


def _mix_seq_kernel(q_ref, lf_ref, k_ref, v_ref, sg_ref, gb_ref, u_ref,
                    st0_ref, tail0_ref, gn_ref, cw_ref, amask_ref, *refs,
                    tb, mid, valid_rows, state_transposed_out, n_cast):
    cast_in, refs = refs[:n_cast], refs[n_cast:]
    out_ref, sfin_ref, cfin_ref = refs[:3]
    cast_out, (st_scr, ubuf) = refs[3:3 + n_cast], refs[3 + n_cast:]
    for src, dst in zip(cast_in, cast_out, strict=True):
        dst[...] = src[...].astype(BF16)
    t = pl.program_id(1)

    @pl.when(t == 0)
    def _():
        st_scr[...] = st0_ref[...]
        ubuf[CONV_TAIL:SUBLANES, :] = tail0_ref[...]

    u = u_ref[...]
    ubuf[SUBLANES:SUBLANES + tb, :] = u
    cw = cw_ref[...]
    conv = (cw[0:1] * ubuf[CONV_TAIL:CONV_TAIL + tb, :]
            + cw[1:2] * ubuf[CONV_TAIL + 1:CONV_TAIL + 1 + tb, :] + cw[2:3] * u)
    out_ref[:, D_A:] = (gb_ref[...].astype(F32) * conv).astype(BF16)
    last = valid_rows if valid_rows is not None else tb
    ubuf[CONV_TAIL:SUBLANES, :] = ubuf[CONV_TAIL + last:SUBLANES + last, :]

    causal = amask_ref[...] > 0.0
    gn = gn_ref[...]
    if valid_rows is not None:
        rowmask = jax.lax.broadcasted_iota(jnp.int32, (CHUNK, SEG), 0) < valid_rows
        rowmask_h = jax.lax.broadcasted_iota(jnp.int32, (CHUNK, DK), 0) < valid_rows
    for ci in range(tb // CHUNK):
        rows = slice(ci * CHUNK, (ci + 1) * CHUNK)
        lf = lf_ref[rows, :]
        if valid_rows is not None:
            lf = jnp.where(rowmask, lf, 0.0)
        b_all = _cumsum_rows(lf) * LOG2E
        for h in range(H_A):
            sl = slice(h * DK, (h + 1) * DK)
            b = b_all[:, sl]
            bm = b[mid:mid + 1, :]
            bl = b[CHUNK - 1:CHUNK, :]
            kk = k_ref[rows, sl].astype(F32)
            if valid_rows is not None:
                kk = jnp.where(rowmask_h, kk, 0.0)
            q1, k1, q2, k2 = _head_tiles(q_ref[rows, sl].astype(F32), kk, b, bm, bl)
            vb = v_ref[rows, sl]
            st = st_scr[h]
            a = pl.dot(q1, k1, trans_b=True)
            a = jnp.where(causal, a, 0.0).astype(BF16)
            o = (jnp.dot(a, vb, preferred_element_type=F32)
                 + jnp.dot(q2.astype(BF16), st.T.astype(BF16), preferred_element_type=F32))
            st_scr[h] = st * jnp.exp2(bl) + pl.dot(vb, k2.astype(BF16), trans_a=True)
            out_ref[rows, sl] = _gated_rmsnorm(o, gn[:, sl], sg_ref[rows, sl].astype(F32))

    @pl.when(t == pl.num_programs(1) - 1)
    def _():
        for h in range(H_A):
            sfin_ref[0, h] = st_scr[h] if state_transposed_out else st_scr[h].T
        cfin_ref[0] = ubuf[CONV_TAIL:SUBLANES, :]


def _mix_seq(proj, row0, nseq, seq_len, tb, st0, tail0, gn, cw, *, mid, valid_rows=None,
             state_transposed_out=False, cast=()):
    nt = seq_len // tb
    cast_specs = []
    for w in cast:
        assert w.shape[0] % (BF16_ROWS * nseq * nt) == 0
        cast_specs.append(pl.BlockSpec((w.shape[0] // (nseq * nt), w.shape[1]),
                                       lambda b, t: (b * nt + t, 0)))
    assert seq_len % tb == 0 and row0 % tb == 0 and tb % CHUNK == 0
    rb0 = row0 // tb
    seg = lambda: pl.BlockSpec((tb, SEG), lambda b, t: (rb0 + b * nt + t, 0))
    const2 = lambda b, t: (0, 0)
    tril = np.tril(np.ones((CHUNK, CHUNK), np.float32))
    kern = functools.partial(_mix_seq_kernel, tb=tb, mid=mid, valid_rows=valid_rows,
                             state_transposed_out=state_transposed_out, n_cast=len(cast))
    res = pl.pallas_call(
        kern,
        grid=(nseq, nt),
        in_specs=[seg() for _ in range(7)] + [
                  pl.BlockSpec((H_A, DV, DK), lambda b, t: (0, 0, 0)),
                  pl.BlockSpec((CONV_W - 1, D_B), const2),
                  pl.BlockSpec((1, D_A), const2),
                  pl.BlockSpec((CONV_W, D_B), const2),
                  pl.BlockSpec((CHUNK, CHUNK), const2)] + cast_specs,
        out_specs=[pl.BlockSpec((tb, D_MODEL), lambda b, t: (b * nt + t, 0)),
                   pl.BlockSpec((1, H_A, DK, DV), lambda b, t: (b, 0, 0, 0)),
                   pl.BlockSpec((1, CONV_W - 1, D_B), lambda b, t: (b, 0, 0))] + cast_specs,
        out_shape=[jax.ShapeDtypeStruct((nseq * seq_len, D_MODEL), BF16),
                   jax.ShapeDtypeStruct((nseq, H_A, DK, DV), F32),
                   jax.ShapeDtypeStruct((nseq, CONV_W - 1, D_B), F32)]
        + [jax.ShapeDtypeStruct(w.shape, BF16) for w in cast],
        scratch_shapes=[pltpu.VMEM((H_A, DV, DK), F32), pltpu.VMEM((SUBLANES + tb, D_B), F32)],
        compiler_params=pltpu.CompilerParams(
            dimension_semantics=("arbitrary", "arbitrary"), vmem_limit_bytes=VMEM_LIMIT),
        name="mix_seq",
    )(*proj, st0, tail0, gn.reshape(1, -1), cw, jnp.asarray(tril), *cast)
    return res[:3], res[3:]


GROUP = 16
S_LEN = 4
S_MID = 2


def _mix_group_kernel(q_ref, lf_ref, k_ref, v_ref, sg_ref, gb_ref, u_ref,
                      s_ref, cbuf_ref, gn_ref, cw_ref, lmat_ref, amask_ref,
                      out_ref, sfin_ref, cfin_ref, full_scr, y_scr):
    cw = cw_ref[...]
    nb = CONV_W - 1
    for s in range(GROUP):
        r0 = SUBLANES * s
        full_scr[r0:r0 + nb, :] = cbuf_ref[s]
        full_scr[r0 + nb:r0 + nb + S_LEN, :] = u_ref[S_LEN * s:S_LEN * (s + 1), :]
    for s in range(GROUP):
        r0 = SUBLANES * s
        f0 = full_scr[r0:r0 + S_LEN, :]
        f1 = full_scr[r0 + 1:r0 + 1 + S_LEN, :]
        f2 = full_scr[r0 + 2:r0 + 2 + S_LEN, :]
        y_scr[S_LEN * s:S_LEN * (s + 1), :] = cw[0:1] * f0 + cw[1:2] * f1 + cw[2:3] * f2
        cfin_ref[s] = full_scr[r0 + S_LEN:r0 + S_LEN + nb, :]
    out_ref[:, D_A:] = (gb_ref[...].astype(F32) * y_scr[...]).astype(BF16)

    b3 = jnp.dot(lmat_ref[...], _split3(lf_ref[...]), preferred_element_type=F32) * LOG2E
    causal = amask_ref[...] > 0.0
    gn = gn_ref[...]
    sub = BF16_ROWS
    per_sub = sub // S_LEN
    rid = jax.lax.broadcasted_iota(jnp.int32, (sub, DK), 0)
    own = [(rid >= S_LEN * j) & (rid < S_LEN * (j + 1)) for j in range(per_sub)]
    ones_blk = jnp.ones((sub, DV), BF16)
    zeros_blk = jnp.zeros((sub, DV), BF16)
    for h in range(H_A):
        sl = slice(h * DK, (h + 1) * DK)
        b = b3[0:CHUNK, sl]
        bm = b3[CHUNK:2 * CHUNK, sl]
        bl = b3[2 * CHUNK:3 * CHUNK, sl]
        q1, k1, q2, k2 = _head_tiles(q_ref[:, sl].astype(F32), k_ref[:, sl].astype(F32),
                                     b, bm, bl)
        vb = v_ref[:, sl]
        a = pl.dot(q1, k1, trans_b=True)
        a = jnp.where(causal, a, 0.0).astype(BF16)
        o1 = jnp.dot(a, vb, preferred_element_type=F32)
        decay = jnp.exp2(bl)
        for blk in range(CHUNK // sub):
            rs = slice(blk * sub, (blk + 1) * sub)
            q2b, k2b, vbb = q2[rs], k2[rs], vb[rs]
            rhs = jnp.concatenate([jnp.concatenate([vbb, zeros_blk], axis=1),
                                   jnp.concatenate([zeros_blk, ones_blk], axis=1)], axis=0)
            acc = o1[rs]
            for j in range(per_sub):
                s = blk * per_sub + j
                st = s_ref[s, h]
                qm = jnp.where(own[j], q2b, 0.0).astype(BF16)
                acc = acc + jnp.dot(qm, st.astype(BF16), preferred_element_type=F32)
                drows = _decay_rows(decay[S_LEN * s:S_LEN * s + 1, :], rid)
                lhs = jnp.concatenate([jnp.where(own[j], k2b, 0.0), drows], axis=0).astype(BF16)
                ud = pl.dot(lhs, rhs, trans_a=True)
                sfin_ref[s, h] = ud[:, DV:] * st + ud[:, :DV]
            out_ref[rs, sl] = _gated_rmsnorm(acc, gn[:, sl], sg_ref[rs, sl].astype(F32))


def _group_mats():
    r = np.arange(CHUNK)
    seq, pos = r // S_LEN, r % S_LEN
    same = seq[:, None] == seq[None, :]
    cum = same & (pos[None, :] <= pos[:, None])
    midm = same & (pos[None, :] <= S_MID)
    lmat = np.concatenate([cum, midm, same], axis=0).astype(np.float32)
    lmat = np.concatenate([lmat, lmat, lmat], axis=1)
    return jnp.asarray(lmat, BF16), jnp.asarray(cum.astype(np.float32))


def _mix_group(proj, nseq, s0, cbuf, gn, cw):
    steps = nseq // GROUP
    seg = lambda: pl.BlockSpec((CHUNK, SEG), lambda i: (i, 0))
    const2 = lambda i: (0, 0)
    lmat, amask = _group_mats()
    return pl.pallas_call(
        _mix_group_kernel,
        grid=(steps,),
        in_specs=[seg() for _ in range(7)] + [
                  pl.BlockSpec((GROUP, H_A, DK, DV), lambda i: (i, 0, 0, 0)),
                  pl.BlockSpec((GROUP, CONV_W - 1, D_B), lambda i: (i, 0, 0)),
                  pl.BlockSpec((1, D_A), const2),
                  pl.BlockSpec((CONV_W, D_B), const2),
                  pl.BlockSpec((3 * CHUNK, 3 * CHUNK), const2),
                  pl.BlockSpec((CHUNK, CHUNK), const2)],
        out_specs=[pl.BlockSpec((CHUNK, D_MODEL), lambda i: (i, 0)),
                   pl.BlockSpec((GROUP, H_A, DK, DV), lambda i: (i, 0, 0, 0)),
                   pl.BlockSpec((GROUP, CONV_W - 1, D_B), lambda i: (i, 0, 0))],
        out_shape=[jax.ShapeDtypeStruct((nseq * S_LEN, D_MODEL), BF16),
                   jax.ShapeDtypeStruct((nseq, H_A, DK, DV), F32),
                   jax.ShapeDtypeStruct((nseq, CONV_W - 1, D_B), F32)],
        scratch_shapes=[pltpu.VMEM((SUBLANES * GROUP, D_B), F32), pltpu.VMEM((CHUNK, D_B), F32)],
        compiler_params=pltpu.CompilerParams(
            dimension_semantics=("arbitrary",), vmem_limit_bytes=VMEM_LIMIT),
        name="mix_group",
    )(*proj, s0, cbuf, gn.reshape(1, -1), cw, lmat, amask)


OUTPROJ_SUB = 512


def _outproj_kernel(x_ref, mix_ref, wo_ref, g0_ref, b0_ref, g1_ref, b1_ref, h_ref):
    for r in range(x_ref.shape[0] // OUTPROJ_SUB):
        rs = slice(r * OUTPROJ_SUB, (r + 1) * OUTPROJ_SUB)
        xn = _layernorm(x_ref[rs, :], g0_ref[...], b0_ref[...])
        m = jnp.dot(mix_ref[rs, :], wo_ref[...], preferred_element_type=F32)
        h_ref[rs, :] = _layernorm(ALPHA * xn + m, g1_ref[...], b1_ref[...])


def _outproj(x, mix, w_o, g0, b0, g1, b1, tm):
    rows = x.shape[0]
    tm = min(tm, rows)
    assert rows % tm == 0 and tm % OUTPROJ_SUB == 0
    row = lambda a: a.reshape(1, -1)
    const = lambda i: (0, 0)
    vec = pl.BlockSpec((1, D_MODEL), const)
    return pl.pallas_call(
        _outproj_kernel,
        grid=(rows // tm,),
        in_specs=[pl.BlockSpec((tm, D_MODEL), lambda i: (i, 0)),
                  pl.BlockSpec((tm, D_MODEL), lambda i: (i, 0)),
                  pl.BlockSpec((D_MODEL, D_MODEL), const),
                  vec, vec, vec, vec],
        out_specs=pl.BlockSpec((tm, D_MODEL), lambda i: (i, 0)),
        out_shape=jax.ShapeDtypeStruct((rows, D_MODEL), F32),
        compiler_params=pltpu.CompilerParams(
            dimension_semantics=("arbitrary",), vmem_limit_bytes=VMEM_LIMIT),
        name="outproj",
    )(x, mix, w_o, row(g0), row(b0), row(g1), row(b1))


def _ffn_kernel(h_ref, wg_ref, wu_ref, wd_ref, g2_ref, b2_ref, o_ref, hb_ref):
    f = pl.program_id(1)

    @pl.when(f == 0)
    def _():
        h = h_ref[...]
        hb_ref[...] = h.astype(BF16)
        o_ref[...] = ALPHA * h

    hb = hb_ref[...]
    g = jnp.dot(hb, wg_ref[...], preferred_element_type=F32)
    u = jnp.dot(hb, wu_ref[...], preferred_element_type=F32)
    a = (_silu(g) * u).astype(BF16)
    o_ref[...] += jnp.dot(a, wd_ref[...], preferred_element_type=F32)

    @pl.when(f == pl.num_programs(1) - 1)
    def _():
        o_ref[...] = _layernorm(o_ref[...], g2_ref[...], b2_ref[...])


def _ffn(h, w_gate, w_up, w_down, g2, b2, tm, tf):
    rows = h.shape[0]
    assert rows % tm == 0 and D_FF % tf == 0
    row = lambda a: a.reshape(1, -1)
    vec = pl.BlockSpec((1, D_MODEL), lambda i, f: (0, 0))
    return pl.pallas_call(
        _ffn_kernel,
        grid=(rows // tm, D_FF // tf),
        in_specs=[pl.BlockSpec((tm, D_MODEL), lambda i, f: (i, 0)),
                  pl.BlockSpec((D_MODEL, tf), lambda i, f: (0, f)),
                  pl.BlockSpec((D_MODEL, tf), lambda i, f: (0, f)),
                  pl.BlockSpec((tf, D_MODEL), lambda i, f: (f, 0)),
                  vec, vec],
        out_specs=pl.BlockSpec((tm, D_MODEL), lambda i, f: (i, 0)),
        out_shape=jax.ShapeDtypeStruct((rows, D_MODEL), F32),
        scratch_shapes=[pltpu.VMEM((tm, D_MODEL), BF16)],
        compiler_params=pltpu.CompilerParams(
            dimension_semantics=("arbitrary", "arbitrary"), vmem_limit_bytes=VMEM_LIMIT),
        name="ffn",
    )(h, w_gate, w_up, w_down, row(g2), row(b2))


def kernel(x_prompt, x_sample, state_hgrn, state_conv, meta_tokens, ln0_g, ln0_b, w_in, b_f, lb_param, gnorm_g, conv_w, w_o, ln1_g, ln1_b, w_gate, w_up, w_down, ln2_g, ln2_b):
    bp, seq, _ = x_prompt.shape
    bs, dseq, _ = x_sample.shape
    assert dseq == S_LEN and seq % CHUNK == 0 and bs % GROUP == 0

    lb = jnp.cumsum(jax.nn.softmax(lb_param.astype(F32), axis=0), axis=0)[0]
    g0, b0 = ln0_g.astype(F32), ln0_b.astype(F32)

    xp = x_prompt.reshape(bp * seq, D_MODEL)
    xs = x_sample.reshape(bs * dseq, D_MODEL)
    xs_ext = jnp.concatenate(
        [xs, meta_tokens.astype(F32), jnp.zeros((CHUNK - N_META, D_MODEL), F32)], axis=0)

    proj_s, w_in_b = _inproj_cast(xs_ext, g0, b0, w_in[0], b_f[0], lb)
    proj_p = _inproj(xp, g0, b0, w_in_b, b_f[0], lb, tm=INPROJ_TM)

    zero_st = jnp.zeros((H_A, DV, DK), F32)
    zero_tail = jnp.zeros((CONV_W - 1, D_B), F32)
    (_, st_meta, tail_meta), _ = _mix_seq(
        proj_s, bs * dseq, 1, CHUNK, CHUNK, zero_st, zero_tail, gnorm_g[0], conv_w[0],
        mid=N_META // 2, valid_rows=N_META, state_transposed_out=True)

    (mix_p, hgrn_p, conv_p), (w_o_b, wg_b, wu_b, wd_b) = _mix_seq(
        proj_p, 0, bp, seq, MIX_TB, st_meta[0], tail_meta[0], gnorm_g[0], conv_w[0],
        mid=CHUNK // 2, cast=(w_o[0], w_gate[0], w_up[0], w_down[0]))
    mix_s, hgrn_s, conv_s = _mix_group(
        proj_s, bs, state_hgrn[0], state_conv[0], gnorm_g[0], conv_w[0])

    h_p = _outproj(xp, mix_p, w_o_b, g0, b0, ln1_g[0], ln1_b[0], tm=OUTPROJ_TM)
    h_s = _outproj(xs, mix_s, w_o_b, g0, b0, ln1_g[0], ln1_b[0], tm=OUTPROJ_TM)

    y_p = _ffn(h_p, wg_b, wu_b, wd_b, ln2_g[0], ln2_b[0], tm=FFN_TM, tf=FFN_TF)
    y_s = _ffn(h_s, wg_b, wu_b, wd_b, ln2_g[0], ln2_b[0], tm=FFN_TM_SMALL, tf=FFN_TF)

    return (y_p.reshape(bp, seq, D_MODEL), y_s.reshape(bs, dseq, D_MODEL),
            hgrn_p[None], conv_p[None], hgrn_s[None], conv_s[None])
```

```python
import functools

import numpy as np
import jax
import jax.numpy as jnp
from jax.experimental import pallas as pl
from jax.experimental.pallas import tpu as pltpu

F32 = jnp.float32
BF16 = jnp.bfloat16

D_MODEL = 2048
D_A = 1024
D_B = 1024
DK = 128
DV = 128
H_A = 8
SEG = 1024
N_SEG = 7
N_META = 16
CHUNK = 64
D_FF = 5632
CONV_W = 3
ALPHA = 2.0 ** 0.25
LN_EPS = 1e-5
RMS_EPS = 1e-6

V7X_VMEM_BYTES = 64 * 1024 * 1024
SUBLANES = 8
BF16_ROWS = 16
VMEM_LIMIT = V7X_VMEM_BYTES - 2 * 1024 * 1024

INPROJ_TM = 512
MIX_TB = 512
OUTPROJ_TM = 512
FFN_TM = 1024
FFN_TM_SMALL = 512
FFN_TF = 512

CONV_TAIL = SUBLANES - (CONV_W - 1)


def _layernorm(x, g, b):
    mu = jnp.mean(x, axis=-1, keepdims=True)
    xc = x - mu
    var = jnp.mean(xc * xc, axis=-1, keepdims=True)
    return xc * jax.lax.rsqrt(var + LN_EPS) * g + b


def _silu(x):
    return x * (1.0 / (1.0 + jnp.exp(-x)))


INPROJ_SUB = 256


def _inproj_kernel(x_ref, g0_ref, b0_ref, w_ref, bf_ref, lb_ref,
                   q_ref, lf_ref, k_ref, v_ref, sg_ref, gb_ref, u_ref):
    for r in range(x_ref.shape[0] // INPROJ_SUB):
        rs = slice(r * INPROJ_SUB, (r + 1) * INPROJ_SUB)
        xn = _layernorm(x_ref[rs, :], g0_ref[...], b0_ref[...]).astype(BF16)

        def seg(j, xn=xn):
            return jnp.dot(xn, w_ref[:, j * SEG:(j + 1) * SEG], preferred_element_type=F32)

        u_ref[rs, :] = seg(5) * seg(6)
        q_ref[rs, :] = _silu(seg(0)).astype(BF16)

        lf, kk = _forget_gate(seg(1) + bf_ref[...], lb_ref[...])
        lf_ref[rs, :] = lf
        k_ref[rs, :] = kk.astype(BF16)

        sg_ref[rs, :] = _silu(seg(3)).astype(BF16)
        gb_ref[rs, :] = seg(4).astype(BF16)
        v_ref[rs, :] = seg(2).astype(BF16)


def _inproj(x, g0, b0, w_in, b_f, lb, tm):
    rows = x.shape[0]
    assert rows % tm == 0 and tm % INPROJ_SUB == 0
    row = lambda a: a.reshape(1, -1)
    const = lambda i: (0, 0)
    out = lambda: pl.BlockSpec((tm, SEG), lambda i: (i, 0))
    sds = lambda dt: jax.ShapeDtypeStruct((rows, SEG), dt)
    return pl.pallas_call(
        _inproj_kernel,
        grid=(rows // tm,),
        in_specs=[
            pl.BlockSpec((tm, D_MODEL), lambda i: (i, 0)),
            pl.BlockSpec((1, D_MODEL), const),
            pl.BlockSpec((1, D_MODEL), const),
            pl.BlockSpec((D_MODEL, N_SEG * SEG), const, pipeline_mode=pl.Buffered(1)),
            pl.BlockSpec((1, SEG), const),
            pl.BlockSpec((1, SEG), const),
        ],
        out_specs=[out() for _ in range(7)],
        out_shape=[sds(BF16), sds(F32), sds(BF16), sds(BF16), sds(BF16), sds(BF16), sds(F32)],
        compiler_params=pltpu.CompilerParams(
            dimension_semantics=("arbitrary",), vmem_limit_bytes=VMEM_LIMIT),
        name="inproj",
    )(x, row(g0), row(b0), w_in, row(b_f), row(lb))


def _forget_gate(z, lb):
    e = jnp.exp(-jnp.abs(z))
    r = 1.0 / (1.0 + e)
    er = e * r
    pos = z >= 0.0
    return jnp.log(lb + (1.0 - lb) * jnp.where(pos, r, er)), (1.0 - lb) * jnp.where(pos, er, r)


def _inproj_cast_kernel(x_ref, g0_ref, b0_ref, w_ref, bf_ref, lb_ref,
                        q_ref, lf_ref, k_ref, v_ref, sg_ref, gb_ref, u_ref, wb_ref,
                        xn_scr, gc_scr):
    j = pl.program_id(0)

    @pl.when(j == 0)
    def _():
        xn_scr[...] = _layernorm(x_ref[...], g0_ref[...], b0_ref[...]).astype(BF16)

    wb = w_ref[...].astype(BF16)
    wb_ref[...] = wb
    acc = jnp.dot(xn_scr[...], wb, preferred_element_type=F32)

    @pl.when(j == 0)
    def _():
        q_ref[...] = _silu(acc).astype(BF16)

    @pl.when(j == 1)
    def _():
        lf, kk = _forget_gate(acc + bf_ref[...], lb_ref[...])
        lf_ref[...] = lf
        k_ref[...] = kk.astype(BF16)

    @pl.when(j == 2)
    def _():
        v_ref[...] = acc.astype(BF16)

    @pl.when(j == 3)
    def _():
        sg_ref[...] = _silu(acc).astype(BF16)

    @pl.when(j == 4)
    def _():
        gb_ref[...] = acc.astype(BF16)

    @pl.when(j == 5)
    def _():
        gc_scr[...] = acc

    @pl.when(j == 6)
    def _():
        u_ref[...] = gc_scr[...] * acc


def _inproj_cast(x, g0, b0, w_in, b_f, lb):
    rows = x.shape[0]
    row = lambda a: a.reshape(1, -1)
    const = lambda j: (0, 0)
    out = lambda: pl.BlockSpec((rows, SEG), const)
    sds = lambda dt: jax.ShapeDtypeStruct((rows, SEG), dt)
    wseg = lambda: pl.BlockSpec((D_MODEL, SEG), lambda j: (0, j))
    res = pl.pallas_call(
        _inproj_cast_kernel,
        grid=(N_SEG,),
        in_specs=[pl.BlockSpec((rows, D_MODEL), const),
                  pl.BlockSpec((1, D_MODEL), const),
                  pl.BlockSpec((1, D_MODEL), const),
                  wseg(),
                  pl.BlockSpec((1, SEG), const),
                  pl.BlockSpec((1, SEG), const)],
        out_specs=[out() for _ in range(7)] + [wseg()],
        out_shape=[sds(BF16), sds(F32), sds(BF16), sds(BF16), sds(BF16), sds(BF16), sds(F32),
                   jax.ShapeDtypeStruct(w_in.shape, BF16)],
        scratch_shapes=[pltpu.VMEM((rows, D_MODEL), BF16), pltpu.VMEM((rows, SEG), F32)],
        compiler_params=pltpu.CompilerParams(
            dimension_semantics=("arbitrary",), vmem_limit_bytes=VMEM_LIMIT),
        name="inproj_cast",
    )(x, row(g0), row(b0), w_in, row(b_f), row(lb))
    return res[:7], res[7]


def _split3(x):
    hi = x.astype(BF16)
    r1 = x - hi.astype(F32)
    mid = r1.astype(BF16)
    lo = (r1 - mid.astype(F32)).astype(BF16)
    return jnp.concatenate([hi, mid, lo], axis=0)


def _cumsum_rows(x):
    rows, lanes = x.shape
    n = rows // SUBLANES
    x3 = x.reshape(n, SUBLANES, lanes)
    pos = jax.lax.broadcasted_iota(jnp.int32, x3.shape, 1)
    s = 1
    while s < SUBLANES:
        x3 = x3 + jnp.where(pos >= s, pltpu.roll(x3, s, 1), 0.0)
        s *= 2
    carry = jnp.zeros((1, lanes), F32)
    out = []
    for g in range(n):
        out.append(x3[g] + carry)
        carry = carry + x3[g, SUBLANES - 1:SUBLANES, :]
    return jnp.concatenate(out, axis=0)


LOG2E = 1.4426950408889634


def _head_tiles(qt, kk, b, bm, bl):
    q1 = (qt * jnp.exp2(b - bm)).astype(BF16)
    k1 = (kk * jnp.exp2(bm - b)).astype(BF16)
    q2 = qt * jnp.exp2(b)
    k2 = kk * jnp.exp2(bl - b)
    return q1, k1, q2, k2


DECAY_ROWS = 16


def _decay_rows(d, rid):
    d_hi = d.astype(BF16).astype(F32)
    d_r = d - d_hi
    d_mid = d_r.astype(BF16).astype(F32)
    return jnp.where(rid == 0, d_hi, jnp.where(rid == 1, d_mid,
                     jnp.where(rid == 2, d_r - d_mid, 0.0)))


def _gated_rmsnorm(o, gn, sg):
    ms = jnp.mean(o * o, axis=-1, keepdims=True)
    return (o * jax.lax.rsqrt(ms + RMS_EPS) * gn * sg).astype(BF16)


def _mix_seq_kernel(q_ref, lf_ref, k_ref, v_ref, sg_ref, gb_ref, u_ref,
                    st0_ref, tail0_ref, gn_ref, cw_ref, amask_ref,
                    out_ref, sfin_ref, cfin_ref, st_scr, ubuf,
                    *, tb, mid, valid_rows, state_transposed_out):
    t = pl.program_id(1)

    @pl.when(t == 0)
    def _():
        st_scr[...] = st0_ref[...]
        ubuf[CONV_TAIL:SUBLANES, :] = tail0_ref[...]

    u = u_ref[...]
    ubuf[SUBLANES:SUBLANES + tb, :] = u
    cw = cw_ref[...]
    conv = (cw[0:1] * ubuf[CONV_TAIL:CONV_TAIL + tb, :]
            + cw[1:2] * ubuf[CONV_TAIL + 1:CONV_TAIL + 1 + tb, :] + cw[2:3] * u)
    out_ref[:, D_A:] = (gb_ref[...].astype(F32) * conv).astype(BF16)
    last = valid_rows if valid_rows is not None else tb
    ubuf[CONV_TAIL:SUBLANES, :] = ubuf[CONV_TAIL + last:SUBLANES + last, :]

    causal = amask_ref[...] > 0.0
    gn = gn_ref[...]
    if valid_rows is not None:
        rowmask = jax.lax.broadcasted_iota(jnp.int32, (CHUNK, SEG), 0) < valid_rows
        rowmask_h = jax.lax.broadcasted_iota(jnp.int32, (CHUNK, DK), 0) < valid_rows
    for ci in range(tb // CHUNK):
        rows = slice(ci * CHUNK, (ci + 1) * CHUNK)
        lf = lf_ref[rows, :]
        if valid_rows is not None:
            lf = jnp.where(rowmask, lf, 0.0)
        b_all = _cumsum_rows(lf) * LOG2E
        for h in range(H_A):
            sl = slice(h * DK, (h + 1) * DK)
            b = b_all[:, sl]
            bm = b[mid:mid + 1, :]
            bl = b[CHUNK - 1:CHUNK, :]
            kk = k_ref[rows, sl].astype(F32)
            if valid_rows is not None:
                kk = jnp.where(rowmask_h, kk, 0.0)
            q1, k1, q2, k2 = _head_tiles(q_ref[rows, sl].astype(F32), kk, b, bm, bl)
            vb = v_ref[rows, sl]
            st = st_scr[h]
            a = pl.dot(q1, k1, trans_b=True)
            a = jnp.where(causal, a, 0.0).astype(BF16)
            o = (jnp.dot(a, vb, preferred_element_type=F32)
                 + jnp.dot(q2.astype(BF16), st.T.astype(BF16), preferred_element_type=F32))
            st_scr[h] = st * jnp.exp2(bl) + pl.dot(vb, k2.astype(BF16), trans_a=True)
            out_ref[rows, sl] = _gated_rmsnorm(o, gn[:, sl], sg_ref[rows, sl].astype(F32))

    @pl.when(t == pl.num_programs(1) - 1)
    def _():
        for h in range(H_A):
            sfin_ref[0, h] = st_scr[h] if state_transposed_out else st_scr[h].T
        cfin_ref[0] = ubuf[CONV_TAIL:SUBLANES, :]


def _mix_seq(proj, row0, nseq, seq_len, tb, st0, tail0, gn, cw, *, mid, valid_rows=None,
             state_transposed_out=False):
    nt = seq_len // tb
    assert seq_len % tb == 0 and row0 % tb == 0 and tb % CHUNK == 0
    rb0 = row0 // tb
    seg = lambda: pl.BlockSpec((tb, SEG), lambda b, t: (rb0 + b * nt + t, 0))
    const2 = lambda b, t: (0, 0)
    tril = np.tril(np.ones((CHUNK, CHUNK), np.float32))
    kern = functools.partial(_mix_seq_kernel, tb=tb, mid=mid, valid_rows=valid_rows,
                             state_transposed_out=state_transposed_out)
    return pl.pallas_call(
        kern,
        grid=(nseq, nt),
        in_specs=[seg() for _ in range(7)] + [
                  pl.BlockSpec((H_A, DV, DK), lambda b, t: (0, 0, 0)),
                  pl.BlockSpec((CONV_W - 1, D_B), const2),
                  pl.BlockSpec((1, D_A), const2),
                  pl.BlockSpec((CONV_W, D_B), const2),
                  pl.BlockSpec((CHUNK, CHUNK), const2)],
        out_specs=[pl.BlockSpec((tb, D_MODEL), lambda b, t: (b * nt + t, 0)),
                   pl.BlockSpec((1, H_A, DK, DV), lambda b, t: (b, 0, 0, 0)),
                   pl.BlockSpec((1, CONV_W - 1, D_B), lambda b, t: (b, 0, 0))],
        out_shape=[jax.ShapeDtypeStruct((nseq * seq_len, D_MODEL), BF16),
                   jax.ShapeDtypeStruct((nseq, H_A, DK, DV), F32),
                   jax.ShapeDtypeStruct((nseq, CONV_W - 1, D_B), F32)],
        scratch_shapes=[pltpu.VMEM((H_A, DV, DK), F32), pltpu.VMEM((SUBLANES + tb, D_B), F32)],
        compiler_params=pltpu.CompilerParams(
            dimension_semantics=("arbitrary", "arbitrary"), vmem_limit_bytes=VMEM_LIMIT),
        name="mix_seq",
    )(*proj, st0, tail0, gn.reshape(1, -1), cw, jnp.asarray(tril))


GROUP = 16
S_LEN = 4
S_MID = 2


def _mix_group_kernel(q_ref, lf_ref, k_ref, v_ref, sg_ref, gb_ref, u_ref,
                      s_ref, cbuf_ref, gn_ref, cw_ref, lmat_ref, amask_ref,
                      out_ref, sfin_ref, cfin_ref, full_scr, y_scr):
    cw = cw_ref[...]
    nb = CONV_W - 1
    for s in range(GROUP):
        r0 = SUBLANES * s
        full_scr[r0:r0 + nb, :] = cbuf_ref[s]
        full_scr[r0 + nb:r0 + nb + S_LEN, :] = u_ref[S_LEN * s:S_LEN * (s + 1), :]
    for s in range(GROUP):
        r0 = SUBLANES * s
        f0 = full_scr[r0:r0 + S_LEN, :]
        f1 = full_scr[r0 + 1:r0 + 1 + S_LEN, :]
        f2 = full_scr[r0 + 2:r0 + 2 + S_LEN, :]
        y_scr[S_LEN * s:S_LEN * (s + 1), :] = cw[0:1] * f0 + cw[1:2] * f1 + cw[2:3] * f2
        cfin_ref[s] = full_scr[r0 + S_LEN:r0 + S_LEN + nb, :]
    out_ref[:, D_A:] = (gb_ref[...].astype(F32) * y_scr[...]).astype(BF16)

    b3 = jnp.dot(lmat_ref[...], _split3(lf_ref[...]), preferred_element_type=F32) * LOG2E
    causal = amask_ref[...] > 0.0
    gn = gn_ref[...]
    sub = BF16_ROWS
    per_sub = sub // S_LEN
    rid = jax.lax.broadcasted_iota(jnp.int32, (sub, DK), 0)
    own = [(rid >= S_LEN * j) & (rid < S_LEN * (j + 1)) for j in range(per_sub)]
    ones_blk = jnp.ones((sub, DV), BF16)
    zeros_blk = jnp.zeros((sub, DV), BF16)
    for h in range(H_A):
        sl = slice(h * DK, (h + 1) * DK)
        b = b3[0:CHUNK, sl]
        bm = b3[CHUNK:2 * CHUNK, sl]
        bl = b3[2 * CHUNK:3 * CHUNK, sl]
        q1, k1, q2, k2 = _head_tiles(q_ref[:, sl].astype(F32), k_ref[:, sl].astype(F32),
                                     b, bm, bl)
        vb = v_ref[:, sl]
        a = pl.dot(q1, k1, trans_b=True)
        a = jnp.where(causal, a, 0.0).astype(BF16)
        o1 = jnp.dot(a, vb, preferred_element_type=F32)
        decay = jnp.exp2(bl)
        for blk in range(CHUNK // sub):
            rs = slice(blk * sub, (blk + 1) * sub)
            q2b, k2b, vbb = q2[rs], k2[rs], vb[rs]
            rhs = jnp.concatenate([jnp.concatenate([vbb, zeros_blk], axis=1),
                                   jnp.concatenate([zeros_blk, ones_blk], axis=1)], axis=0)
            acc = o1[rs]
            for j in range(per_sub):
                s = blk * per_sub + j
                st = s_ref[s, h]
                qm = jnp.where(own[j], q2b, 0.0).astype(BF16)
                acc = acc + jnp.dot(qm, st.astype(BF16), preferred_element_type=F32)
                drows = _decay_rows(decay[S_LEN * s:S_LEN * s + 1, :], rid)
                lhs = jnp.concatenate([jnp.where(own[j], k2b, 0.0), drows], axis=0).astype(BF16)
                ud = pl.dot(lhs, rhs, trans_a=True)
                sfin_ref[s, h] = ud[:, DV:] * st + ud[:, :DV]
            out_ref[rs, sl] = _gated_rmsnorm(acc, gn[:, sl], sg_ref[rs, sl].astype(F32))


def _group_mats():
    r = np.arange(CHUNK)
    seq, pos = r // S_LEN, r % S_LEN
    same = seq[:, None] == seq[None, :]
    cum = same & (pos[None, :] <= pos[:, None])
    midm = same & (pos[None, :] <= S_MID)
    lmat = np.concatenate([cum, midm, same], axis=0).astype(np.float32)
    lmat = np.concatenate([lmat, lmat, lmat], axis=1)
    return jnp.asarray(lmat, BF16), jnp.asarray(cum.astype(np.float32))


def _mix_group(proj, nseq, s0, cbuf, gn, cw):
    steps = nseq // GROUP
    seg = lambda: pl.BlockSpec((CHUNK, SEG), lambda i: (i, 0))
    const2 = lambda i: (0, 0)
    lmat, amask = _group_mats()
    return pl.pallas_call(
        _mix_group_kernel,
        grid=(steps,),
        in_specs=[seg() for _ in range(7)] + [
                  pl.BlockSpec((GROUP, H_A, DK, DV), lambda i: (i, 0, 0, 0)),
                  pl.BlockSpec((GROUP, CONV_W - 1, D_B), lambda i: (i, 0, 0)),
                  pl.BlockSpec((1, D_A), const2),
                  pl.BlockSpec((CONV_W, D_B), const2),
                  pl.BlockSpec((3 * CHUNK, 3 * CHUNK), const2),
                  pl.BlockSpec((CHUNK, CHUNK), const2)],
        out_specs=[pl.BlockSpec((CHUNK, D_MODEL), lambda i: (i, 0)),
                   pl.BlockSpec((GROUP, H_A, DK, DV), lambda i: (i, 0, 0, 0)),
                   pl.BlockSpec((GROUP, CONV_W - 1, D_B), lambda i: (i, 0, 0))],
        out_shape=[jax.ShapeDtypeStruct((nseq * S_LEN, D_MODEL), BF16),
                   jax.ShapeDtypeStruct((nseq, H_A, DK, DV), F32),
                   jax.ShapeDtypeStruct((nseq, CONV_W - 1, D_B), F32)],
        scratch_shapes=[pltpu.VMEM((SUBLANES * GROUP, D_B), F32), pltpu.VMEM((CHUNK, D_B), F32)],
        compiler_params=pltpu.CompilerParams(
            dimension_semantics=("arbitrary",), vmem_limit_bytes=VMEM_LIMIT),
        name="mix_group",
    )(*proj, s0, cbuf, gn.reshape(1, -1), cw, lmat, amask)


def _outproj_kernel(x_ref, mix_ref, wo_ref, g0_ref, b0_ref, g1_ref, b1_ref, h_ref, *wob_ref):
    wo = wo_ref[...]
    if wob_ref:
        wo = wo.astype(BF16)
        wob_ref[0][...] = wo
    xn = _layernorm(x_ref[...], g0_ref[...], b0_ref[...])
    m = jnp.dot(mix_ref[...], wo, preferred_element_type=F32)
    h_ref[...] = _layernorm(ALPHA * xn + m, g1_ref[...], b1_ref[...])


def _outproj(x, mix, w_o, g0, b0, g1, b1, tm, emit_cast=False):
    rows = x.shape[0]
    tm = min(tm, rows)
    assert rows % tm == 0 and (not emit_cast or rows == tm)
    row = lambda a: a.reshape(1, -1)
    const = lambda i: (0, 0)
    vec = pl.BlockSpec((1, D_MODEL), const)
    tile = lambda: pl.BlockSpec((tm, D_MODEL), lambda i: (i, 0))
    w_spec = (pl.BlockSpec((D_MODEL, D_MODEL), const, pipeline_mode=pl.Buffered(1)) if emit_cast
              else pl.BlockSpec((D_MODEL, D_MODEL), const))
    out_specs, out_shape = [tile()], [jax.ShapeDtypeStruct((rows, D_MODEL), F32)]
    if emit_cast:
        out_specs.append(pl.BlockSpec((D_MODEL, D_MODEL), const))
        out_shape.append(jax.ShapeDtypeStruct(w_o.shape, BF16))
    res = pl.pallas_call(
        _outproj_kernel,
        grid=(rows // tm,),
        in_specs=[tile(), tile(), w_spec, vec, vec, vec, vec],
        out_specs=out_specs,
        out_shape=out_shape,
        compiler_params=pltpu.CompilerParams(
            dimension_semantics=("arbitrary",), vmem_limit_bytes=VMEM_LIMIT),
        name="outproj",
    )(x, mix, w_o, row(g0), row(b0), row(g1), row(b1))
    return (res[0], res[1]) if emit_cast else res[0]


def _ffn_kernel(h_ref, wg_ref, wu_ref, wd_ref, g2_ref, b2_ref, o_ref, *rest):
    *wb_refs, hb_ref = rest
    f = pl.program_id(1)

    @pl.when(f == 0)
    def _():
        h = h_ref[...]
        hb_ref[...] = h.astype(BF16)
        o_ref[...] = ALPHA * h

    wg, wu, wd = wg_ref[...], wu_ref[...], wd_ref[...]
    if wb_refs:
        wg, wu, wd = wg.astype(BF16), wu.astype(BF16), wd.astype(BF16)
        for ref, w in zip(wb_refs, (wg, wu, wd), strict=True):
            ref[...] = w
    hb = hb_ref[...]
    g = jnp.dot(hb, wg, preferred_element_type=F32)
    u = jnp.dot(hb, wu, preferred_element_type=F32)
    a = (_silu(g) * u).astype(BF16)
    o_ref[...] += jnp.dot(a, wd, preferred_element_type=F32)

    @pl.when(f == pl.num_programs(1) - 1)
    def _():
        o_ref[...] = _layernorm(o_ref[...], g2_ref[...], b2_ref[...])


def _ffn(h, w_gate, w_up, w_down, g2, b2, tm, tf, emit_cast=False):
    rows = h.shape[0]
    assert rows % tm == 0 and D_FF % tf == 0 and (not emit_cast or rows == tm)
    row = lambda a: a.reshape(1, -1)
    vec = pl.BlockSpec((1, D_MODEL), lambda i, f: (0, 0))
    w_specs = lambda: [pl.BlockSpec((D_MODEL, tf), lambda i, f: (0, f)),
                       pl.BlockSpec((D_MODEL, tf), lambda i, f: (0, f)),
                       pl.BlockSpec((tf, D_MODEL), lambda i, f: (f, 0))]
    out_specs = [pl.BlockSpec((tm, D_MODEL), lambda i, f: (i, 0))]
    out_shape = [jax.ShapeDtypeStruct((rows, D_MODEL), F32)]
    if emit_cast:
        out_specs += w_specs()
        out_shape += [jax.ShapeDtypeStruct(w.shape, BF16) for w in (w_gate, w_up, w_down)]
    res = pl.pallas_call(
        _ffn_kernel,
        grid=(rows // tm, D_FF // tf),
        in_specs=[pl.BlockSpec((tm, D_MODEL), lambda i, f: (i, 0))] + w_specs() + [vec, vec],
        out_specs=out_specs,
        out_shape=out_shape,
        scratch_shapes=[pltpu.VMEM((tm, D_MODEL), BF16)],
        compiler_params=pltpu.CompilerParams(
            dimension_semantics=("arbitrary", "arbitrary"), vmem_limit_bytes=VMEM_LIMIT),
        name="ffn",
    )(h, w_gate, w_up, w_down, row(g2), row(b2))
    return (res[0], tuple(res[1:])) if emit_cast else res[0]


def kernel(x_prompt, x_sample, state_hgrn, state_conv, meta_tokens, ln0_g, ln0_b, w_in, b_f, lb_param, gnorm_g, conv_w, w_o, ln1_g, ln1_b, w_gate, w_up, w_down, ln2_g, ln2_b):
    bp, seq, _ = x_prompt.shape
    bs, dseq, _ = x_sample.shape
    assert dseq == S_LEN and seq % CHUNK == 0 and bs % GROUP == 0

    lb = jnp.cumsum(jax.nn.softmax(lb_param.astype(F32), axis=0), axis=0)[0]
    g0, b0 = ln0_g.astype(F32), ln0_b.astype(F32)

    xp = x_prompt.reshape(bp * seq, D_MODEL)
    xs = x_sample.reshape(bs * dseq, D_MODEL)
    xs_ext = jnp.concatenate(
        [xs, meta_tokens.astype(F32), jnp.zeros((CHUNK - N_META, D_MODEL), F32)], axis=0)

    proj_s, w_in_b = _inproj_cast(xs_ext, g0, b0, w_in[0], b_f[0], lb)
    proj_p = _inproj(xp, g0, b0, w_in_b, b_f[0], lb, tm=INPROJ_TM)

    zero_st = jnp.zeros((H_A, DV, DK), F32)
    zero_tail = jnp.zeros((CONV_W - 1, D_B), F32)
    _, st_meta, tail_meta = _mix_seq(
        proj_s, bs * dseq, 1, CHUNK, CHUNK, zero_st, zero_tail, gnorm_g[0], conv_w[0],
        mid=N_META // 2, valid_rows=N_META, state_transposed_out=True)

    mix_p, hgrn_p, conv_p = _mix_seq(
        proj_p, 0, bp, seq, MIX_TB, st_meta[0], tail_meta[0], gnorm_g[0], conv_w[0],
        mid=CHUNK // 2)
    mix_s, hgrn_s, conv_s = _mix_group(
        proj_s, bs, state_hgrn[0], state_conv[0], gnorm_g[0], conv_w[0])

    h_s, w_o_b = _outproj(xs, mix_s, w_o[0], g0, b0, ln1_g[0], ln1_b[0], tm=OUTPROJ_TM,
                          emit_cast=True)
    y_s, (wg_b, wu_b, wd_b) = _ffn(h_s, w_gate[0], w_up[0], w_down[0], ln2_g[0], ln2_b[0],
                                   tm=FFN_TM_SMALL, tf=FFN_TF, emit_cast=True)
    h_p = _outproj(xp, mix_p, w_o_b, g0, b0, ln1_g[0], ln1_b[0], tm=OUTPROJ_TM)
    y_p = _ffn(h_p, wg_b, wu_b, wd_b, ln2_g[0], ln2_b[0], tm=FFN_TM, tf=FFN_TF)

    return (y_p.reshape(bp, seq, D_MODEL), y_s.reshape(bs, dseq, D_MODEL),
            hgrn_p[None], conv_p[None], hgrn_s[None], conv_s[None])
```

```python
import functools

import numpy as np
import jax
import jax.numpy as jnp
from jax.experimental import pallas as pl
from jax.experimental.pallas import tpu as pltpu

F32 = jnp.float32
BF16 = jnp.bfloat16

D_MODEL = 2048
D_A = 1024
D_B = 1024
DK = 128
DV = 128
H_A = 8
SEG = 1024
N_SEG = 7
N_META = 16
CHUNK = 64
D_FF = 5632
CONV_W = 3
ALPHA = 2.0 ** 0.25
LN_EPS = 1e-5
RMS_EPS = 1e-6

V7X_VMEM_BYTES = 64 * 1024 * 1024
SUBLANES = 8
BF16_ROWS = 16
VMEM_LIMIT = V7X_VMEM_BYTES - 2 * 1024 * 1024

INPROJ_TM = 256
MIX_TB = 512
OUTPROJ_TM = 512
FFN_TM = 1024
FFN_TM_SMALL = 512
FFN_TF = 512

CONV_TAIL = SUBLANES - (CONV_W - 1)


def _layernorm(x, g, b):
    mu = jnp.mean(x, axis=-1, keepdims=True)
    xc = x - mu
    var = jnp.mean(xc * xc, axis=-1, keepdims=True)
    return xc * jax.lax.rsqrt(var + LN_EPS) * g + b


def _silu(x):
    return x * (1.0 / (1.0 + jnp.exp(-x)))


INPROJ_SUB = 256


def _inproj_kernel(x_ref, g0_ref, b0_ref, w_ref, bf_ref, lb_ref, *refs, n_cast):
    cast_in, refs = refs[:n_cast], refs[n_cast:]
    (q_ref, lf_ref, k_ref, v_ref, sg_ref, gb_ref, u_ref), cast_out = refs[:7], refs[7:]
    for src, dst in zip(cast_in, cast_out, strict=True):
        dst[...] = src[...].astype(BF16)

    for r in range(x_ref.shape[0] // INPROJ_SUB):
        rs = slice(r * INPROJ_SUB, (r + 1) * INPROJ_SUB)
        xn = _layernorm(x_ref[rs, :], g0_ref[...], b0_ref[...]).astype(BF16)

        def seg(j, xn=xn):
            return jnp.dot(xn, w_ref[:, j * SEG:(j + 1) * SEG], preferred_element_type=F32)

        u_ref[rs, :] = seg(5) * seg(6)
        q_ref[rs, :] = _silu(seg(0)).astype(BF16)

        lf, kk = _forget_gate(seg(1) + bf_ref[...], lb_ref[...])
        lf_ref[rs, :] = lf
        k_ref[rs, :] = kk.astype(BF16)

        sg_ref[rs, :] = _silu(seg(3)).astype(BF16)
        gb_ref[rs, :] = seg(4).astype(BF16)
        v_ref[rs, :] = seg(2).astype(BF16)


def _inproj(x, g0, b0, w_in, b_f, lb, tm, cast=()):
    rows = x.shape[0]
    assert rows % tm == 0 and tm % INPROJ_SUB == 0
    steps = rows // tm
    row = lambda a: a.reshape(1, -1)
    const = lambda i: (0, 0)
    out = lambda: pl.BlockSpec((tm, SEG), lambda i: (i, 0))
    sds = lambda dt: jax.ShapeDtypeStruct((rows, SEG), dt)
    cast_specs = []
    for w in cast:
        assert w.shape[0] % (BF16_ROWS * steps) == 0
        cast_specs.append(pl.BlockSpec((w.shape[0] // steps, w.shape[1]), lambda i: (i, 0)))
    res = pl.pallas_call(
        functools.partial(_inproj_kernel, n_cast=len(cast)),
        grid=(steps,),
        in_specs=[
            pl.BlockSpec((tm, D_MODEL), lambda i: (i, 0)),
            pl.BlockSpec((1, D_MODEL), const),
            pl.BlockSpec((1, D_MODEL), const),
            pl.BlockSpec((D_MODEL, N_SEG * SEG), const, pipeline_mode=pl.Buffered(1)),
            pl.BlockSpec((1, SEG), const),
            pl.BlockSpec((1, SEG), const),
        ] + cast_specs,
        out_specs=[out() for _ in range(7)] + cast_specs,
        out_shape=[sds(BF16), sds(F32), sds(BF16), sds(BF16), sds(BF16), sds(BF16), sds(F32)]
        + [jax.ShapeDtypeStruct(w.shape, BF16) for w in cast],
        compiler_params=pltpu.CompilerParams(
            dimension_semantics=("arbitrary",), vmem_limit_bytes=VMEM_LIMIT),
        name="inproj",
    )(x, row(g0), row(b0), w_in, row(b_f), row(lb), *cast)
    return res[:7], res[7:]


def _forget_gate(z, lb):
    e = jnp.exp(-jnp.abs(z))
    r = 1.0 / (1.0 + e)
    er = e * r
    pos = z >= 0.0
    return jnp.log(lb + (1.0 - lb) * jnp.where(pos, r, er)), (1.0 - lb) * jnp.where(pos, er, r)


def _inproj_cast_kernel(x_ref, g0_ref, b0_ref, w_ref, bf_ref, lb_ref,
                        q_ref, lf_ref, k_ref, v_ref, sg_ref, gb_ref, u_ref, wb_ref,
                        xn_scr, gc_scr):
    j = pl.program_id(0)

    @pl.when(j == 0)
    def _():
        xn_scr[...] = _layernorm(x_ref[...], g0_ref[...], b0_ref[...]).astype(BF16)

    wb = w_ref[...].astype(BF16)
    wb_ref[...] = wb
    acc = jnp.dot(xn_scr[...], wb, preferred_element_type=F32)

    @pl.when(j == 0)
    def _():
        q_ref[...] = _silu(acc).astype(BF16)

    @pl.when(j == 1)
    def _():
        lf, kk = _forget_gate(acc + bf_ref[...], lb_ref[...])
        lf_ref[...] = lf
        k_ref[...] = kk.astype(BF16)

    @pl.when(j == 2)
    def _():
        v_ref[...] = acc.astype(BF16)

    @pl.when(j == 3)
    def _():
        sg_ref[...] = _silu(acc).astype(BF16)

    @pl.when(j == 4)
    def _():
        gb_ref[...] = acc.astype(BF16)

    @pl.when(j == 5)
    def _():
        gc_scr[...] = acc

    @pl.when(j == 6)
    def _():
        u_ref[...] = gc_scr[...] * acc


def _inproj_cast(x, g0, b0, w_in, b_f, lb):
    rows = x.shape[0]
    row = lambda a: a.reshape(1, -1)
    const = lambda j: (0, 0)
    out = lambda: pl.BlockSpec((rows, SEG), const)
    sds = lambda dt: jax.ShapeDtypeStruct((rows, SEG), dt)
    wseg = lambda: pl.BlockSpec((D_MODEL, SEG), lambda j: (0, j))
    res = pl.pallas_call(
        _inproj_cast_kernel,
        grid=(N_SEG,),
        in_specs=[pl.BlockSpec((rows, D_MODEL), const),
                  pl.BlockSpec((1, D_MODEL), const),
                  pl.BlockSpec((1, D_MODEL), const),
                  wseg(),
                  pl.BlockSpec((1, SEG), const),
                  pl.BlockSpec((1, SEG), const)],
        out_specs=[out() for _ in range(7)] + [wseg()],
        out_shape=[sds(BF16), sds(F32), sds(BF16), sds(BF16), sds(BF16), sds(BF16), sds(F32),
                   jax.ShapeDtypeStruct(w_in.shape, BF16)],
        scratch_shapes=[pltpu.VMEM((rows, D_MODEL), BF16), pltpu.VMEM((rows, SEG), F32)],
        compiler_params=pltpu.CompilerParams(
            dimension_semantics=("arbitrary",), vmem_limit_bytes=VMEM_LIMIT),
        name="inproj_cast",
    )(x, row(g0), row(b0), w_in, row(b_f), row(lb))
    return res[:7], res[7]


def _split3(x):
    hi = x.astype(BF16)
    r1 = x - hi.astype(F32)
    mid = r1.astype(BF16)
    lo = (r1 - mid.astype(F32)).astype(BF16)
    return jnp.concatenate([hi, mid, lo], axis=0)


def _cumsum_rows(x):
    rows, lanes = x.shape
    n = rows // SUBLANES
    x3 = x.reshape(n, SUBLANES, lanes)
    pos = jax.lax.broadcasted_iota(jnp.int32, x3.shape, 1)
    s = 1
    while s < SUBLANES:
        x3 = x3 + jnp.where(pos >= s, pltpu.roll(x3, s, 1), 0.0)
        s *= 2
    carry = jnp.zeros((1, lanes), F32)
    out = []
    for g in range(n):
        out.append(x3[g] + carry)
        carry = carry + x3[g, SUBLANES - 1:SUBLANES, :]
    return jnp.concatenate(out, axis=0)


LOG2E = 1.4426950408889634


def _head_tiles(qt, kk, b, bm, bl):
    q1 = (qt * jnp.exp2(b - bm)).astype(BF16)
    k1 = (kk * jnp.exp2(bm - b)).astype(BF16)
    q2 = qt * jnp.exp2(b)
    k2 = kk * jnp.exp2(bl - b)
    return q1, k1, q2, k2


DECAY_ROWS = 16


def _decay_rows(d, rid):
    d_hi = d.astype(BF16).astype(F32)
    d_r = d - d_hi
    d_mid = d_r.astype(BF16).astype(F32)
    return jnp.where(rid == 0, d_hi, jnp.where(rid == 1, d_mid,
                     jnp.where(rid == 2, d_r - d_mid, 0.0)))


def _gated_rmsnorm(o, gn, sg):
    ms = jnp.mean(o * o, axis=-1, keepdims=True)
    return (o * jax.lax.rsqrt(ms + RMS_EPS) * gn * sg).astype(BF16)


def _mix_seq_kernel(q_ref, lf_ref, k_ref, v_ref, sg_ref, gb_ref, u_ref,
                    st0_ref, tail0_ref, gn_ref, cw_ref, amask_ref,
                    out_ref, sfin_ref, cfin_ref, st_scr, ubuf,
                    *, tb, mid, valid_rows, state_transposed_out):
    t = pl.program_id(1)

    @pl.when(t == 0)
    def _():
        st_scr[...] = st0_ref[...]
        ubuf[CONV_TAIL:SUBLANES, :] = tail0_ref[...]

    u = u_ref[...]
    ubuf[SUBLANES:SUBLANES + tb, :] = u
    cw = cw_ref[...]
    conv = (cw[0:1] * ubuf[CONV_TAIL:CONV_TAIL + tb, :]
            + cw[1:2] * ubuf[CONV_TAIL + 1:CONV_TAIL + 1 + tb, :] + cw[2:3] * u)
    out_ref[:, D_A:] = (gb_ref[...].astype(F32) * conv).astype(BF16)
    last = valid_rows if valid_rows is not None else tb
    ubuf[CONV_TAIL:SUBLANES, :] = ubuf[CONV_TAIL + last:SUBLANES + last, :]

    causal = amask_ref[...] > 0.0
    gn = gn_ref[...]
    if valid_rows is not None:
        rowmask = jax.lax.broadcasted_iota(jnp.int32, (CHUNK, SEG), 0) < valid_rows
        rowmask_h = jax.lax.broadcasted_iota(jnp.int32, (CHUNK, DK), 0) < valid_rows
    for ci in range(tb // CHUNK):
        rows = slice(ci * CHUNK, (ci + 1) * CHUNK)
        lf = lf_ref[rows, :]
        if valid_rows is not None:
            lf = jnp.where(rowmask, lf, 0.0)
        b_all = _cumsum_rows(lf) * LOG2E
        for h in range(H_A):
            sl = slice(h * DK, (h + 1) * DK)
            b = b_all[:, sl]
            bm = b[mid:mid + 1, :]
            bl = b[CHUNK - 1:CHUNK, :]
            kk = k_ref[rows, sl].astype(F32)
            if valid_rows is not None:
                kk = jnp.where(rowmask_h, kk, 0.0)
            q1, k1, q2, k2 = _head_tiles(q_ref[rows, sl].astype(F32), kk, b, bm, bl)
            vb = v_ref[rows, sl]
            st = st_scr[h]
            a = pl.dot(q1, k1, trans_b=True)
            a = jnp.where(causal, a, 0.0).astype(BF16)
            o = (jnp.dot(a, vb, preferred_element_type=F32)
                 + jnp.dot(q2.astype(BF16), st.T.astype(BF16), preferred_element_type=F32))
            st_scr[h] = st * jnp.exp2(bl) + pl.dot(vb, k2.astype(BF16), trans_a=True)
            out_ref[rows, sl] = _gated_rmsnorm(o, gn[:, sl], sg_ref[rows, sl].astype(F32))

    @pl.when(t == pl.num_programs(1) - 1)
    def _():
        for h in range(H_A):
            sfin_ref[0, h] = st_scr[h] if state_transposed_out else st_scr[h].T
        cfin_ref[0] = ubuf[CONV_TAIL:SUBLANES, :]


def _mix_seq(proj, row0, nseq, seq_len, tb, st0, tail0, gn, cw, *, mid, valid_rows=None,
             state_transposed_out=False):
    nt = seq_len // tb
    assert seq_len % tb == 0 and row0 % tb == 0 and tb % CHUNK == 0
    rb0 = row0 // tb
    seg = lambda: pl.BlockSpec((tb, SEG), lambda b, t: (rb0 + b * nt + t, 0))
    const2 = lambda b, t: (0, 0)
    tril = np.tril(np.ones((CHUNK, CHUNK), np.float32))
    kern = functools.partial(_mix_seq_kernel, tb=tb, mid=mid, valid_rows=valid_rows,
                             state_transposed_out=state_transposed_out)
    return pl.pallas_call(
        kern,
        grid=(nseq, nt),
        in_specs=[seg() for _ in range(7)] + [
                  pl.BlockSpec((H_A, DV, DK), lambda b, t: (0, 0, 0)),
                  pl.BlockSpec((CONV_W - 1, D_B), const2),
                  pl.BlockSpec((1, D_A), const2),
                  pl.BlockSpec((CONV_W, D_B), const2),
                  pl.BlockSpec((CHUNK, CHUNK), const2)],
        out_specs=[pl.BlockSpec((tb, D_MODEL), lambda b, t: (b * nt + t, 0)),
                   pl.BlockSpec((1, H_A, DK, DV), lambda b, t: (b, 0, 0, 0)),
                   pl.BlockSpec((1, CONV_W - 1, D_B), lambda b, t: (b, 0, 0))],
        out_shape=[jax.ShapeDtypeStruct((nseq * seq_len, D_MODEL), BF16),
                   jax.ShapeDtypeStruct((nseq, H_A, DK, DV), F32),
                   jax.ShapeDtypeStruct((nseq, CONV_W - 1, D_B), F32)],
        scratch_shapes=[pltpu.VMEM((H_A, DV, DK), F32), pltpu.VMEM((SUBLANES + tb, D_B), F32)],
        compiler_params=pltpu.CompilerParams(
            dimension_semantics=("arbitrary", "arbitrary"), vmem_limit_bytes=VMEM_LIMIT),
        name="mix_seq",
    )(*proj, st0, tail0, gn.reshape(1, -1), cw, jnp.asarray(tril))


GROUP = 16
S_LEN = 4
S_MID = 2


def _mix_group_kernel(q_ref, lf_ref, k_ref, v_ref, sg_ref, gb_ref, u_ref,
                      s_ref, cbuf_ref, gn_ref, cw_ref, lmat_ref, amask_ref,
                      out_ref, sfin_ref, cfin_ref, full_scr, y_scr):
    cw = cw_ref[...]
    nb = CONV_W - 1
    for s in range(GROUP):
        r0 = SUBLANES * s
        full_scr[r0:r0 + nb, :] = cbuf_ref[s]
        full_scr[r0 + nb:r0 + nb + S_LEN, :] = u_ref[S_LEN * s:S_LEN * (s + 1), :]
    for s in range(GROUP):
        r0 = SUBLANES * s
        f0 = full_scr[r0:r0 + S_LEN, :]
        f1 = full_scr[r0 + 1:r0 + 1 + S_LEN, :]
        f2 = full_scr[r0 + 2:r0 + 2 + S_LEN, :]
        y_scr[S_LEN * s:S_LEN * (s + 1), :] = cw[0:1] * f0 + cw[1:2] * f1 + cw[2:3] * f2
        cfin_ref[s] = full_scr[r0 + S_LEN:r0 + S_LEN + nb, :]
    out_ref[:, D_A:] = (gb_ref[...].astype(F32) * y_scr[...]).astype(BF16)

    b3 = jnp.dot(lmat_ref[...], _split3(lf_ref[...]), preferred_element_type=F32) * LOG2E
    causal = amask_ref[...] > 0.0
    gn = gn_ref[...]
    sub = BF16_ROWS
    per_sub = sub // S_LEN
    rid = jax.lax.broadcasted_iota(jnp.int32, (sub, DK), 0)
    own = [(rid >= S_LEN * j) & (rid < S_LEN * (j + 1)) for j in range(per_sub)]
    ones_blk = jnp.ones((sub, DV), BF16)
    zeros_blk = jnp.zeros((sub, DV), BF16)
    for h in range(H_A):
        sl = slice(h * DK, (h + 1) * DK)
        b = b3[0:CHUNK, sl]
        bm = b3[CHUNK:2 * CHUNK, sl]
        bl = b3[2 * CHUNK:3 * CHUNK, sl]
        q1, k1, q2, k2 = _head_tiles(q_ref[:, sl].astype(F32), k_ref[:, sl].astype(F32),
                                     b, bm, bl)
        vb = v_ref[:, sl]
        a = pl.dot(q1, k1, trans_b=True)
        a = jnp.where(causal, a, 0.0).astype(BF16)
        o1 = jnp.dot(a, vb, preferred_element_type=F32)
        decay = jnp.exp2(bl)
        for blk in range(CHUNK // sub):
            rs = slice(blk * sub, (blk + 1) * sub)
            q2b, k2b, vbb = q2[rs], k2[rs], vb[rs]
            rhs = jnp.concatenate([jnp.concatenate([vbb, zeros_blk], axis=1),
                                   jnp.concatenate([zeros_blk, ones_blk], axis=1)], axis=0)
            acc = o1[rs]
            for j in range(per_sub):
                s = blk * per_sub + j
                st = s_ref[s, h]
                qm = jnp.where(own[j], q2b, 0.0).astype(BF16)
                acc = acc + jnp.dot(qm, st.astype(BF16), preferred_element_type=F32)
                drows = _decay_rows(decay[S_LEN * s:S_LEN * s + 1, :], rid)
                lhs = jnp.concatenate([jnp.where(own[j], k2b, 0.0), drows], axis=0).astype(BF16)
                ud = pl.dot(lhs, rhs, trans_a=True)
                sfin_ref[s, h] = ud[:, DV:] * st + ud[:, :DV]
            out_ref[rs, sl] = _gated_rmsnorm(acc, gn[:, sl], sg_ref[rs, sl].astype(F32))


def _group_mats():
    r = np.arange(CHUNK)
    seq, pos = r // S_LEN, r % S_LEN
    same = seq[:, None] == seq[None, :]
    cum = same & (pos[None, :] <= pos[:, None])
    midm = same & (pos[None, :] <= S_MID)
    lmat = np.concatenate([cum, midm, same], axis=0).astype(np.float32)
    lmat = np.concatenate([lmat, lmat, lmat], axis=1)
    return jnp.asarray(lmat, BF16), jnp.asarray(cum.astype(np.float32))


def _mix_group(proj, nseq, s0, cbuf, gn, cw):
    steps = nseq // GROUP
    seg = lambda: pl.BlockSpec((CHUNK, SEG), lambda i: (i, 0))
    const2 = lambda i: (0, 0)
    lmat, amask = _group_mats()
    return pl.pallas_call(
        _mix_group_kernel,
        grid=(steps,),
        in_specs=[seg() for _ in range(7)] + [
                  pl.BlockSpec((GROUP, H_A, DK, DV), lambda i: (i, 0, 0, 0)),
                  pl.BlockSpec((GROUP, CONV_W - 1, D_B), lambda i: (i, 0, 0)),
                  pl.BlockSpec((1, D_A), const2),
                  pl.BlockSpec((CONV_W, D_B), const2),
                  pl.BlockSpec((3 * CHUNK, 3 * CHUNK), const2),
                  pl.BlockSpec((CHUNK, CHUNK), const2)],
        out_specs=[pl.BlockSpec((CHUNK, D_MODEL), lambda i: (i, 0)),
                   pl.BlockSpec((GROUP, H_A, DK, DV), lambda i: (i, 0, 0, 0)),
                   pl.BlockSpec((GROUP, CONV_W - 1, D_B), lambda i: (i, 0, 0))],
        out_shape=[jax.ShapeDtypeStruct((nseq * S_LEN, D_MODEL), BF16),
                   jax.ShapeDtypeStruct((nseq, H_A, DK, DV), F32),
                   jax.ShapeDtypeStruct((nseq, CONV_W - 1, D_B), F32)],
        scratch_shapes=[pltpu.VMEM((SUBLANES * GROUP, D_B), F32), pltpu.VMEM((CHUNK, D_B), F32)],
        compiler_params=pltpu.CompilerParams(
            dimension_semantics=("arbitrary",), vmem_limit_bytes=VMEM_LIMIT),
        name="mix_group",
    )(*proj, s0, cbuf, gn.reshape(1, -1), cw, lmat, amask)


def _outproj_kernel(x_ref, mix_ref, wo_ref, g0_ref, b0_ref, g1_ref, b1_ref, h_ref):
    xn = _layernorm(x_ref[...], g0_ref[...], b0_ref[...])
    m = jnp.dot(mix_ref[...], wo_ref[...], preferred_element_type=F32)
    h_ref[...] = _layernorm(ALPHA * xn + m, g1_ref[...], b1_ref[...])


def _outproj(x, mix, w_o, g0, b0, g1, b1, tm):
    rows = x.shape[0]
    tm = min(tm, rows)
    assert rows % tm == 0
    row = lambda a: a.reshape(1, -1)
    const = lambda i: (0, 0)
    vec = pl.BlockSpec((1, D_MODEL), const)
    return pl.pallas_call(
        _outproj_kernel,
        grid=(rows // tm,),
        in_specs=[pl.BlockSpec((tm, D_MODEL), lambda i: (i, 0)),
                  pl.BlockSpec((tm, D_MODEL), lambda i: (i, 0)),
                  pl.BlockSpec((D_MODEL, D_MODEL), const),
                  vec, vec, vec, vec],
        out_specs=pl.BlockSpec((tm, D_MODEL), lambda i: (i, 0)),
        out_shape=jax.ShapeDtypeStruct((rows, D_MODEL), F32),
        compiler_params=pltpu.CompilerParams(
            dimension_semantics=("arbitrary",), vmem_limit_bytes=VMEM_LIMIT),
        name="outproj",
    )(x, mix, w_o, row(g0), row(b0), row(g1), row(b1))


def _ffn_kernel(h_ref, wg_ref, wu_ref, wd_ref, g2_ref, b2_ref, o_ref, hb_ref):
    f = pl.program_id(1)

    @pl.when(f == 0)
    def _():
        h = h_ref[...]
        hb_ref[...] = h.astype(BF16)
        o_ref[...] = ALPHA * h

    hb = hb_ref[...]
    g = jnp.dot(hb, wg_ref[...], preferred_element_type=F32)
    u = jnp.dot(hb, wu_ref[...], preferred_element_type=F32)
    a = (_silu(g) * u).astype(BF16)
    o_ref[...] += jnp.dot(a, wd_ref[...], preferred_element_type=F32)

    @pl.when(f == pl.num_programs(1) - 1)
    def _():
        o_ref[...] = _layernorm(o_ref[...], g2_ref[...], b2_ref[...])


def _ffn(h, w_gate, w_up, w_down, g2, b2, tm, tf):
    rows = h.shape[0]
    assert rows % tm == 0 and D_FF % tf == 0
    row = lambda a: a.reshape(1, -1)
    vec = pl.BlockSpec((1, D_MODEL), lambda i, f: (0, 0))
    return pl.pallas_call(
        _ffn_kernel,
        grid=(rows // tm, D_FF // tf),
        in_specs=[pl.BlockSpec((tm, D_MODEL), lambda i, f: (i, 0)),
                  pl.BlockSpec((D_MODEL, tf), lambda i, f: (0, f)),
                  pl.BlockSpec((D_MODEL, tf), lambda i, f: (0, f)),
                  pl.BlockSpec((tf, D_MODEL), lambda i, f: (f, 0)),
                  vec, vec],
        out_specs=pl.BlockSpec((tm, D_MODEL), lambda i, f: (i, 0)),
        out_shape=jax.ShapeDtypeStruct((rows, D_MODEL), F32),
        scratch_shapes=[pltpu.VMEM((tm, D_MODEL), BF16)],
        compiler_params=pltpu.CompilerParams(
            dimension_semantics=("arbitrary", "arbitrary"), vmem_limit_bytes=VMEM_LIMIT),
        name="ffn",
    )(h, w_gate, w_up, w_down, row(g2), row(b2))


def kernel(x_prompt, x_sample, state_hgrn, state_conv, meta_tokens, ln0_g, ln0_b, w_in, b_f, lb_param, gnorm_g, conv_w, w_o, ln1_g, ln1_b, w_gate, w_up, w_down, ln2_g, ln2_b):
    bp, seq, _ = x_prompt.shape
    bs, dseq, _ = x_sample.shape
    assert dseq == S_LEN and seq % CHUNK == 0 and bs % GROUP == 0

    lb = jnp.cumsum(jax.nn.softmax(lb_param.astype(F32), axis=0), axis=0)[0]
    g0, b0 = ln0_g.astype(F32), ln0_b.astype(F32)

    xp = x_prompt.reshape(bp * seq, D_MODEL)
    xs = x_sample.reshape(bs * dseq, D_MODEL)
    xs_ext = jnp.concatenate(
        [xs, meta_tokens.astype(F32), jnp.zeros((CHUNK - N_META, D_MODEL), F32)], axis=0)

    proj_s, w_in_b = _inproj_cast(xs_ext, g0, b0, w_in[0], b_f[0], lb)
    proj_p, (w_o_b, wg_b, wu_b, wd_b) = _inproj(
        xp, g0, b0, w_in_b, b_f[0], lb, tm=INPROJ_TM,
        cast=(w_o[0], w_gate[0], w_up[0], w_down[0]))

    zero_st = jnp.zeros((H_A, DV, DK), F32)
    zero_tail = jnp.zeros((CONV_W - 1, D_B), F32)
    _, st_meta, tail_meta = _mix_seq(
        proj_s, bs * dseq, 1, CHUNK, CHUNK, zero_st, zero_tail, gnorm_g[0], conv_w[0],
        mid=N_META // 2, valid_rows=N_META, state_transposed_out=True)

    mix_p, hgrn_p, conv_p = _mix_seq(
        proj_p, 0, bp, seq, MIX_TB, st_meta[0], tail_meta[0], gnorm_g[0], conv_w[0],
        mid=CHUNK // 2)
    mix_s, hgrn_s, conv_s = _mix_group(
        proj_s, bs, state_hgrn[0], state_conv[0], gnorm_g[0], conv_w[0])

    h_p = _outproj(xp, mix_p, w_o_b, g0, b0, ln1_g[0], ln1_b[0], tm=OUTPROJ_TM)
    h_s = _outproj(xs, mix_s, w_o_b, g0, b0, ln1_g[0], ln1_b[0], tm=OUTPROJ_TM)

    y_p = _ffn(h_p, wg_b, wu_b, wd_b, ln2_g[0], ln2_b[0], tm=FFN_TM, tf=FFN_TF)
    y_s = _ffn(h_s, wg_b, wu_b, wd_b, ln2_g[0], ln2_b[0], tm=FFN_TM_SMALL, tf=FFN_TF)

    return (y_p.reshape(bp, seq, D_MODEL), y_s.reshape(bs, dseq, D_MODEL),
            hgrn_p[None], conv_p[None], hgrn_s[None], conv_s[None])
```

```python
import functools

import numpy as np
import jax
import jax.numpy as jnp
from jax.experimental import pallas as pl
from jax.experimental.pallas import tpu as pltpu

F32 = jnp.float32
BF16 = jnp.bfloat16

D_MODEL = 2048
D_A = 1024
D_B = 1024
DK = 128
DV = 128
H_A = 8
SEG = 1024
N_SEG = 7
N_META = 16
CHUNK = 64
D_FF = 5632
CONV_W = 3
ALPHA = 2.0 ** 0.25
LN_EPS = 1e-5
RMS_EPS = 1e-6

V7X_VMEM_BYTES = 64 * 1024 * 1024
SUBLANES = 8
BF16_ROWS = 16
VMEM_LIMIT = V7X_VMEM_BYTES - 2 * 1024 * 1024

INPROJ_TM = 256
MIX_TB = 512
OUTPROJ_TM = 512
FFN_TM = 1024
FFN_TM_SMALL = 512
FFN_TF = 512

CONV_TAIL = SUBLANES - (CONV_W - 1)


def _layernorm(x, g, b):
    mu = jnp.mean(x, axis=-1, keepdims=True)
    xc = x - mu
    var = jnp.mean(xc * xc, axis=-1, keepdims=True)
    return xc * jax.lax.rsqrt(var + LN_EPS) * g + b


def _silu(x):
    return x * (1.0 / (1.0 + jnp.exp(-x)))


INPROJ_SUB = 256


def _inproj_kernel(x_ref, g0_ref, b0_ref, w_ref, bf_ref, lb_ref, *refs, n_cast):
    cast_in, refs = refs[:n_cast], refs[n_cast:]
    (q_ref, lf_ref, k_ref, v_ref, sg_ref, gb_ref, u_ref), cast_out = refs[:7], refs[7:]
    for src, dst in zip(cast_in, cast_out, strict=True):
        if len(dst.shape) == 2:
            dst[...] = src[...].astype(BF16)
        else:
            ct = dst.shape[2]
            for c in range(dst.shape[0]):
                dst[c] = src[:, c * ct:(c + 1) * ct].astype(BF16)

    for r in range(x_ref.shape[0] // INPROJ_SUB):
        rs = slice(r * INPROJ_SUB, (r + 1) * INPROJ_SUB)
        xn = _layernorm(x_ref[rs, :], g0_ref[...], b0_ref[...]).astype(BF16)

        def seg(j, xn=xn):
            return jnp.dot(xn, w_ref[:, j * SEG:(j + 1) * SEG], preferred_element_type=F32)

        u_ref[rs, :] = seg(5) * seg(6)
        q_ref[rs, :] = _silu(seg(0)).astype(BF16)

        lf, kk = _forget_gate(seg(1) + bf_ref[...], lb_ref[...])
        lf_ref[rs, :] = lf
        k_ref[rs, :] = kk.astype(BF16)

        sg_ref[rs, :] = _silu(seg(3)).astype(BF16)
        gb_ref[rs, :] = seg(4).astype(BF16)
        v_ref[rs, :] = seg(2).astype(BF16)


def _inproj(x, g0, b0, w_in, b_f, lb, tm, cast=()):
    rows = x.shape[0]
    assert rows % tm == 0 and tm % INPROJ_SUB == 0
    steps = rows // tm
    row = lambda a: a.reshape(1, -1)
    const = lambda i: (0, 0)
    out = lambda: pl.BlockSpec((tm, SEG), lambda i: (i, 0))
    sds = lambda dt: jax.ShapeDtypeStruct((rows, SEG), dt)
    cast_in_specs, cast_out_specs, cast_shapes = [], [], []
    for w, ct in cast:
        assert w.shape[0] % (BF16_ROWS * steps) == 0
        rpt = w.shape[0] // steps
        cast_in_specs.append(pl.BlockSpec((rpt, w.shape[1]), lambda i: (i, 0)))
        if ct is None:
            cast_out_specs.append(pl.BlockSpec((rpt, w.shape[1]), lambda i: (i, 0)))
            cast_shapes.append(jax.ShapeDtypeStruct(w.shape, BF16))
        else:
            assert w.shape[1] % ct == 0
            nct = w.shape[1] // ct
            cast_out_specs.append(pl.BlockSpec((nct, rpt, ct), lambda i: (0, i, 0)))
            cast_shapes.append(jax.ShapeDtypeStruct((nct, w.shape[0], ct), BF16))
    res = pl.pallas_call(
        functools.partial(_inproj_kernel, n_cast=len(cast)),
        grid=(steps,),
        in_specs=[
            pl.BlockSpec((tm, D_MODEL), lambda i: (i, 0)),
            pl.BlockSpec((1, D_MODEL), const),
            pl.BlockSpec((1, D_MODEL), const),
            pl.BlockSpec((D_MODEL, N_SEG * SEG), const, pipeline_mode=pl.Buffered(1)),
            pl.BlockSpec((1, SEG), const),
            pl.BlockSpec((1, SEG), const),
        ] + cast_in_specs,
        out_specs=[out() for _ in range(7)] + cast_out_specs,
        out_shape=[sds(BF16), sds(F32), sds(BF16), sds(BF16), sds(BF16), sds(BF16), sds(F32)]
        + cast_shapes,
        compiler_params=pltpu.CompilerParams(
            dimension_semantics=("arbitrary",), vmem_limit_bytes=VMEM_LIMIT),
        name="inproj",
    )(x, row(g0), row(b0), w_in, row(b_f), row(lb), *[w for w, _ in cast])
    return res[:7], res[7:]


def _forget_gate(z, lb):
    e = jnp.exp(-jnp.abs(z))
    r = 1.0 / (1.0 + e)
    er = e * r
    pos = z >= 0.0
    return jnp.log(lb + (1.0 - lb) * jnp.where(pos, r, er)), (1.0 - lb) * jnp.where(pos, er, r)


def _inproj_cast_kernel(x_ref, g0_ref, b0_ref, w_ref, bf_ref, lb_ref,
                        q_ref, lf_ref, k_ref, v_ref, sg_ref, gb_ref, u_ref, wb_ref,
                        xn_scr, gc_scr):
    j = pl.program_id(0)

    @pl.when(j == 0)
    def _():
        xn_scr[...] = _layernorm(x_ref[...], g0_ref[...], b0_ref[...]).astype(BF16)

    wb = w_ref[...].astype(BF16)
    wb_ref[...] = wb
    acc = jnp.dot(xn_scr[...], wb, preferred_element_type=F32)

    @pl.when(j == 0)
    def _():
        q_ref[...] = _silu(acc).astype(BF16)

    @pl.when(j == 1)
    def _():
        lf, kk = _forget_gate(acc + bf_ref[...], lb_ref[...])
        lf_ref[...] = lf
        k_ref[...] = kk.astype(BF16)

    @pl.when(j == 2)
    def _():
        v_ref[...] = acc.astype(BF16)

    @pl.when(j == 3)
    def _():
        sg_ref[...] = _silu(acc).astype(BF16)

    @pl.when(j == 4)
    def _():
        gb_ref[...] = acc.astype(BF16)

    @pl.when(j == 5)
    def _():
        gc_scr[...] = acc

    @pl.when(j == 6)
    def _():
        u_ref[...] = gc_scr[...] * acc


def _inproj_cast(x, g0, b0, w_in, b_f, lb):
    rows = x.shape[0]
    row = lambda a: a.reshape(1, -1)
    const = lambda j: (0, 0)
    out = lambda: pl.BlockSpec((rows, SEG), const)
    sds = lambda dt: jax.ShapeDtypeStruct((rows, SEG), dt)
    wseg = lambda: pl.BlockSpec((D_MODEL, SEG), lambda j: (0, j))
    res = pl.pallas_call(
        _inproj_cast_kernel,
        grid=(N_SEG,),
        in_specs=[pl.BlockSpec((rows, D_MODEL), const),
                  pl.BlockSpec((1, D_MODEL), const),
                  pl.BlockSpec((1, D_MODEL), const),
                  wseg(),
                  pl.BlockSpec((1, SEG), const),
                  pl.BlockSpec((1, SEG), const)],
        out_specs=[out() for _ in range(7)] + [wseg()],
        out_shape=[sds(BF16), sds(F32), sds(BF16), sds(BF16), sds(BF16), sds(BF16), sds(F32),
                   jax.ShapeDtypeStruct(w_in.shape, BF16)],
        scratch_shapes=[pltpu.VMEM((rows, D_MODEL), BF16), pltpu.VMEM((rows, SEG), F32)],
        compiler_params=pltpu.CompilerParams(
            dimension_semantics=("arbitrary",), vmem_limit_bytes=VMEM_LIMIT),
        name="inproj_cast",
    )(x, row(g0), row(b0), w_in, row(b_f), row(lb))
    return res[:7], res[7]


def _split3(x):
    hi = x.astype(BF16)
    r1 = x - hi.astype(F32)
    mid = r1.astype(BF16)
    lo = (r1 - mid.astype(F32)).astype(BF16)
    return jnp.concatenate([hi, mid, lo], axis=0)


def _cumsum_rows(x):
    rows, lanes = x.shape
    n = rows // SUBLANES
    x3 = x.reshape(n, SUBLANES, lanes)
    pos = jax.lax.broadcasted_iota(jnp.int32, x3.shape, 1)
    s = 1
    while s < SUBLANES:
        x3 = x3 + jnp.where(pos >= s, pltpu.roll(x3, s, 1), 0.0)
        s *= 2
    carry = jnp.zeros((1, lanes), F32)
    out = []
    for g in range(n):
        out.append(x3[g] + carry)
        carry = carry + x3[g, SUBLANES - 1:SUBLANES, :]
    return jnp.concatenate(out, axis=0)


LOG2E = 1.4426950408889634


def _head_tiles(qt, kk, b, bm, bl):
    q1 = (qt * jnp.exp2(b - bm)).astype(BF16)
    k1 = (kk * jnp.exp2(bm - b)).astype(BF16)
    q2 = qt * jnp.exp2(b)
    k2 = kk * jnp.exp2(bl - b)
    return q1, k1, q2, k2


DECAY_ROWS = 16


def _decay_rows(d, rid):
    d_hi = d.astype(BF16).astype(F32)
    d_r = d - d_hi
    d_mid = d_r.astype(BF16).astype(F32)
    return jnp.where(rid == 0, d_hi, jnp.where(rid == 1, d_mid,
                     jnp.where(rid == 2, d_r - d_mid, 0.0)))


def _gated_rmsnorm(o, gn, sg):
    ms = jnp.mean(o * o, axis=-1, keepdims=True)
    return (o * jax.lax.rsqrt(ms + RMS_EPS) * gn * sg).astype(BF16)


def _mix_seq_kernel(q_ref, lf_ref, k_ref, v_ref, sg_ref, gb_ref, u_ref,
                    st0_ref, tail0_ref, gn_ref, cw_ref, amask_ref,
                    out_ref, sfin_ref, cfin_ref, st_scr, ubuf,
                    *, tb, mid, valid_rows, state_transposed_out):
    t = pl.program_id(1)

    @pl.when(t == 0)
    def _():
        st_scr[...] = st0_ref[...]
        ubuf[CONV_TAIL:SUBLANES, :] = tail0_ref[...]

    u = u_ref[...]
    ubuf[SUBLANES:SUBLANES + tb, :] = u
    cw = cw_ref[...]
    conv = (cw[0:1] * ubuf[CONV_TAIL:CONV_TAIL + tb, :]
            + cw[1:2] * ubuf[CONV_TAIL + 1:CONV_TAIL + 1 + tb, :] + cw[2:3] * u)
    out_ref[:, D_A:] = (gb_ref[...].astype(F32) * conv).astype(BF16)
    last = valid_rows if valid_rows is not None else tb
    ubuf[CONV_TAIL:SUBLANES, :] = ubuf[CONV_TAIL + last:SUBLANES + last, :]

    causal = amask_ref[...] > 0.0
    gn = gn_ref[...]
    if valid_rows is not None:
        rowmask = jax.lax.broadcasted_iota(jnp.int32, (CHUNK, SEG), 0) < valid_rows
        rowmask_h = jax.lax.broadcasted_iota(jnp.int32, (CHUNK, DK), 0) < valid_rows
    for ci in range(tb // CHUNK):
        rows = slice(ci * CHUNK, (ci + 1) * CHUNK)
        lf = lf_ref[rows, :]
        if valid_rows is not None:
            lf = jnp.where(rowmask, lf, 0.0)
        b_all = _cumsum_rows(lf) * LOG2E
        for h in range(H_A):
            sl = slice(h * DK, (h + 1) * DK)
            b = b_all[:, sl]
            bm = b[mid:mid + 1, :]
            bl = b[CHUNK - 1:CHUNK, :]
            kk = k_ref[rows, sl].astype(F32)
            if valid_rows is not None:
                kk = jnp.where(rowmask_h, kk, 0.0)
            q1, k1, q2, k2 = _head_tiles(q_ref[rows, sl].astype(F32), kk, b, bm, bl)
            vb = v_ref[rows, sl]
            st = st_scr[h]
            a = pl.dot(q1, k1, trans_b=True)
            a = jnp.where(causal, a, 0.0).astype(BF16)
            o = (jnp.dot(a, vb, preferred_element_type=F32)
                 + jnp.dot(q2.astype(BF16), st.T.astype(BF16), preferred_element_type=F32))
            st_scr[h] = st * jnp.exp2(bl) + pl.dot(vb, k2.astype(BF16), trans_a=True)
            out_ref[rows, sl] = _gated_rmsnorm(o, gn[:, sl], sg_ref[rows, sl].astype(F32))

    @pl.when(t == pl.num_programs(1) - 1)
    def _():
        for h in range(H_A):
            sfin_ref[0, h] = st_scr[h] if state_transposed_out else st_scr[h].T
        cfin_ref[0] = ubuf[CONV_TAIL:SUBLANES, :]


def _mix_seq(proj, row0, nseq, seq_len, tb, st0, tail0, gn, cw, *, mid, valid_rows=None,
             state_transposed_out=False):
    nt = seq_len // tb
    assert seq_len % tb == 0 and row0 % tb == 0 and tb % CHUNK == 0
    rb0 = row0 // tb
    seg = lambda: pl.BlockSpec((tb, SEG), lambda b, t: (rb0 + b * nt + t, 0))
    const2 = lambda b, t: (0, 0)
    tril = np.tril(np.ones((CHUNK, CHUNK), np.float32))
    kern = functools.partial(_mix_seq_kernel, tb=tb, mid=mid, valid_rows=valid_rows,
                             state_transposed_out=state_transposed_out)
    return pl.pallas_call(
        kern,
        grid=(nseq, nt),
        in_specs=[seg() for _ in range(7)] + [
                  pl.BlockSpec((H_A, DV, DK), lambda b, t: (0, 0, 0)),
                  pl.BlockSpec((CONV_W - 1, D_B), const2),
                  pl.BlockSpec((1, D_A), const2),
                  pl.BlockSpec((CONV_W, D_B), const2),
                  pl.BlockSpec((CHUNK, CHUNK), const2)],
        out_specs=[pl.BlockSpec((tb, D_MODEL), lambda b, t: (b * nt + t, 0)),
                   pl.BlockSpec((1, H_A, DK, DV), lambda b, t: (b, 0, 0, 0)),
                   pl.BlockSpec((1, CONV_W - 1, D_B), lambda b, t: (b, 0, 0))],
        out_shape=[jax.ShapeDtypeStruct((nseq * seq_len, D_MODEL), BF16),
                   jax.ShapeDtypeStruct((nseq, H_A, DK, DV), F32),
                   jax.ShapeDtypeStruct((nseq, CONV_W - 1, D_B), F32)],
        scratch_shapes=[pltpu.VMEM((H_A, DV, DK), F32), pltpu.VMEM((SUBLANES + tb, D_B), F32)],
        compiler_params=pltpu.CompilerParams(
            dimension_semantics=("arbitrary", "arbitrary"), vmem_limit_bytes=VMEM_LIMIT),
        name="mix_seq",
    )(*proj, st0, tail0, gn.reshape(1, -1), cw, jnp.asarray(tril))


GROUP = 16
S_LEN = 4
S_MID = 2


def _mix_group_kernel(q_ref, lf_ref, k_ref, v_ref, sg_ref, gb_ref, u_ref,
                      s_ref, cbuf_ref, gn_ref, cw_ref, lmat_ref, amask_ref,
                      out_ref, sfin_ref, cfin_ref, full_scr, y_scr):
    cw = cw_ref[...]
    nb = CONV_W - 1
    for s in range(GROUP):
        r0 = SUBLANES * s
        full_scr[r0:r0 + nb, :] = cbuf_ref[s]
        full_scr[r0 + nb:r0 + nb + S_LEN, :] = u_ref[S_LEN * s:S_LEN * (s + 1), :]
    for s in range(GROUP):
        r0 = SUBLANES * s
        f0 = full_scr[r0:r0 + S_LEN, :]
        f1 = full_scr[r0 + 1:r0 + 1 + S_LEN, :]
        f2 = full_scr[r0 + 2:r0 + 2 + S_LEN, :]
        y_scr[S_LEN * s:S_LEN * (s + 1), :] = cw[0:1] * f0 + cw[1:2] * f1 + cw[2:3] * f2
        cfin_ref[s] = full_scr[r0 + S_LEN:r0 + S_LEN + nb, :]
    out_ref[:, D_A:] = (gb_ref[...].astype(F32) * y_scr[...]).astype(BF16)

    b3 = jnp.dot(lmat_ref[...], _split3(lf_ref[...]), preferred_element_type=F32) * LOG2E
    causal = amask_ref[...] > 0.0
    gn = gn_ref[...]
    sub = BF16_ROWS
    per_sub = sub // S_LEN
    rid = jax.lax.broadcasted_iota(jnp.int32, (sub, DK), 0)
    own = [(rid >= S_LEN * j) & (rid < S_LEN * (j + 1)) for j in range(per_sub)]
    ones_blk = jnp.ones((sub, DV), BF16)
    zeros_blk = jnp.zeros((sub, DV), BF16)
    for h in range(H_A):
        sl = slice(h * DK, (h + 1) * DK)
        b = b3[0:CHUNK, sl]
        bm = b3[CHUNK:2 * CHUNK, sl]
        bl = b3[2 * CHUNK:3 * CHUNK, sl]
        q1, k1, q2, k2 = _head_tiles(q_ref[:, sl].astype(F32), k_ref[:, sl].astype(F32),
                                     b, bm, bl)
        vb = v_ref[:, sl]
        a = pl.dot(q1, k1, trans_b=True)
        a = jnp.where(causal, a, 0.0).astype(BF16)
        o1 = jnp.dot(a, vb, preferred_element_type=F32)
        decay = jnp.exp2(bl)
        for blk in range(CHUNK // sub):
            rs = slice(blk * sub, (blk + 1) * sub)
            q2b, k2b, vbb = q2[rs], k2[rs], vb[rs]
            rhs = jnp.concatenate([jnp.concatenate([vbb, zeros_blk], axis=1),
                                   jnp.concatenate([zeros_blk, ones_blk], axis=1)], axis=0)
            acc = o1[rs]
            for j in range(per_sub):
                s = blk * per_sub + j
                st = s_ref[s, h]
                qm = jnp.where(own[j], q2b, 0.0).astype(BF16)
                acc = acc + jnp.dot(qm, st.astype(BF16), preferred_element_type=F32)
                drows = _decay_rows(decay[S_LEN * s:S_LEN * s + 1, :], rid)
                lhs = jnp.concatenate([jnp.where(own[j], k2b, 0.0), drows], axis=0).astype(BF16)
                ud = pl.dot(lhs, rhs, trans_a=True)
                sfin_ref[s, h] = ud[:, DV:] * st + ud[:, :DV]
            out_ref[rs, sl] = _gated_rmsnorm(acc, gn[:, sl], sg_ref[rs, sl].astype(F32))


def _group_mats():
    r = np.arange(CHUNK)
    seq, pos = r // S_LEN, r % S_LEN
    same = seq[:, None] == seq[None, :]
    cum = same & (pos[None, :] <= pos[:, None])
    midm = same & (pos[None, :] <= S_MID)
    lmat = np.concatenate([cum, midm, same], axis=0).astype(np.float32)
    lmat = np.concatenate([lmat, lmat, lmat], axis=1)
    return jnp.asarray(lmat, BF16), jnp.asarray(cum.astype(np.float32))


def _mix_group(proj, nseq, s0, cbuf, gn, cw):
    steps = nseq // GROUP
    seg = lambda: pl.BlockSpec((CHUNK, SEG), lambda i: (i, 0))
    const2 = lambda i: (0, 0)
    lmat, amask = _group_mats()
    return pl.pallas_call(
        _mix_group_kernel,
        grid=(steps,),
        in_specs=[seg() for _ in range(7)] + [
                  pl.BlockSpec((GROUP, H_A, DK, DV), lambda i: (i, 0, 0, 0)),
                  pl.BlockSpec((GROUP, CONV_W - 1, D_B), lambda i: (i, 0, 0)),
                  pl.BlockSpec((1, D_A), const2),
                  pl.BlockSpec((CONV_W, D_B), const2),
                  pl.BlockSpec((3 * CHUNK, 3 * CHUNK), const2),
                  pl.BlockSpec((CHUNK, CHUNK), const2)],
        out_specs=[pl.BlockSpec((CHUNK, D_MODEL), lambda i: (i, 0)),
                   pl.BlockSpec((GROUP, H_A, DK, DV), lambda i: (i, 0, 0, 0)),
                   pl.BlockSpec((GROUP, CONV_W - 1, D_B), lambda i: (i, 0, 0))],
        out_shape=[jax.ShapeDtypeStruct((nseq * S_LEN, D_MODEL), BF16),
                   jax.ShapeDtypeStruct((nseq, H_A, DK, DV), F32),
                   jax.ShapeDtypeStruct((nseq, CONV_W - 1, D_B), F32)],
        scratch_shapes=[pltpu.VMEM((SUBLANES * GROUP, D_B), F32), pltpu.VMEM((CHUNK, D_B), F32)],
        compiler_params=pltpu.CompilerParams(
            dimension_semantics=("arbitrary",), vmem_limit_bytes=VMEM_LIMIT),
        name="mix_group",
    )(*proj, s0, cbuf, gn.reshape(1, -1), cw, lmat, amask)


def _outproj_kernel(x_ref, mix_ref, wo_ref, g0_ref, b0_ref, g1_ref, b1_ref, h_ref):
    xn = _layernorm(x_ref[...], g0_ref[...], b0_ref[...])
    m = jnp.dot(mix_ref[...], wo_ref[...], preferred_element_type=F32)
    h_ref[...] = _layernorm(ALPHA * xn + m, g1_ref[...], b1_ref[...])


def _outproj(x, mix, w_o, g0, b0, g1, b1, tm):
    rows = x.shape[0]
    tm = min(tm, rows)
    assert rows % tm == 0
    row = lambda a: a.reshape(1, -1)
    const = lambda i: (0, 0)
    vec = pl.BlockSpec((1, D_MODEL), const)
    return pl.pallas_call(
        _outproj_kernel,
        grid=(rows // tm,),
        in_specs=[pl.BlockSpec((tm, D_MODEL), lambda i: (i, 0)),
                  pl.BlockSpec((tm, D_MODEL), lambda i: (i, 0)),
                  pl.BlockSpec((D_MODEL, D_MODEL), const),
                  vec, vec, vec, vec],
        out_specs=pl.BlockSpec((tm, D_MODEL), lambda i: (i, 0)),
        out_shape=jax.ShapeDtypeStruct((rows, D_MODEL), F32),
        compiler_params=pltpu.CompilerParams(
            dimension_semantics=("arbitrary",), vmem_limit_bytes=VMEM_LIMIT),
        name="outproj",
    )(x, mix, w_o, row(g0), row(b0), row(g1), row(b1))


def _ffn_kernel(h_ref, wg_ref, wu_ref, wd_ref, g2_ref, b2_ref, o_ref, hb_ref):
    f = pl.program_id(1)

    @pl.when(f == 0)
    def _():
        h = h_ref[...]
        hb_ref[...] = h.astype(BF16)
        o_ref[...] = ALPHA * h

    hb = hb_ref[...]
    g = jnp.dot(hb, wg_ref[...], preferred_element_type=F32)
    u = jnp.dot(hb, wu_ref[...], preferred_element_type=F32)
    a = (_silu(g) * u).astype(BF16)
    o_ref[...] += jnp.dot(a, wd_ref[...], preferred_element_type=F32)

    @pl.when(f == pl.num_programs(1) - 1)
    def _():
        o_ref[...] = _layernorm(o_ref[...], g2_ref[...], b2_ref[...])


def _ffn(h, w_gate, w_up, w_down, g2, b2, tm):
    rows = h.shape[0]
    nf, _, tf = w_gate.shape
    assert rows % tm == 0 and nf * tf == D_FF and w_up.shape == w_gate.shape
    row = lambda a: a.reshape(1, -1)
    vec = pl.BlockSpec((1, D_MODEL), lambda i, f: (0, 0))
    return pl.pallas_call(
        _ffn_kernel,
        grid=(rows // tm, nf),
        in_specs=[pl.BlockSpec((tm, D_MODEL), lambda i, f: (i, 0)),
                  pl.BlockSpec((None, D_MODEL, tf), lambda i, f: (f, 0, 0)),
                  pl.BlockSpec((None, D_MODEL, tf), lambda i, f: (f, 0, 0)),
                  pl.BlockSpec((tf, D_MODEL), lambda i, f: (f, 0)),
                  vec, vec],
        out_specs=pl.BlockSpec((tm, D_MODEL), lambda i, f: (i, 0)),
        out_shape=jax.ShapeDtypeStruct((rows, D_MODEL), F32),
        scratch_shapes=[pltpu.VMEM((tm, D_MODEL), BF16)],
        compiler_params=pltpu.CompilerParams(
            dimension_semantics=("arbitrary", "arbitrary"), vmem_limit_bytes=VMEM_LIMIT),
        name="ffn",
    )(h, w_gate, w_up, w_down, row(g2), row(b2))


def kernel(x_prompt, x_sample, state_hgrn, state_conv, meta_tokens, ln0_g, ln0_b, w_in, b_f, lb_param, gnorm_g, conv_w, w_o, ln1_g, ln1_b, w_gate, w_up, w_down, ln2_g, ln2_b):
    bp, seq, _ = x_prompt.shape
    bs, dseq, _ = x_sample.shape
    assert dseq == S_LEN and seq % CHUNK == 0 and bs % GROUP == 0

    lb = jnp.cumsum(jax.nn.softmax(lb_param.astype(F32), axis=0), axis=0)[0]
    g0, b0 = ln0_g.astype(F32), ln0_b.astype(F32)

    xp = x_prompt.reshape(bp * seq, D_MODEL)
    xs = x_sample.reshape(bs * dseq, D_MODEL)
    xs_ext = jnp.concatenate(
        [xs, meta_tokens.astype(F32), jnp.zeros((CHUNK - N_META, D_MODEL), F32)], axis=0)

    proj_s, w_in_b = _inproj_cast(xs_ext, g0, b0, w_in[0], b_f[0], lb)
    proj_p, (w_o_b, wg_b, wu_b, wd_b) = _inproj(
        xp, g0, b0, w_in_b, b_f[0], lb, tm=INPROJ_TM,
        cast=((w_o[0], None), (w_gate[0], FFN_TF), (w_up[0], FFN_TF), (w_down[0], None)))

    zero_st = jnp.zeros((H_A, DV, DK), F32)
    zero_tail = jnp.zeros((CONV_W - 1, D_B), F32)
    _, st_meta, tail_meta = _mix_seq(
        proj_s, bs * dseq, 1, CHUNK, CHUNK, zero_st, zero_tail, gnorm_g[0], conv_w[0],
        mid=N_META // 2, valid_rows=N_META, state_transposed_out=True)

    mix_p, hgrn_p, conv_p = _mix_seq(
        proj_p, 0, bp, seq, MIX_TB, st_meta[0], tail_meta[0], gnorm_g[0], conv_w[0],
        mid=CHUNK // 2)
    mix_s, hgrn_s, conv_s = _mix_group(
        proj_s, bs, state_hgrn[0], state_conv[0], gnorm_g[0], conv_w[0])

    h_p = _outproj(xp, mix_p, w_o_b, g0, b0, ln1_g[0], ln1_b[0], tm=OUTPROJ_TM)
    h_s = _outproj(xs, mix_s, w_o_b, g0, b0, ln1_g[0], ln1_b[0], tm=OUTPROJ_TM)

    y_p = _ffn(h_p, wg_b, wu_b, wd_b, ln2_g[0], ln2_b[0], tm=FFN_TM)
    y_s = _ffn(h_s, wg_b, wu_b, wd_b, ln2_g[0], ln2_b[0], tm=FFN_TM_SMALL)

    return (y_p.reshape(bp, seq, D_MODEL), y_s.reshape(bs, dseq, D_MODEL),
            hgrn_p[None], conv_p[None], hgrn_s[None], conv_s[None])
```

```python
import functools

import numpy as np
import jax
import jax.numpy as jnp
from jax.experimental import pallas as pl
from jax.experimental.pallas import tpu as pltpu

F32 = jnp.float32
BF16 = jnp.bfloat16

D_MODEL = 2048
D_A = 1024
D_B = 1024
DK = 128
DV = 128
H_A = 8
SEG = 1024
N_SEG = 7
N_META = 16
CHUNK = 64
D_FF = 5632
CONV_W = 3
ALPHA = 2.0 ** 0.25
LN_EPS = 1e-5
RMS_EPS = 1e-6

V7X_VMEM_BYTES = 64 * 1024 * 1024
SUBLANES = 8
BF16_ROWS = 16
VMEM_LIMIT = V7X_VMEM_BYTES - 2 * 1024 * 1024

INPROJ_TM = 256
MIX_TB = 512
OUTPROJ_TM = 512
FFN_TM = 1024
FFN_TM_SMALL = 512
FFN_TF = 512

CONV_TAIL = SUBLANES - (CONV_W - 1)


def _layernorm(x, g, b):
    mu = jnp.mean(x, axis=-1, keepdims=True)
    xc = x - mu
    var = jnp.mean(xc * xc, axis=-1, keepdims=True)
    return xc * jax.lax.rsqrt(var + LN_EPS) * g + b


def _silu(x):
    return x * (1.0 / (1.0 + jnp.exp(-x)))


INPROJ_SUB = 256


def _inproj_kernel(x_ref, g0_ref, b0_ref, w_ref, bf_ref, lb_ref, *refs, n_cast):
    cast_in, refs = refs[:n_cast], refs[n_cast:]
    (q_ref, lf_ref, k_ref, v_ref, sg_ref, gb_ref, u_ref), cast_out = refs[:7], refs[7:]
    for src, dst in zip(cast_in, cast_out, strict=True):
        dst[...] = src[...].astype(BF16)

    for r in range(x_ref.shape[0] // INPROJ_SUB):
        rs = slice(r * INPROJ_SUB, (r + 1) * INPROJ_SUB)
        xn = _layernorm(x_ref[rs, :], g0_ref[...], b0_ref[...]).astype(BF16)

        def seg(j, xn=xn):
            return jnp.dot(xn, w_ref[:, j * SEG:(j + 1) * SEG], preferred_element_type=F32)

        u_ref[rs, :] = seg(5) * seg(6)
        q_ref[rs, :] = _silu(seg(0)).astype(BF16)

        lf, kk = _forget_gate(seg(1) + bf_ref[...], lb_ref[...])
        lf_ref[rs, :] = lf
        k_ref[rs, :] = kk.astype(BF16)

        sg_ref[rs, :] = _silu(seg(3)).astype(BF16)
        gb_ref[rs, :] = seg(4).astype(BF16)
        v_ref[rs, :] = seg(2).astype(BF16)


def _inproj(x, g0, b0, w_in, b_f, lb, tm, cast=()):
    rows = x.shape[0]
    assert rows % tm == 0 and tm % INPROJ_SUB == 0
    steps = rows // tm
    row = lambda a: a.reshape(1, -1)
    const = lambda i: (0, 0)
    out = lambda: pl.BlockSpec((tm, SEG), lambda i: (i, 0))
    sds = lambda dt: jax.ShapeDtypeStruct((rows, SEG), dt)
    cast_specs = []
    for w in cast:
        assert w.shape[0] % (BF16_ROWS * steps) == 0
        cast_specs.append(pl.BlockSpec((w.shape[0] // steps, w.shape[1]), lambda i: (i, 0)))
    res = pl.pallas_call(
        functools.partial(_inproj_kernel, n_cast=len(cast)),
        grid=(steps,),
        in_specs=[
            pl.BlockSpec((tm, D_MODEL), lambda i: (i, 0)),
            pl.BlockSpec((1, D_MODEL), const),
            pl.BlockSpec((1, D_MODEL), const),
            pl.BlockSpec((D_MODEL, N_SEG * SEG), const, pipeline_mode=pl.Buffered(1)),
            pl.BlockSpec((1, SEG), const),
            pl.BlockSpec((1, SEG), const),
        ] + cast_specs,
        out_specs=[out() for _ in range(7)] + cast_specs,
        out_shape=[sds(BF16), sds(F32), sds(BF16), sds(BF16), sds(BF16), sds(BF16), sds(F32)]
        + [jax.ShapeDtypeStruct(w.shape, BF16) for w in cast],
        compiler_params=pltpu.CompilerParams(
            dimension_semantics=("arbitrary",), vmem_limit_bytes=VMEM_LIMIT),
        name="inproj",
    )(x, row(g0), row(b0), w_in, row(b_f), row(lb), *cast)
    return res[:7], res[7:]


def _forget_gate(z, lb):
    e = jnp.exp(-jnp.abs(z))
    r = 1.0 / (1.0 + e)
    er = e * r
    pos = z >= 0.0
    return jnp.log(lb + (1.0 - lb) * jnp.where(pos, r, er)), (1.0 - lb) * jnp.where(pos, er, r)


def _inproj_cast_kernel(x_ref, meta_ref, g0_ref, b0_ref, w_ref, bf_ref, lbp_ref,
                        q_ref, lf_ref, k_ref, v_ref, sg_ref, gb_ref, u_ref, wb_ref, lb_ref,
                        xn_scr, gc_scr):
    j = pl.program_id(0)

    @pl.when(j == 0)
    def _():
        n_x, n_m = x_ref.shape[0], meta_ref.shape[0]
        xn_scr[0:n_x, :] = _layernorm(x_ref[...], g0_ref[...], b0_ref[...]).astype(BF16)
        xn_scr[n_x:n_x + n_m, :] = _layernorm(meta_ref[...], g0_ref[...],
                                              b0_ref[...]).astype(BF16)
        xn_scr[n_x + n_m:, :] = jnp.zeros((xn_scr.shape[0] - n_x - n_m, D_MODEL), BF16)
        p = lbp_ref[...]
        e = jnp.exp(p - jnp.max(p, axis=0, keepdims=True))
        lb_ref[...] = e[0:1, :] / jnp.sum(e, axis=0, keepdims=True)

    wb = w_ref[...].astype(BF16)
    wb_ref[...] = wb
    acc = jnp.dot(xn_scr[...], wb, preferred_element_type=F32)

    @pl.when(j == 0)
    def _():
        q_ref[...] = _silu(acc).astype(BF16)

    @pl.when(j == 1)
    def _():
        lf, kk = _forget_gate(acc + bf_ref[...], lb_ref[...])
        lf_ref[...] = lf
        k_ref[...] = kk.astype(BF16)

    @pl.when(j == 2)
    def _():
        v_ref[...] = acc.astype(BF16)

    @pl.when(j == 3)
    def _():
        sg_ref[...] = _silu(acc).astype(BF16)

    @pl.when(j == 4)
    def _():
        gb_ref[...] = acc.astype(BF16)

    @pl.when(j == 5)
    def _():
        gc_scr[...] = acc

    @pl.when(j == 6)
    def _():
        u_ref[...] = gc_scr[...] * acc


def _inproj_cast(x, meta, g0, b0, w_in, b_f, lb_param):
    assert x.shape[0] % BF16_ROWS == 0 and meta.shape[0] % BF16_ROWS == 0
    rows = x.shape[0] + CHUNK
    row = lambda a: a.reshape(1, -1)
    const = lambda j: (0, 0)
    out = lambda: pl.BlockSpec((rows, SEG), const)
    sds = lambda dt: jax.ShapeDtypeStruct((rows, SEG), dt)
    wseg = lambda: pl.BlockSpec((D_MODEL, SEG), lambda j: (0, j))
    res = pl.pallas_call(
        _inproj_cast_kernel,
        grid=(N_SEG,),
        in_specs=[pl.BlockSpec(x.shape, const),
                  pl.BlockSpec(meta.shape, const),
                  pl.BlockSpec((1, D_MODEL), const),
                  pl.BlockSpec((1, D_MODEL), const),
                  wseg(),
                  pl.BlockSpec((1, SEG), const),
                  pl.BlockSpec(lb_param.shape, const)],
        out_specs=[out() for _ in range(7)] + [wseg(), pl.BlockSpec((1, SEG), const)],
        out_shape=[sds(BF16), sds(F32), sds(BF16), sds(BF16), sds(BF16), sds(BF16), sds(F32),
                   jax.ShapeDtypeStruct(w_in.shape, BF16), jax.ShapeDtypeStruct((1, SEG), F32)],
        scratch_shapes=[pltpu.VMEM((rows, D_MODEL), BF16), pltpu.VMEM((rows, SEG), F32)],
        compiler_params=pltpu.CompilerParams(
            dimension_semantics=("arbitrary",), vmem_limit_bytes=VMEM_LIMIT),
        name="inproj_cast",
    )(x, meta, row(g0), row(b0), w_in, row(b_f), lb_param.astype(F32))
    return res[:7], res[7], res[8]


def _split3(x):
    hi = x.astype(BF16)
    r1 = x - hi.astype(F32)
    mid = r1.astype(BF16)
    lo = (r1 - mid.astype(F32)).astype(BF16)
    return jnp.concatenate([hi, mid, lo], axis=0)


def _cumsum_rows(x):
    rows, lanes = x.shape
    n = rows // SUBLANES
    x3 = x.reshape(n, SUBLANES, lanes)
    pos = jax.lax.broadcasted_iota(jnp.int32, x3.shape, 1)
    s = 1
    while s < SUBLANES:
        x3 = x3 + jnp.where(pos >= s, pltpu.roll(x3, s, 1), 0.0)
        s *= 2
    carry = jnp.zeros((1, lanes), F32)
    out = []
    for g in range(n):
        out.append(x3[g] + carry)
        carry = carry + x3[g, SUBLANES - 1:SUBLANES, :]
    return jnp.concatenate(out, axis=0)


LOG2E = 1.4426950408889634


def _head_tiles(qt, kk, b, bm, bl):
    q1 = (qt * jnp.exp2(b - bm)).astype(BF16)
    k1 = (kk * jnp.exp2(bm - b)).astype(BF16)
    q2 = qt * jnp.exp2(b)
    k2 = kk * jnp.exp2(bl - b)
    return q1, k1, q2, k2


DECAY_ROWS = BF16_ROWS


def _decay_rows(d, rid):
    d_hi = d.astype(BF16).astype(F32)
    d_r = d - d_hi
    d_mid = d_r.astype(BF16).astype(F32)
    return jnp.where(rid == 0, d_hi, jnp.where(rid == 1, d_mid,
                     jnp.where(rid == 2, d_r - d_mid, 0.0)))


def _gated_rmsnorm(o, gn, sg):
    ms = jnp.mean(o * o, axis=-1, keepdims=True)
    return (o * jax.lax.rsqrt(ms + RMS_EPS) * gn * sg).astype(BF16)


def _mix_seq_kernel(q_ref, lf_ref, k_ref, v_ref, sg_ref, gb_ref, u_ref,
                    st0_ref, tail0_ref, gn_ref, cw_ref, amask_ref,
                    out_ref, sfin_ref, cfin_ref, st_scr, ubuf,
                    *, tb, mid, valid_rows, state_transposed_out):
    t = pl.program_id(1)

    @pl.when(t == 0)
    def _():
        st_scr[...] = st0_ref[...]
        ubuf[CONV_TAIL:SUBLANES, :] = tail0_ref[...]

    u = u_ref[...]
    ubuf[SUBLANES:SUBLANES + tb, :] = u
    cw = cw_ref[...]
    conv = (cw[0:1] * ubuf[CONV_TAIL:CONV_TAIL + tb, :]
            + cw[1:2] * ubuf[CONV_TAIL + 1:CONV_TAIL + 1 + tb, :] + cw[2:3] * u)
    out_ref[:, D_A:] = (gb_ref[...].astype(F32) * conv).astype(BF16)
    last = valid_rows if valid_rows is not None else tb
    ubuf[CONV_TAIL:SUBLANES, :] = ubuf[CONV_TAIL + last:SUBLANES + last, :]

    causal = amask_ref[...] > 0.0
    gn = gn_ref[...]
    if valid_rows is not None:
        rowmask = jax.lax.broadcasted_iota(jnp.int32, (CHUNK, SEG), 0) < valid_rows
        rowmask_h = jax.lax.broadcasted_iota(jnp.int32, (CHUNK, DK), 0) < valid_rows
    for ci in range(tb // CHUNK):
        rows = slice(ci * CHUNK, (ci + 1) * CHUNK)
        lf = lf_ref[rows, :]
        if valid_rows is not None:
            lf = jnp.where(rowmask, lf, 0.0)
        b_all = _cumsum_rows(lf) * LOG2E
        for h in range(H_A):
            sl = slice(h * DK, (h + 1) * DK)
            b = b_all[:, sl]
            bm = b[mid:mid + 1, :]
            bl = b[CHUNK - 1:CHUNK, :]
            kk = k_ref[rows, sl].astype(F32)
            if valid_rows is not None:
                kk = jnp.where(rowmask_h, kk, 0.0)
            q1, k1, q2, k2 = _head_tiles(q_ref[rows, sl].astype(F32), kk, b, bm, bl)
            vb = v_ref[rows, sl]
            st = st_scr[h]
            a = pl.dot(q1, k1, trans_b=True)
            a = jnp.where(causal, a, 0.0).astype(BF16)
            o = (jnp.dot(a, vb, preferred_element_type=F32)
                 + jnp.dot(q2.astype(BF16), st.T.astype(BF16), preferred_element_type=F32))
            st_scr[h] = st * jnp.exp2(bl) + pl.dot(vb, k2.astype(BF16), trans_a=True)
            out_ref[rows, sl] = _gated_rmsnorm(o, gn[:, sl], sg_ref[rows, sl].astype(F32))

    @pl.when(t == pl.num_programs(1) - 1)
    def _():
        for h in range(H_A):
            sfin_ref[0, h] = st_scr[h] if state_transposed_out else st_scr[h].T
        cfin_ref[0] = ubuf[CONV_TAIL:SUBLANES, :]


def _mix_seq(proj, row0, nseq, seq_len, tb, st0, tail0, gn, cw, *, mid, valid_rows=None,
             state_transposed_out=False):
    nt = seq_len // tb
    assert seq_len % tb == 0 and row0 % tb == 0 and tb % CHUNK == 0
    rb0 = row0 // tb
    seg = lambda: pl.BlockSpec((tb, SEG), lambda b, t: (rb0 + b * nt + t, 0))
    const2 = lambda b, t: (0, 0)
    tril = np.tril(np.ones((CHUNK, CHUNK), np.float32))
    kern = functools.partial(_mix_seq_kernel, tb=tb, mid=mid, valid_rows=valid_rows,
                             state_transposed_out=state_transposed_out)
    return pl.pallas_call(
        kern,
        grid=(nseq, nt),
        in_specs=[seg() for _ in range(7)] + [
                  pl.BlockSpec((H_A, DV, DK), lambda b, t: (0, 0, 0)),
                  pl.BlockSpec((CONV_W - 1, D_B), const2),
                  pl.BlockSpec((1, D_A), const2),
                  pl.BlockSpec((CONV_W, D_B), const2),
                  pl.BlockSpec((CHUNK, CHUNK), const2)],
        out_specs=[pl.BlockSpec((tb, D_MODEL), lambda b, t: (b * nt + t, 0)),
                   pl.BlockSpec((1, H_A, DK, DV), lambda b, t: (b, 0, 0, 0)),
                   pl.BlockSpec((1, CONV_W - 1, D_B), lambda b, t: (b, 0, 0))],
        out_shape=[jax.ShapeDtypeStruct((nseq * seq_len, D_MODEL), BF16),
                   jax.ShapeDtypeStruct((nseq, H_A, DK, DV), F32),
                   jax.ShapeDtypeStruct((nseq, CONV_W - 1, D_B), F32)],
        scratch_shapes=[pltpu.VMEM((H_A, DV, DK), F32), pltpu.VMEM((SUBLANES + tb, D_B), F32)],
        compiler_params=pltpu.CompilerParams(
            dimension_semantics=("arbitrary", "arbitrary"), vmem_limit_bytes=VMEM_LIMIT),
        name="mix_seq",
    )(*proj, st0, tail0, gn.reshape(1, -1), cw, jnp.asarray(tril))


GROUP = 16
S_LEN = 4
S_MID = 2


def _mix_group_kernel(q_ref, lf_ref, k_ref, v_ref, sg_ref, gb_ref, u_ref,
                      s_ref, cbuf_ref, gn_ref, cw_ref, lmat_ref, amask_ref,
                      out_ref, sfin_ref, cfin_ref, full_scr, y_scr):
    cw = cw_ref[...]
    nb = CONV_W - 1
    for s in range(GROUP):
        r0 = SUBLANES * s
        full_scr[r0:r0 + nb, :] = cbuf_ref[s]
        full_scr[r0 + nb:r0 + nb + S_LEN, :] = u_ref[S_LEN * s:S_LEN * (s + 1), :]
    for s in range(GROUP):
        r0 = SUBLANES * s
        f0 = full_scr[r0:r0 + S_LEN, :]
        f1 = full_scr[r0 + 1:r0 + 1 + S_LEN, :]
        f2 = full_scr[r0 + 2:r0 + 2 + S_LEN, :]
        y_scr[S_LEN * s:S_LEN * (s + 1), :] = cw[0:1] * f0 + cw[1:2] * f1 + cw[2:3] * f2
        cfin_ref[s] = full_scr[r0 + S_LEN:r0 + S_LEN + nb, :]
    out_ref[:, D_A:] = (gb_ref[...].astype(F32) * y_scr[...]).astype(BF16)

    b3 = jnp.dot(lmat_ref[...], _split3(lf_ref[...]), preferred_element_type=F32) * LOG2E
    causal = amask_ref[...] > 0.0
    gn = gn_ref[...]
    sub = BF16_ROWS
    per_sub = sub // S_LEN
    rid = jax.lax.broadcasted_iota(jnp.int32, (sub, DK), 0)
    own = [(rid >= S_LEN * j) & (rid < S_LEN * (j + 1)) for j in range(per_sub)]
    ones_blk = jnp.ones((sub, DV), BF16)
    zeros_blk = jnp.zeros((sub, DV), BF16)
    for h in range(H_A):
        sl = slice(h * DK, (h + 1) * DK)
        b = b3[0:CHUNK, sl]
        bm = b3[CHUNK:2 * CHUNK, sl]
        bl = b3[2 * CHUNK:3 * CHUNK, sl]
        q1, k1, q2, k2 = _head_tiles(q_ref[:, sl].astype(F32), k_ref[:, sl].astype(F32),
                                     b, bm, bl)
        vb = v_ref[:, sl]
        a = pl.dot(q1, k1, trans_b=True)
        a = jnp.where(causal, a, 0.0).astype(BF16)
        o1 = jnp.dot(a, vb, preferred_element_type=F32)
        decay = jnp.exp2(bl)
        for blk in range(CHUNK // sub):
            rs = slice(blk * sub, (blk + 1) * sub)
            q2b, k2b, vbb = q2[rs], k2[rs], vb[rs]
            rhs = jnp.concatenate([jnp.concatenate([vbb, zeros_blk], axis=1),
                                   jnp.concatenate([zeros_blk, ones_blk], axis=1)], axis=0)
            acc = o1[rs]
            for j in range(per_sub):
                s = blk * per_sub + j
                st = s_ref[s, h]
                qm = jnp.where(own[j], q2b, 0.0).astype(BF16)
                acc = acc + jnp.dot(qm, st.astype(BF16), preferred_element_type=F32)
                drows = _decay_rows(decay[S_LEN * s:S_LEN * s + 1, :], rid)
                lhs = jnp.concatenate([jnp.where(own[j], k2b, 0.0), drows], axis=0).astype(BF16)
                ud = pl.dot(lhs, rhs, trans_a=True)
                sfin_ref[s, h] = ud[:, DV:] * st + ud[:, :DV]
            out_ref[rs, sl] = _gated_rmsnorm(acc, gn[:, sl], sg_ref[rs, sl].astype(F32))


def _group_mats():
    r = np.arange(CHUNK)
    seq, pos = r // S_LEN, r % S_LEN
    same = seq[:, None] == seq[None, :]
    cum = same & (pos[None, :] <= pos[:, None])
    midm = same & (pos[None, :] <= S_MID)
    lmat = np.concatenate([cum, midm, same], axis=0).astype(np.float32)
    lmat = np.concatenate([lmat, lmat, lmat], axis=1)
    return jnp.asarray(lmat, BF16), jnp.asarray(cum.astype(np.float32))


def _mix_group(proj, nseq, s0, cbuf, gn, cw):
    steps = nseq // GROUP
    seg = lambda: pl.BlockSpec((CHUNK, SEG), lambda i: (i, 0))
    const2 = lambda i: (0, 0)
    lmat, amask = _group_mats()
    return pl.pallas_call(
        _mix_group_kernel,
        grid=(steps,),
        in_specs=[seg() for _ in range(7)] + [
                  pl.BlockSpec((GROUP, H_A, DK, DV), lambda i: (i, 0, 0, 0)),
                  pl.BlockSpec((GROUP, CONV_W - 1, D_B), lambda i: (i, 0, 0)),
                  pl.BlockSpec((1, D_A), const2),
                  pl.BlockSpec((CONV_W, D_B), const2),
                  pl.BlockSpec((3 * CHUNK, 3 * CHUNK), const2),
                  pl.BlockSpec((CHUNK, CHUNK), const2)],
        out_specs=[pl.BlockSpec((CHUNK, D_MODEL), lambda i: (i, 0)),
                   pl.BlockSpec((GROUP, H_A, DK, DV), lambda i: (i, 0, 0, 0)),
                   pl.BlockSpec((GROUP, CONV_W - 1, D_B), lambda i: (i, 0, 0))],
        out_shape=[jax.ShapeDtypeStruct((nseq * S_LEN, D_MODEL), BF16),
                   jax.ShapeDtypeStruct((nseq, H_A, DK, DV), F32),
                   jax.ShapeDtypeStruct((nseq, CONV_W - 1, D_B), F32)],
        scratch_shapes=[pltpu.VMEM((SUBLANES * GROUP, D_B), F32), pltpu.VMEM((CHUNK, D_B), F32)],
        compiler_params=pltpu.CompilerParams(
            dimension_semantics=("arbitrary",), vmem_limit_bytes=VMEM_LIMIT),
        name="mix_group",
    )(*proj, s0, cbuf, gn.reshape(1, -1), cw, lmat, amask)


def _outproj_kernel(x_ref, mix_ref, wo_ref, g0_ref, b0_ref, g1_ref, b1_ref, h_ref):
    xn = _layernorm(x_ref[...], g0_ref[...], b0_ref[...])
    m = jnp.dot(mix_ref[...], wo_ref[...], preferred_element_type=F32)
    h_ref[...] = _layernorm(ALPHA * xn + m, g1_ref[...], b1_ref[...])


def _outproj(x, mix, w_o, g0, b0, g1, b1, tm):
    rows = x.shape[0]
    tm = min(tm, rows)
    assert rows % tm == 0
    row = lambda a: a.reshape(1, -1)
    const = lambda i: (0, 0)
    vec = pl.BlockSpec((1, D_MODEL), const)
    return pl.pallas_call(
        _outproj_kernel,
        grid=(rows // tm,),
        in_specs=[pl.BlockSpec((tm, D_MODEL), lambda i: (i, 0)),
                  pl.BlockSpec((tm, D_MODEL), lambda i: (i, 0)),
                  pl.BlockSpec((D_MODEL, D_MODEL), const),
                  vec, vec, vec, vec],
        out_specs=pl.BlockSpec((tm, D_MODEL), lambda i: (i, 0)),
        out_shape=jax.ShapeDtypeStruct((rows, D_MODEL), F32),
        compiler_params=pltpu.CompilerParams(
            dimension_semantics=("arbitrary",), vmem_limit_bytes=VMEM_LIMIT),
        name="outproj",
    )(x, mix, w_o, row(g0), row(b0), row(g1), row(b1))


def _ffn_kernel(h_ref, wg_ref, wu_ref, wd_ref, g2_ref, b2_ref, o_ref, hb_ref):
    f = pl.program_id(1)

    @pl.when(f == 0)
    def _():
        h = h_ref[...]
        hb_ref[...] = h.astype(BF16)
        o_ref[...] = ALPHA * h

    hb = hb_ref[...]
    g = jnp.dot(hb, wg_ref[...], preferred_element_type=F32)
    u = jnp.dot(hb, wu_ref[...], preferred_element_type=F32)
    a = (_silu(g) * u).astype(BF16)
    o_ref[...] += jnp.dot(a, wd_ref[...], preferred_element_type=F32)

    @pl.when(f == pl.num_programs(1) - 1)
    def _():
        o_ref[...] = _layernorm(o_ref[...], g2_ref[...], b2_ref[...])


def _ffn(h, w_gate, w_up, w_down, g2, b2, tm, tf):
    rows = h.shape[0]
    assert rows % tm == 0 and D_FF % tf == 0
    row = lambda a: a.reshape(1, -1)
    vec = pl.BlockSpec((1, D_MODEL), lambda i, f: (0, 0))
    return pl.pallas_call(
        _ffn_kernel,
        grid=(rows // tm, D_FF // tf),
        in_specs=[pl.BlockSpec((tm, D_MODEL), lambda i, f: (i, 0)),
                  pl.BlockSpec((D_MODEL, tf), lambda i, f: (0, f)),
                  pl.BlockSpec((D_MODEL, tf), lambda i, f: (0, f)),
                  pl.BlockSpec((tf, D_MODEL), lambda i, f: (f, 0)),
                  vec, vec],
        out_specs=pl.BlockSpec((tm, D_MODEL), lambda i, f: (i, 0)),
        out_shape=jax.ShapeDtypeStruct((rows, D_MODEL), F32),
        scratch_shapes=[pltpu.VMEM((tm, D_MODEL), BF16)],
        compiler_params=pltpu.CompilerParams(
            dimension_semantics=("arbitrary", "arbitrary"), vmem_limit_bytes=VMEM_LIMIT),
        name="ffn",
    )(h, w_gate, w_up, w_down, row(g2), row(b2))


def kernel(x_prompt, x_sample, state_hgrn, state_conv, meta_tokens, ln0_g, ln0_b, w_in, b_f, lb_param, gnorm_g, conv_w, w_o, ln1_g, ln1_b, w_gate, w_up, w_down, ln2_g, ln2_b):
    bp, seq, _ = x_prompt.shape
    bs, dseq, _ = x_sample.shape
    assert dseq == S_LEN and seq % CHUNK == 0 and bs % GROUP == 0

    g0, b0 = ln0_g.astype(F32), ln0_b.astype(F32)

    xp = x_prompt.reshape(bp * seq, D_MODEL)
    xs = x_sample.reshape(bs * dseq, D_MODEL)
    proj_s, w_in_b, lb = _inproj_cast(xs, meta_tokens.astype(F32), g0, b0, w_in[0], b_f[0],
                                      lb_param)
    proj_p, (w_o_b, wg_b, wu_b, wd_b) = _inproj(
        xp, g0, b0, w_in_b, b_f[0], lb, tm=INPROJ_TM,
        cast=(w_o[0], w_gate[0], w_up[0], w_down[0]))

    zero_st = jnp.zeros((H_A, DV, DK), F32)
    zero_tail = jnp.zeros((CONV_W - 1, D_B), F32)
    _, st_meta, tail_meta = _mix_seq(
        proj_s, bs * dseq, 1, CHUNK, CHUNK, zero_st, zero_tail, gnorm_g[0], conv_w[0],
        mid=N_META // 2, valid_rows=N_META, state_transposed_out=True)

    mix_p, hgrn_p, conv_p = _mix_seq(
        proj_p, 0, bp, seq, MIX_TB, st_meta[0], tail_meta[0], gnorm_g[0], conv_w[0],
        mid=CHUNK // 2)
    mix_s, hgrn_s, conv_s = _mix_group(
        proj_s, bs, state_hgrn[0], state_conv[0], gnorm_g[0], conv_w[0])

    h_p = _outproj(xp, mix_p, w_o_b, g0, b0, ln1_g[0], ln1_b[0], tm=OUTPROJ_TM)
    h_s = _outproj(xs, mix_s, w_o_b, g0, b0, ln1_g[0], ln1_b[0], tm=OUTPROJ_TM)

    y_p = _ffn(h_p, wg_b, wu_b, wd_b, ln2_g[0], ln2_b[0], tm=FFN_TM, tf=FFN_TF)
    y_s = _ffn(h_s, wg_b, wu_b, wd_b, ln2_g[0], ln2_b[0], tm=FFN_TM_SMALL, tf=FFN_TF)

    return (y_p.reshape(bp, seq, D_MODEL), y_s.reshape(bs, dseq, D_MODEL),
            hgrn_p[None], conv_p[None], hgrn_s[None], conv_s[None])
```

```python
import functools

import numpy as np
import jax
import jax.numpy as jnp
from jax.experimental import pallas as pl
from jax.experimental.pallas import tpu as pltpu

F32 = jnp.float32
BF16 = jnp.bfloat16

D_MODEL = 2048
D_A = 1024
D_B = 1024
DK = 128
DV = 128
H_A = 8
SEG = 1024
N_SEG = 7
PB_Q, PB_K, PB_V, PB_SG, PB_GB = range(5)
PF_LF, PF_U = range(2)
PB_W, PF_W = 5 * SEG, 2 * SEG
N_META = 16
CHUNK = 64
D_FF = 5632
CONV_W = 3
ALPHA = 2.0 ** 0.25
LN_EPS = 1e-5
RMS_EPS = 1e-6

V7X_VMEM_BYTES = 64 * 1024 * 1024
SUBLANES = 8
BF16_ROWS = 16
VMEM_LIMIT = V7X_VMEM_BYTES - 2 * 1024 * 1024

INPROJ_TM = 256
MIX_TB = 512
OUTPROJ_TM = 512
FFN_TM = 1024
FFN_TM_SMALL = 512
FFN_TF = 512

CONV_TAIL = SUBLANES - (CONV_W - 1)


def _seg(i):
    return slice(i * SEG, (i + 1) * SEG)


def _layernorm(x, g, b):
    mu = jnp.mean(x, axis=-1, keepdims=True)
    xc = x - mu
    var = jnp.mean(xc * xc, axis=-1, keepdims=True)
    return xc * jax.lax.rsqrt(var + LN_EPS) * g + b


def _silu(x):
    return x * (1.0 / (1.0 + jnp.exp(-x)))


def _flatten_rows(x3_ref, dst_ref):
    n, slen, _ = x3_ref.shape
    for b in range(n):
        dst_ref[b * slen:(b + 1) * slen, :] = x3_ref[b]


def _unflatten_rows(src_ref, y3_ref):
    n, slen, _ = y3_ref.shape
    for b in range(n):
        y3_ref[b] = src_ref[b * slen:(b + 1) * slen, :]


INPROJ_SUB = 256


def _inproj_kernel(x_ref, g0_ref, b0_ref, w_ref, bf_ref, lb_ref, *refs, n_cast):
    cast_in, refs = refs[:n_cast], refs[n_cast:]
    (pb_ref, pf_ref), cast_out = refs[:2], refs[2:]
    for src, dst in zip(cast_in, cast_out, strict=True):
        dst[...] = src[...].astype(BF16)

    for r in range(x_ref.shape[0] // INPROJ_SUB):
        rs = slice(r * INPROJ_SUB, (r + 1) * INPROJ_SUB)
        xn = _layernorm(x_ref[rs, :], g0_ref[...], b0_ref[...]).astype(BF16)

        def seg(j, xn=xn):
            return jnp.dot(xn, w_ref[:, j * SEG:(j + 1) * SEG], preferred_element_type=F32)

        pf_ref[rs, _seg(PF_U)] = seg(5) * seg(6)
        pb_ref[rs, _seg(PB_Q)] = _silu(seg(0)).astype(BF16)

        lf, kk = _forget_gate(seg(1) + bf_ref[...], lb_ref[...])
        pf_ref[rs, _seg(PF_LF)] = lf
        pb_ref[rs, _seg(PB_K)] = kk.astype(BF16)

        pb_ref[rs, _seg(PB_SG)] = _silu(seg(3)).astype(BF16)
        pb_ref[rs, _seg(PB_GB)] = seg(4).astype(BF16)
        pb_ref[rs, _seg(PB_V)] = seg(2).astype(BF16)


def _inproj(x, g0, b0, w_in, b_f, lb, tm, cast=()):
    rows = x.shape[0]
    assert rows % tm == 0 and tm % INPROJ_SUB == 0
    steps = rows // tm
    row = lambda a: a.reshape(1, -1)
    const = lambda i: (0, 0)
    cast_specs = []
    for w in cast:
        assert w.shape[0] % (BF16_ROWS * steps) == 0
        cast_specs.append(pl.BlockSpec((w.shape[0] // steps, w.shape[1]), lambda i: (i, 0)))
    res = pl.pallas_call(
        functools.partial(_inproj_kernel, n_cast=len(cast)),
        grid=(steps,),
        in_specs=[
            pl.BlockSpec((tm, D_MODEL), lambda i: (i, 0)),
            pl.BlockSpec((1, D_MODEL), const),
            pl.BlockSpec((1, D_MODEL), const),
            pl.BlockSpec((D_MODEL, N_SEG * SEG), const, pipeline_mode=pl.Buffered(1)),
            pl.BlockSpec((1, SEG), const),
            pl.BlockSpec((1, SEG), const),
        ] + cast_specs,
        out_specs=[pl.BlockSpec((tm, PB_W), lambda i: (i, 0)),
                   pl.BlockSpec((tm, PF_W), lambda i: (i, 0))] + cast_specs,
        out_shape=[jax.ShapeDtypeStruct((rows, PB_W), BF16),
                   jax.ShapeDtypeStruct((rows, PF_W), F32)]
        + [jax.ShapeDtypeStruct(w.shape, BF16) for w in cast],
        compiler_params=pltpu.CompilerParams(
            dimension_semantics=("arbitrary",), vmem_limit_bytes=VMEM_LIMIT),
        name="inproj",
    )(x, row(g0), row(b0), w_in, row(b_f), row(lb), *cast)
    return res[:2], res[2:]


def _forget_gate(z, lb):
    e = jnp.exp(-jnp.abs(z))
    r = 1.0 / (1.0 + e)
    er = e * r
    pos = z >= 0.0
    return jnp.log(lb + (1.0 - lb) * jnp.where(pos, r, er)), (1.0 - lb) * jnp.where(pos, er, r)


def _inproj_cast_kernel(x_ref, meta_ref, g0_ref, b0_ref, w_ref, bf_ref, lbp_ref,
                        pb_ref, pf_ref, wb_ref, lb_ref, xn_scr, gc_scr, xs_scr):
    j = pl.program_id(0)

    @pl.when(j == 0)
    def _():
        n_x, n_m = xs_scr.shape[0], meta_ref.shape[0]
        _flatten_rows(x_ref, xs_scr)
        xn_scr[0:n_x, :] = _layernorm(xs_scr[...], g0_ref[...], b0_ref[...]).astype(BF16)
        xn_scr[n_x:n_x + n_m, :] = _layernorm(meta_ref[...], g0_ref[...],
                                              b0_ref[...]).astype(BF16)
        xn_scr[n_x + n_m:, :] = jnp.zeros((xn_scr.shape[0] - n_x - n_m, D_MODEL), BF16)
        p = lbp_ref[...]
        e = jnp.exp(p - jnp.max(p, axis=0, keepdims=True))
        lb_ref[...] = e[0:1, :] / jnp.sum(e, axis=0, keepdims=True)

    wb = w_ref[...].astype(BF16)
    wb_ref[...] = wb
    acc = jnp.dot(xn_scr[...], wb, preferred_element_type=F32)

    @pl.when(j == 0)
    def _():
        pb_ref[:, _seg(PB_Q)] = _silu(acc).astype(BF16)

    @pl.when(j == 1)
    def _():
        lf, kk = _forget_gate(acc + bf_ref[...], lb_ref[...])
        pf_ref[:, _seg(PF_LF)] = lf
        pb_ref[:, _seg(PB_K)] = kk.astype(BF16)

    @pl.when(j == 2)
    def _():
        pb_ref[:, _seg(PB_V)] = acc.astype(BF16)

    @pl.when(j == 3)
    def _():
        pb_ref[:, _seg(PB_SG)] = _silu(acc).astype(BF16)

    @pl.when(j == 4)
    def _():
        pb_ref[:, _seg(PB_GB)] = acc.astype(BF16)

    @pl.when(j == 5)
    def _():
        gc_scr[...] = acc

    @pl.when(j == 6)
    def _():
        pf_ref[:, _seg(PF_U)] = gc_scr[...] * acc


def _inproj_cast(x, meta, g0, b0, w_in, b_f, lb_param):
    n_x = x.shape[0] * x.shape[1]
    assert n_x % BF16_ROWS == 0 and meta.shape[0] % BF16_ROWS == 0
    rows = n_x + CHUNK
    row = lambda a: a.reshape(1, -1)
    const = lambda j: (0, 0)
    wseg = lambda: pl.BlockSpec((D_MODEL, SEG), lambda j: (0, j))
    res = pl.pallas_call(
        _inproj_cast_kernel,
        grid=(N_SEG,),
        in_specs=[pl.BlockSpec(x.shape, lambda j: (0, 0, 0), pipeline_mode=pl.Buffered(1)),
                  pl.BlockSpec(meta.shape, const),
                  pl.BlockSpec((1, D_MODEL), const),
                  pl.BlockSpec((1, D_MODEL), const),
                  wseg(),
                  pl.BlockSpec((1, SEG), const),
                  pl.BlockSpec(lb_param.shape, const)],
        out_specs=[pl.BlockSpec((rows, PB_W), const), pl.BlockSpec((rows, PF_W), const),
                   wseg(), pl.BlockSpec((1, SEG), const)],
        out_shape=[jax.ShapeDtypeStruct((rows, PB_W), BF16),
                   jax.ShapeDtypeStruct((rows, PF_W), F32),
                   jax.ShapeDtypeStruct(w_in.shape, BF16), jax.ShapeDtypeStruct((1, SEG), F32)],
        scratch_shapes=[pltpu.VMEM((rows, D_MODEL), BF16), pltpu.VMEM((rows, SEG), F32),
                        pltpu.VMEM((n_x, D_MODEL), F32)],
        compiler_params=pltpu.CompilerParams(
            dimension_semantics=("arbitrary",), vmem_limit_bytes=VMEM_LIMIT),
        name="inproj_cast",
    )(x, meta, row(g0), row(b0), w_in, row(b_f), lb_param.astype(F32))
    return res[:2], res[2], res[3]


def _split3(x):
    hi = x.astype(BF16)
    r1 = x - hi.astype(F32)
    mid = r1.astype(BF16)
    lo = (r1 - mid.astype(F32)).astype(BF16)
    return jnp.concatenate([hi, mid, lo], axis=0)


def _cumsum_rows(x):
    rows, lanes = x.shape
    n = rows // SUBLANES
    x3 = x.reshape(n, SUBLANES, lanes)
    pos = jax.lax.broadcasted_iota(jnp.int32, x3.shape, 1)
    s = 1
    while s < SUBLANES:
        x3 = x3 + jnp.where(pos >= s, pltpu.roll(x3, s, 1), 0.0)
        s *= 2
    carry = jnp.zeros((1, lanes), F32)
    out = []
    for g in range(n):
        out.append(x3[g] + carry)
        carry = carry + x3[g, SUBLANES - 1:SUBLANES, :]
    return jnp.concatenate(out, axis=0)


LOG2E = 1.4426950408889634


def _head_tiles(qt, kk, b, bm, bl):
    q1 = (qt * jnp.exp2(b - bm)).astype(BF16)
    k1 = (kk * jnp.exp2(bm - b)).astype(BF16)
    q2 = qt * jnp.exp2(b)
    k2 = kk * jnp.exp2(bl - b)
    return q1, k1, q2, k2


DECAY_ROWS = BF16_ROWS


def _decay_rows(d, rid):
    d_hi = d.astype(BF16).astype(F32)
    d_r = d - d_hi
    d_mid = d_r.astype(BF16).astype(F32)
    return jnp.where(rid == 0, d_hi, jnp.where(rid == 1, d_mid,
                     jnp.where(rid == 2, d_r - d_mid, 0.0)))


def _gated_rmsnorm(o, gn, sg):
    ms = jnp.mean(o * o, axis=-1, keepdims=True)
    return (o * jax.lax.rsqrt(ms + RMS_EPS) * gn * sg).astype(BF16)


def _mix_seq_kernel(pb_ref, pf_ref, st0_ref, tail0_ref, gn_ref, cw_ref, amask_ref,
                    out_ref, sfin_ref, cfin_ref, st_scr, ubuf,
                    *, tb, mid, valid_rows, state_transposed_out):
    t = pl.program_id(1)

    @pl.when(t == 0)
    def _():
        st_scr[...] = st0_ref[...]
        ubuf[CONV_TAIL:SUBLANES, :] = tail0_ref[...]

    u = pf_ref[:, _seg(PF_U)]
    ubuf[SUBLANES:SUBLANES + tb, :] = u
    cw = cw_ref[...]
    conv = (cw[0:1] * ubuf[CONV_TAIL:CONV_TAIL + tb, :]
            + cw[1:2] * ubuf[CONV_TAIL + 1:CONV_TAIL + 1 + tb, :] + cw[2:3] * u)
    out_ref[:, D_A:] = (pb_ref[:, _seg(PB_GB)].astype(F32) * conv).astype(BF16)
    last = valid_rows if valid_rows is not None else tb
    ubuf[CONV_TAIL:SUBLANES, :] = ubuf[CONV_TAIL + last:SUBLANES + last, :]

    causal = amask_ref[...] > 0.0
    gn = gn_ref[...]
    if valid_rows is not None:
        rowmask = jax.lax.broadcasted_iota(jnp.int32, (CHUNK, SEG), 0) < valid_rows
        rowmask_h = jax.lax.broadcasted_iota(jnp.int32, (CHUNK, DK), 0) < valid_rows
    state = [st_scr[h] for h in range(H_A)]
    for ci in range(tb // CHUNK):
        rows = slice(ci * CHUNK, (ci + 1) * CHUNK)
        lf = pf_ref[rows, _seg(PF_LF)]
        if valid_rows is not None:
            lf = jnp.where(rowmask, lf, 0.0)
        b_all = _cumsum_rows(lf) * LOG2E
        for h in range(H_A):
            sl = slice(h * DK, (h + 1) * DK)
            col = lambda seg_i, h=h: slice(seg_i * SEG + h * DK, seg_i * SEG + (h + 1) * DK)
            b = b_all[:, sl]
            bm = b[mid:mid + 1, :]
            bl = b[CHUNK - 1:CHUNK, :]
            kk = pb_ref[rows, col(PB_K)].astype(F32)
            if valid_rows is not None:
                kk = jnp.where(rowmask_h, kk, 0.0)
            q1, k1, q2, k2 = _head_tiles(pb_ref[rows, col(PB_Q)].astype(F32), kk, b, bm, bl)
            vb = pb_ref[rows, col(PB_V)]
            st = state[h]
            a = pl.dot(q1, k1, trans_b=True)
            a = jnp.where(causal, a, 0.0).astype(BF16)
            o = (jnp.dot(a, vb, preferred_element_type=F32)
                 + jnp.dot(q2.astype(BF16), st.T.astype(BF16), preferred_element_type=F32))
            state[h] = st * jnp.exp2(bl) + pl.dot(vb, k2.astype(BF16), trans_a=True)
            out_ref[rows, sl] = _gated_rmsnorm(o, gn[:, sl],
                                               pb_ref[rows, col(PB_SG)].astype(F32))
    for h in range(H_A):
        st_scr[h] = state[h]

    @pl.when(t == pl.num_programs(1) - 1)
    def _():
        for h in range(H_A):
            sfin_ref[0, h] = st_scr[h] if state_transposed_out else st_scr[h].T
        cfin_ref[0] = ubuf[CONV_TAIL:SUBLANES, :]


def _mix_seq(proj, row0, nseq, seq_len, tb, st0, tail0, gn, cw, *, mid, valid_rows=None,
             state_transposed_out=False):
    nt = seq_len // tb
    assert seq_len % tb == 0 and row0 % tb == 0 and tb % CHUNK == 0
    rb0 = row0 // tb
    rowblk = lambda b, t: (rb0 + b * nt + t, 0)
    const2 = lambda b, t: (0, 0)
    tril = np.tril(np.ones((CHUNK, CHUNK), np.float32))
    kern = functools.partial(_mix_seq_kernel, tb=tb, mid=mid, valid_rows=valid_rows,
                             state_transposed_out=state_transposed_out)
    return pl.pallas_call(
        kern,
        grid=(nseq, nt),
        in_specs=[pl.BlockSpec((tb, PB_W), rowblk), pl.BlockSpec((tb, PF_W), rowblk),
                  pl.BlockSpec((H_A, DV, DK), lambda b, t: (0, 0, 0)),
                  pl.BlockSpec((CONV_W - 1, D_B), const2),
                  pl.BlockSpec((1, D_A), const2),
                  pl.BlockSpec((CONV_W, D_B), const2),
                  pl.BlockSpec((CHUNK, CHUNK), const2)],
        out_specs=[pl.BlockSpec((tb, D_MODEL), lambda b, t: (b * nt + t, 0)),
                   pl.BlockSpec((1, H_A, DK, DV), lambda b, t: (b, 0, 0, 0)),
                   pl.BlockSpec((1, CONV_W - 1, D_B), lambda b, t: (b, 0, 0))],
        out_shape=[jax.ShapeDtypeStruct((nseq * seq_len, D_MODEL), BF16),
                   jax.ShapeDtypeStruct((nseq, H_A, DK, DV), F32),
                   jax.ShapeDtypeStruct((nseq, CONV_W - 1, D_B), F32)],
        scratch_shapes=[pltpu.VMEM((H_A, DV, DK), F32), pltpu.VMEM((SUBLANES + tb, D_B), F32)],
        compiler_params=pltpu.CompilerParams(
            dimension_semantics=("arbitrary", "arbitrary"), vmem_limit_bytes=VMEM_LIMIT),
        name="mix_seq",
    )(*proj, st0, tail0, gn.reshape(1, -1), cw, jnp.asarray(tril))


GROUP = 16
S_LEN = 4
S_MID = 2


def _mix_group_kernel(pb_ref, pf_ref, s_ref, cbuf_ref, gn_ref, cw_ref, lmat_ref, amask_ref,
                      out_ref, sfin_ref, cfin_ref, full_scr, y_scr):
    cw = cw_ref[...]
    nb = CONV_W - 1
    for s in range(GROUP):
        r0 = SUBLANES * s
        full_scr[r0:r0 + nb, :] = cbuf_ref[s]
        full_scr[r0 + nb:r0 + nb + S_LEN, :] = pf_ref[S_LEN * s:S_LEN * (s + 1), _seg(PF_U)]
    for s in range(GROUP):
        r0 = SUBLANES * s
        f0 = full_scr[r0:r0 + S_LEN, :]
        f1 = full_scr[r0 + 1:r0 + 1 + S_LEN, :]
        f2 = full_scr[r0 + 2:r0 + 2 + S_LEN, :]
        y_scr[S_LEN * s:S_LEN * (s + 1), :] = cw[0:1] * f0 + cw[1:2] * f1 + cw[2:3] * f2
        cfin_ref[s] = full_scr[r0 + S_LEN:r0 + S_LEN + nb, :]
    out_ref[:, D_A:] = (pb_ref[:, _seg(PB_GB)].astype(F32) * y_scr[...]).astype(BF16)

    b3 = jnp.dot(lmat_ref[...], _split3(pf_ref[:, _seg(PF_LF)]),
                 preferred_element_type=F32) * LOG2E
    causal = amask_ref[...] > 0.0
    gn = gn_ref[...]
    sub = BF16_ROWS
    per_sub = sub // S_LEN
    rid = jax.lax.broadcasted_iota(jnp.int32, (sub, DK), 0)
    own = [(rid >= S_LEN * j) & (rid < S_LEN * (j + 1)) for j in range(per_sub)]
    ones_blk = jnp.ones((sub, DV), BF16)
    zeros_blk = jnp.zeros((sub, DV), BF16)
    for h in range(H_A):
        sl = slice(h * DK, (h + 1) * DK)
        col = lambda seg_i, h=h: slice(seg_i * SEG + h * DK, seg_i * SEG + (h + 1) * DK)
        b = b3[0:CHUNK, sl]
        bm = b3[CHUNK:2 * CHUNK, sl]
        bl = b3[2 * CHUNK:3 * CHUNK, sl]
        q1, k1, q2, k2 = _head_tiles(pb_ref[:, col(PB_Q)].astype(F32),
                                     pb_ref[:, col(PB_K)].astype(F32), b, bm, bl)
        vb = pb_ref[:, col(PB_V)]
        a = pl.dot(q1, k1, trans_b=True)
        a = jnp.where(causal, a, 0.0).astype(BF16)
        o1 = jnp.dot(a, vb, preferred_element_type=F32)
        decay = jnp.exp2(bl)
        for blk in range(CHUNK // sub):
            rs = slice(blk * sub, (blk + 1) * sub)
            q2b, k2b, vbb = q2[rs], k2[rs], vb[rs]
            rhs = jnp.concatenate([jnp.concatenate([vbb, zeros_blk], axis=1),
                                   jnp.concatenate([zeros_blk, ones_blk], axis=1)], axis=0)
            acc = o1[rs]
            for j in range(per_sub):
                s = blk * per_sub + j
                st = s_ref[s, h]
                qm = jnp.where(own[j], q2b, 0.0).astype(BF16)
                acc = acc + jnp.dot(qm, st.astype(BF16), preferred_element_type=F32)
                drows = _decay_rows(decay[S_LEN * s:S_LEN * s + 1, :], rid)
                lhs = jnp.concatenate([jnp.where(own[j], k2b, 0.0), drows], axis=0).astype(BF16)
                ud = pl.dot(lhs, rhs, trans_a=True)
                sfin_ref[s, h] = ud[:, DV:] * st + ud[:, :DV]
            out_ref[rs, sl] = _gated_rmsnorm(acc, gn[:, sl],
                                             pb_ref[rs, col(PB_SG)].astype(F32))


def _group_mats():
    r = np.arange(CHUNK)
    seq, pos = r // S_LEN, r % S_LEN
    same = seq[:, None] == seq[None, :]
    cum = same & (pos[None, :] <= pos[:, None])
    midm = same & (pos[None, :] <= S_MID)
    lmat = np.concatenate([cum, midm, same], axis=0).astype(np.float32)
    lmat = np.concatenate([lmat, lmat, lmat], axis=1)
    return jnp.asarray(lmat, BF16), jnp.asarray(cum.astype(np.float32))


def _mix_group(proj, nseq, s0, cbuf, gn, cw):
    steps = nseq // GROUP
    const2 = lambda i: (0, 0)
    lmat, amask = _group_mats()
    return pl.pallas_call(
        _mix_group_kernel,
        grid=(steps,),
        in_specs=[pl.BlockSpec((CHUNK, PB_W), lambda i: (i, 0)),
                  pl.BlockSpec((CHUNK, PF_W), lambda i: (i, 0)),
                  pl.BlockSpec((GROUP, H_A, DK, DV), lambda i: (i, 0, 0, 0)),
                  pl.BlockSpec((GROUP, CONV_W - 1, D_B), lambda i: (i, 0, 0)),
                  pl.BlockSpec((1, D_A), const2),
                  pl.BlockSpec((CONV_W, D_B), const2),
                  pl.BlockSpec((3 * CHUNK, 3 * CHUNK), const2),
                  pl.BlockSpec((CHUNK, CHUNK), const2)],
        out_specs=[pl.BlockSpec((CHUNK, D_MODEL), lambda i: (i, 0)),
                   pl.BlockSpec((GROUP, H_A, DK, DV), lambda i: (i, 0, 0, 0)),
                   pl.BlockSpec((GROUP, CONV_W - 1, D_B), lambda i: (i, 0, 0))],
        out_shape=[jax.ShapeDtypeStruct((nseq * S_LEN, D_MODEL), BF16),
                   jax.ShapeDtypeStruct((nseq, H_A, DK, DV), F32),
                   jax.ShapeDtypeStruct((nseq, CONV_W - 1, D_B), F32)],
        scratch_shapes=[pltpu.VMEM((SUBLANES * GROUP, D_B), F32), pltpu.VMEM((CHUNK, D_B), F32)],
        compiler_params=pltpu.CompilerParams(
            dimension_semantics=("arbitrary",), vmem_limit_bytes=VMEM_LIMIT),
        name="mix_group",
    )(*proj, s0, cbuf, gn.reshape(1, -1), cw, lmat, amask)


def _outproj_kernel(x_ref, mix_ref, wo_ref, g0_ref, b0_ref, g1_ref, b1_ref, h_ref, *xs_scr):
    if xs_scr:
        _flatten_rows(x_ref, xs_scr[0])
        x_ref = xs_scr[0]
    xn = _layernorm(x_ref[...], g0_ref[...], b0_ref[...])
    m = jnp.dot(mix_ref[...], wo_ref[...], preferred_element_type=F32)
    h_ref[...] = _layernorm(ALPHA * xn + m, g1_ref[...], b1_ref[...])


def _outproj(x, mix, w_o, g0, b0, g1, b1, tm):
    rows = mix.shape[0]
    tm = min(tm, rows)
    assert rows % tm == 0 and (x.ndim == 2 or rows == tm)
    row = lambda a: a.reshape(1, -1)
    const = lambda i: (0, 0)
    vec = pl.BlockSpec((1, D_MODEL), const)
    x_spec = (pl.BlockSpec((tm, D_MODEL), lambda i: (i, 0)) if x.ndim == 2 else
              pl.BlockSpec(x.shape, lambda i: (0, 0, 0), pipeline_mode=pl.Buffered(1)))
    return pl.pallas_call(
        _outproj_kernel,
        grid=(rows // tm,),
        in_specs=[x_spec,
                  pl.BlockSpec((tm, D_MODEL), lambda i: (i, 0)),
                  pl.BlockSpec((D_MODEL, D_MODEL), const),
                  vec, vec, vec, vec],
        out_specs=pl.BlockSpec((tm, D_MODEL), lambda i: (i, 0)),
        out_shape=jax.ShapeDtypeStruct((rows, D_MODEL), F32),
        scratch_shapes=[] if x.ndim == 2 else [pltpu.VMEM((rows, D_MODEL), F32)],
        compiler_params=pltpu.CompilerParams(
            dimension_semantics=("arbitrary",), vmem_limit_bytes=VMEM_LIMIT),
        name="outproj",
    )(x, mix, w_o, row(g0), row(b0), row(g1), row(b1))


def _ffn_kernel(h_ref, wg_ref, wu_ref, wd_ref, g2_ref, b2_ref, o_ref, hb_ref, *acc_scr):
    acc_ref = acc_scr[0] if acc_scr else o_ref
    f = pl.program_id(1)

    @pl.when(f == 0)
    def _():
        h = h_ref[...]
        hb_ref[...] = h.astype(BF16)
        acc_ref[...] = ALPHA * h

    hb = hb_ref[...]
    g = jnp.dot(hb, wg_ref[...], preferred_element_type=F32)
    u = jnp.dot(hb, wu_ref[...], preferred_element_type=F32)
    a = (_silu(g) * u).astype(BF16)
    acc_ref[...] += jnp.dot(a, wd_ref[...], preferred_element_type=F32)

    @pl.when(f == pl.num_programs(1) - 1)
    def _():
        acc_ref[...] = _layernorm(acc_ref[...], g2_ref[...], b2_ref[...])
        if acc_scr:
            _unflatten_rows(acc_ref, o_ref)


def _ffn(h, w_gate, w_up, w_down, g2, b2, tm, tf, seq_shape=None):
    rows = h.shape[0]
    assert rows % tm == 0 and D_FF % tf == 0
    row = lambda a: a.reshape(1, -1)
    vec = pl.BlockSpec((1, D_MODEL), lambda i, f: (0, 0))
    if seq_shape is None:
        out_spec = pl.BlockSpec((tm, D_MODEL), lambda i, f: (i, 0))
        out_shape = jax.ShapeDtypeStruct((rows, D_MODEL), F32)
        acc = []
    else:
        assert rows == tm == seq_shape[0] * seq_shape[1]
        out_spec = pl.BlockSpec((*seq_shape, D_MODEL), lambda i, f: (0, 0, 0))
        out_shape = jax.ShapeDtypeStruct((*seq_shape, D_MODEL), F32)
        acc = [pltpu.VMEM((tm, D_MODEL), F32)]
    return pl.pallas_call(
        _ffn_kernel,
        grid=(rows // tm, D_FF // tf),
        in_specs=[pl.BlockSpec((tm, D_MODEL), lambda i, f: (i, 0)),
                  pl.BlockSpec((D_MODEL, tf), lambda i, f: (0, f)),
                  pl.BlockSpec((D_MODEL, tf), lambda i, f: (0, f)),
                  pl.BlockSpec((tf, D_MODEL), lambda i, f: (f, 0)),
                  vec, vec],
        out_specs=out_spec,
        out_shape=out_shape,
        scratch_shapes=[pltpu.VMEM((tm, D_MODEL), BF16)] + acc,
        compiler_params=pltpu.CompilerParams(
            dimension_semantics=("arbitrary", "arbitrary"), vmem_limit_bytes=VMEM_LIMIT),
        name="ffn",
    )(h, w_gate, w_up, w_down, row(g2), row(b2))


def kernel(x_prompt, x_sample, state_hgrn, state_conv, meta_tokens, ln0_g, ln0_b, w_in, b_f, lb_param, gnorm_g, conv_w, w_o, ln1_g, ln1_b, w_gate, w_up, w_down, ln2_g, ln2_b):
    bp, seq, _ = x_prompt.shape
    bs, dseq, _ = x_sample.shape
    assert dseq == S_LEN and seq % CHUNK == 0 and bs % GROUP == 0

    g0, b0 = ln0_g.astype(F32), ln0_b.astype(F32)

    xp = x_prompt.reshape(bp * seq, D_MODEL)
    proj_s, w_in_b, lb = _inproj_cast(x_sample, meta_tokens.astype(F32), g0, b0, w_in[0],
                                      b_f[0], lb_param)
    proj_p, (w_o_b, wg_b, wu_b, wd_b) = _inproj(
        xp, g0, b0, w_in_b, b_f[0], lb, tm=INPROJ_TM,
        cast=(w_o[0], w_gate[0], w_up[0], w_down[0]))

    zero_st = jnp.zeros((H_A, DV, DK), F32)
    zero_tail = jnp.zeros((CONV_W - 1, D_B), F32)
    _, st_meta, tail_meta = _mix_seq(
        proj_s, bs * dseq, 1, CHUNK, CHUNK, zero_st, zero_tail, gnorm_g[0], conv_w[0],
        mid=N_META // 2, valid_rows=N_META, state_transposed_out=True)

    mix_p, hgrn_p, conv_p = _mix_seq(
        proj_p, 0, bp, seq, MIX_TB, st_meta[0], tail_meta[0], gnorm_g[0], conv_w[0],
        mid=CHUNK // 2)
    mix_s, hgrn_s, conv_s = _mix_group(
        proj_s, bs, state_hgrn[0], state_conv[0], gnorm_g[0], conv_w[0])

    h_p = _outproj(xp, mix_p, w_o_b, g0, b0, ln1_g[0], ln1_b[0], tm=OUTPROJ_TM)
    h_s = _outproj(x_sample, mix_s, w_o_b, g0, b0, ln1_g[0], ln1_b[0], tm=OUTPROJ_TM)

    y_p = _ffn(h_p, wg_b, wu_b, wd_b, ln2_g[0], ln2_b[0], tm=FFN_TM, tf=FFN_TF)
    y_s = _ffn(h_s, wg_b, wu_b, wd_b, ln2_g[0], ln2_b[0], tm=FFN_TM_SMALL, tf=FFN_TF,
               seq_shape=(bs, dseq))

    return (y_p.reshape(bp, seq, D_MODEL), y_s,
            hgrn_p[None], conv_p[None], hgrn_s[None], conv_s[None])
```

```python
import functools

import numpy as np
import jax
import jax.numpy as jnp
from jax.experimental import pallas as pl
from jax.experimental.pallas import tpu as pltpu

F32 = jnp.float32
BF16 = jnp.bfloat16

D_MODEL = 2048
D_A = 1024
D_B = 1024
DK = 128
DV = 128
H_A = 8
SEG = 1024
N_SEG = 7
W_SEG_STRIDE = SEG + 128
PB_Q, PB_K, PB_V, PB_SG, PB_GB = range(5)
PF_LF, PF_U = range(2)
PB_W, PF_W = 5 * SEG, 2 * SEG
N_META = 16
CHUNK = 64
D_FF = 5632
CONV_W = 3
ALPHA = 2.0 ** 0.25
LN_EPS = 1e-5
RMS_EPS = 1e-6

V7X_VMEM_BYTES = 64 * 1024 * 1024
SUBLANES = 8
BF16_ROWS = 16
VMEM_LIMIT = V7X_VMEM_BYTES - 2 * 1024 * 1024

INPROJ_TM = 256
MIX_TB = 512
OUTPROJ_TM = 512
FFN_TM = 1024
FFN_TM_SMALL = 512
FFN_TF = 512

CONV_TAIL = SUBLANES - (CONV_W - 1)


def _seg(i):
    return slice(i * SEG, (i + 1) * SEG)


def _layernorm(x, g, b):
    mu = jnp.mean(x, axis=-1, keepdims=True)
    xc = x - mu
    var = jnp.mean(xc * xc, axis=-1, keepdims=True)
    return xc * jax.lax.rsqrt(var + LN_EPS) * g + b


def _silu(x):
    return x * (1.0 / (1.0 + jnp.exp(-x)))


def _flatten_rows(x3_ref, dst_ref):
    n, slen, _ = x3_ref.shape
    for b in range(n):
        dst_ref[b * slen:(b + 1) * slen, :] = x3_ref[b]


def _unflatten_rows(src_ref, y3_ref):
    n, slen, _ = y3_ref.shape
    for b in range(n):
        y3_ref[b] = src_ref[b * slen:(b + 1) * slen, :]


INPROJ_SUB = 256


def _inproj_kernel(x_ref, g0_ref, b0_ref, w_ref, bf_ref, lb_ref, *refs, n_cast):
    cast_in, refs = refs[:n_cast], refs[n_cast:]
    (pb_ref, pf_ref), cast_out = refs[:2], refs[2:]
    for src, dst in zip(cast_in, cast_out, strict=True):
        dst[...] = src[...].astype(BF16)

    for r in range(x_ref.shape[0] // INPROJ_SUB):
        rs = slice(r * INPROJ_SUB, (r + 1) * INPROJ_SUB)
        xn = _layernorm(x_ref[rs, :], g0_ref[...], b0_ref[...]).astype(BF16)

        def seg(j, xn=xn):
            return jnp.dot(xn, w_ref[:, j * W_SEG_STRIDE:j * W_SEG_STRIDE + SEG],
                           preferred_element_type=F32)

        pf_ref[rs, _seg(PF_U)] = seg(5) * seg(6)
        pb_ref[rs, _seg(PB_Q)] = _silu(seg(0)).astype(BF16)

        lf, kk = _forget_gate(seg(1) + bf_ref[...], lb_ref[...])
        pf_ref[rs, _seg(PF_LF)] = lf
        pb_ref[rs, _seg(PB_K)] = kk.astype(BF16)

        pb_ref[rs, _seg(PB_SG)] = _silu(seg(3)).astype(BF16)
        pb_ref[rs, _seg(PB_GB)] = seg(4).astype(BF16)
        pb_ref[rs, _seg(PB_V)] = seg(2).astype(BF16)


def _inproj(x, g0, b0, w_in, b_f, lb, tm, cast=()):
    rows = x.shape[0]
    assert rows % tm == 0 and tm % INPROJ_SUB == 0
    steps = rows // tm
    row = lambda a: a.reshape(1, -1)
    const = lambda i: (0, 0)
    cast_specs = []
    for w in cast:
        assert w.shape[0] % (BF16_ROWS * steps) == 0
        cast_specs.append(pl.BlockSpec((w.shape[0] // steps, w.shape[1]), lambda i: (i, 0)))
    res = pl.pallas_call(
        functools.partial(_inproj_kernel, n_cast=len(cast)),
        grid=(steps,),
        in_specs=[
            pl.BlockSpec((tm, D_MODEL), lambda i: (i, 0)),
            pl.BlockSpec((1, D_MODEL), const),
            pl.BlockSpec((1, D_MODEL), const),
            pl.BlockSpec((D_MODEL, N_SEG * W_SEG_STRIDE), const, pipeline_mode=pl.Buffered(1)),
            pl.BlockSpec((1, SEG), const),
            pl.BlockSpec((1, SEG), const),
        ] + cast_specs,
        out_specs=[pl.BlockSpec((tm, PB_W), lambda i: (i, 0)),
                   pl.BlockSpec((tm, PF_W), lambda i: (i, 0))] + cast_specs,
        out_shape=[jax.ShapeDtypeStruct((rows, PB_W), BF16),
                   jax.ShapeDtypeStruct((rows, PF_W), F32)]
        + [jax.ShapeDtypeStruct(w.shape, BF16) for w in cast],
        compiler_params=pltpu.CompilerParams(
            dimension_semantics=("arbitrary",), vmem_limit_bytes=VMEM_LIMIT),
        name="inproj",
    )(x, row(g0), row(b0), w_in, row(b_f), row(lb), *cast)
    return res[:2], res[2:]


def _forget_gate(z, lb):
    e = jnp.exp(-jnp.abs(z))
    r = 1.0 / (1.0 + e)
    er = e * r
    pos = z >= 0.0
    return jnp.log(lb + (1.0 - lb) * jnp.where(pos, r, er)), (1.0 - lb) * jnp.where(pos, er, r)


def _inproj_cast_kernel(x_ref, meta_ref, g0_ref, b0_ref, w_ref, bf_ref, lbp_ref,
                        pb_ref, pf_ref, wb_ref, lb_ref, xn_scr, gc_scr, xs_scr):
    j = pl.program_id(0)

    @pl.when(j == 0)
    def _():
        n_x, n_m = xs_scr.shape[0], meta_ref.shape[0]
        _flatten_rows(x_ref, xs_scr)
        xn_scr[0:n_x, :] = _layernorm(xs_scr[...], g0_ref[...], b0_ref[...]).astype(BF16)
        xn_scr[n_x:n_x + n_m, :] = _layernorm(meta_ref[...], g0_ref[...],
                                              b0_ref[...]).astype(BF16)
        xn_scr[n_x + n_m:, :] = jnp.zeros((xn_scr.shape[0] - n_x - n_m, D_MODEL), BF16)
        p = lbp_ref[...]
        e = jnp.exp(p - jnp.max(p, axis=0, keepdims=True))
        lb_ref[...] = e[0:1, :] / jnp.sum(e, axis=0, keepdims=True)

    wb = w_ref[...].astype(BF16)
    wb_ref[:, 0:SEG] = wb
    wb_ref[:, SEG:] = jnp.zeros((D_MODEL, W_SEG_STRIDE - SEG), BF16)
    acc = jnp.dot(xn_scr[...], wb, preferred_element_type=F32)

    @pl.when(j == 0)
    def _():
        pb_ref[:, _seg(PB_Q)] = _silu(acc).astype(BF16)

    @pl.when(j == 1)
    def _():
        lf, kk = _forget_gate(acc + bf_ref[...], lb_ref[...])
        pf_ref[:, _seg(PF_LF)] = lf
        pb_ref[:, _seg(PB_K)] = kk.astype(BF16)

    @pl.when(j == 2)
    def _():
        pb_ref[:, _seg(PB_V)] = acc.astype(BF16)

    @pl.when(j == 3)
    def _():
        pb_ref[:, _seg(PB_SG)] = _silu(acc).astype(BF16)

    @pl.when(j == 4)
    def _():
        pb_ref[:, _seg(PB_GB)] = acc.astype(BF16)

    @pl.when(j == 5)
    def _():
        gc_scr[...] = acc

    @pl.when(j == 6)
    def _():
        pf_ref[:, _seg(PF_U)] = gc_scr[...] * acc


def _inproj_cast(x, meta, g0, b0, w_in, b_f, lb_param):
    n_x = x.shape[0] * x.shape[1]
    assert n_x % BF16_ROWS == 0 and meta.shape[0] % BF16_ROWS == 0
    rows = n_x + CHUNK
    row = lambda a: a.reshape(1, -1)
    const = lambda j: (0, 0)
    wseg = lambda: pl.BlockSpec((D_MODEL, SEG), lambda j: (0, j))
    res = pl.pallas_call(
        _inproj_cast_kernel,
        grid=(N_SEG,),
        in_specs=[pl.BlockSpec(x.shape, lambda j: (0, 0, 0), pipeline_mode=pl.Buffered(1)),
                  pl.BlockSpec(meta.shape, const),
                  pl.BlockSpec((1, D_MODEL), const),
                  pl.BlockSpec((1, D_MODEL), const),
                  wseg(),
                  pl.BlockSpec((1, SEG), const),
                  pl.BlockSpec(lb_param.shape, const)],
        out_specs=[pl.BlockSpec((rows, PB_W), const), pl.BlockSpec((rows, PF_W), const),
                   pl.BlockSpec((D_MODEL, W_SEG_STRIDE), lambda j: (0, j)),
                   pl.BlockSpec((1, SEG), const)],
        out_shape=[jax.ShapeDtypeStruct((rows, PB_W), BF16),
                   jax.ShapeDtypeStruct((rows, PF_W), F32),
                   jax.ShapeDtypeStruct((D_MODEL, N_SEG * W_SEG_STRIDE), BF16),
                   jax.ShapeDtypeStruct((1, SEG), F32)],
        scratch_shapes=[pltpu.VMEM((rows, D_MODEL), BF16), pltpu.VMEM((rows, SEG), F32),
                        pltpu.VMEM((n_x, D_MODEL), F32)],
        compiler_params=pltpu.CompilerParams(
            dimension_semantics=("arbitrary",), vmem_limit_bytes=VMEM_LIMIT),
        name="inproj_cast",
    )(x, meta, row(g0), row(b0), w_in, row(b_f), lb_param.astype(F32))
    return res[:2], res[2], res[3]


def _split3(x):
    hi = x.astype(BF16)
    r1 = x - hi.astype(F32)
    mid = r1.astype(BF16)
    lo = (r1 - mid.astype(F32)).astype(BF16)
    return jnp.concatenate([hi, mid, lo], axis=0)


def _cumsum_rows(x):
    rows, lanes = x.shape
    n = rows // SUBLANES
    x3 = x.reshape(n, SUBLANES, lanes)
    pos = jax.lax.broadcasted_iota(jnp.int32, x3.shape, 1)
    s = 1
    while s < SUBLANES:
        x3 = x3 + jnp.where(pos >= s, pltpu.roll(x3, s, 1), 0.0)
        s *= 2
    carry = jnp.zeros((1, lanes), F32)
    out = []
    for g in range(n):
        out.append(x3[g] + carry)
        carry = carry + x3[g, SUBLANES - 1:SUBLANES, :]
    return jnp.concatenate(out, axis=0)


LOG2E = 1.4426950408889634


def _head_tiles(qt, kk, b, bm, bl):
    q1 = (qt * jnp.exp2(b - bm)).astype(BF16)
    k1 = (kk * jnp.exp2(bm - b)).astype(BF16)
    q2 = qt * jnp.exp2(b)
    k2 = kk * jnp.exp2(bl - b)
    return q1, k1, q2, k2


DECAY_ROWS = BF16_ROWS


def _decay_rows(d, rid):
    d_hi = d.astype(BF16).astype(F32)
    d_r = d - d_hi
    d_mid = d_r.astype(BF16).astype(F32)
    return jnp.where(rid == 0, d_hi, jnp.where(rid == 1, d_mid,
                     jnp.where(rid == 2, d_r - d_mid, 0.0)))


def _gated_rmsnorm(o, gn, sg):
    ms = jnp.mean(o * o, axis=-1, keepdims=True)
    return (o * jax.lax.rsqrt(ms + RMS_EPS) * gn * sg).astype(BF16)


def _mix_seq_kernel(pb_ref, pf_ref, st0_ref, tail0_ref, gn_ref, cw_ref, amask_ref,
                    out_ref, sfin_ref, cfin_ref, st_scr, ubuf,
                    *, tb, mid, valid_rows, state_transposed_out):
    t = pl.program_id(1)

    @pl.when(t == 0)
    def _():
        st_scr[...] = st0_ref[...]
        ubuf[CONV_TAIL:SUBLANES, :] = tail0_ref[...]

    u = pf_ref[:, _seg(PF_U)]
    ubuf[SUBLANES:SUBLANES + tb, :] = u
    cw = cw_ref[...]
    conv = (cw[0:1] * ubuf[CONV_TAIL:CONV_TAIL + tb, :]
            + cw[1:2] * ubuf[CONV_TAIL + 1:CONV_TAIL + 1 + tb, :] + cw[2:3] * u)
    out_ref[:, D_A:] = (pb_ref[:, _seg(PB_GB)].astype(F32) * conv).astype(BF16)
    last = valid_rows if valid_rows is not None else tb
    ubuf[CONV_TAIL:SUBLANES, :] = ubuf[CONV_TAIL + last:SUBLANES + last, :]

    causal = amask_ref[...] > 0.0
    gn = gn_ref[...]
    if valid_rows is not None:
        rowmask = jax.lax.broadcasted_iota(jnp.int32, (CHUNK, SEG), 0) < valid_rows
        rowmask_h = jax.lax.broadcasted_iota(jnp.int32, (CHUNK, DK), 0) < valid_rows
    state = [st_scr[h] for h in range(H_A)]
    for ci in range(tb // CHUNK):
        rows = slice(ci * CHUNK, (ci + 1) * CHUNK)
        lf = pf_ref[rows, _seg(PF_LF)]
        if valid_rows is not None:
            lf = jnp.where(rowmask, lf, 0.0)
        b_all = _cumsum_rows(lf) * LOG2E
        for h in range(H_A):
            sl = slice(h * DK, (h + 1) * DK)
            col = lambda seg_i, h=h: slice(seg_i * SEG + h * DK, seg_i * SEG + (h + 1) * DK)
            b = b_all[:, sl]
            bm = b[mid:mid + 1, :]
            bl = b[CHUNK - 1:CHUNK, :]
            kk = pb_ref[rows, col(PB_K)].astype(F32)
            if valid_rows is not None:
                kk = jnp.where(rowmask_h, kk, 0.0)
            q1, k1, q2, k2 = _head_tiles(pb_ref[rows, col(PB_Q)].astype(F32), kk, b, bm, bl)
            vb = pb_ref[rows, col(PB_V)]
            st = state[h]
            a = pl.dot(q1, k1, trans_b=True)
            a = jnp.where(causal, a, 0.0).astype(BF16)
            o = (jnp.dot(a, vb, preferred_element_type=F32)
                 + jnp.dot(q2.astype(BF16), st.T.astype(BF16), preferred_element_type=F32))
            state[h] = st * jnp.exp2(bl) + pl.dot(vb, k2.astype(BF16), trans_a=True)
            out_ref[rows, sl] = _gated_rmsnorm(o, gn[:, sl],
                                               pb_ref[rows, col(PB_SG)].astype(F32))
    for h in range(H_A):
        st_scr[h] = state[h]

    @pl.when(t == pl.num_programs(1) - 1)
    def _():
        for h in range(H_A):
            sfin_ref[0, h] = st_scr[h] if state_transposed_out else st_scr[h].T
        cfin_ref[0] = ubuf[CONV_TAIL:SUBLANES, :]


def _mix_seq(proj, row0, nseq, seq_len, tb, st0, tail0, gn, cw, *, mid, valid_rows=None,
             state_transposed_out=False):
    nt = seq_len // tb
    assert seq_len % tb == 0 and row0 % tb == 0 and tb % CHUNK == 0
    rb0 = row0 // tb
    rowblk = lambda b, t: (rb0 + b * nt + t, 0)
    const2 = lambda b, t: (0, 0)
    tril = np.tril(np.ones((CHUNK, CHUNK), np.float32))
    kern = functools.partial(_mix_seq_kernel, tb=tb, mid=mid, valid_rows=valid_rows,
                             state_transposed_out=state_transposed_out)
    return pl.pallas_call(
        kern,
        grid=(nseq, nt),
        in_specs=[pl.BlockSpec((tb, PB_W), rowblk), pl.BlockSpec((tb, PF_W), rowblk),
                  pl.BlockSpec((H_A, DV, DK), lambda b, t: (0, 0, 0)),
                  pl.BlockSpec((CONV_W - 1, D_B), const2),
                  pl.BlockSpec((1, D_A), const2),
                  pl.BlockSpec((CONV_W, D_B), const2),
                  pl.BlockSpec((CHUNK, CHUNK), const2)],
        out_specs=[pl.BlockSpec((tb, D_MODEL), lambda b, t: (b * nt + t, 0)),
                   pl.BlockSpec((1, H_A, DK, DV), lambda b, t: (b, 0, 0, 0)),
                   pl.BlockSpec((1, CONV_W - 1, D_B), lambda b, t: (b, 0, 0))],
        out_shape=[jax.ShapeDtypeStruct((nseq * seq_len, D_MODEL), BF16),
                   jax.ShapeDtypeStruct((nseq, H_A, DK, DV), F32),
                   jax.ShapeDtypeStruct((nseq, CONV_W - 1, D_B), F32)],
        scratch_shapes=[pltpu.VMEM((H_A, DV, DK), F32), pltpu.VMEM((SUBLANES + tb, D_B), F32)],
        compiler_params=pltpu.CompilerParams(
            dimension_semantics=("arbitrary", "arbitrary"), vmem_limit_bytes=VMEM_LIMIT),
        name="mix_seq",
    )(*proj, st0, tail0, gn.reshape(1, -1), cw, jnp.asarray(tril))


GROUP = 16
S_LEN = 4
S_MID = 2


def _mix_group_kernel(pb_ref, pf_ref, s_ref, cbuf_ref, gn_ref, cw_ref, lmat_ref, amask_ref,
                      out_ref, sfin_ref, cfin_ref, full_scr, y_scr):
    cw = cw_ref[...]
    nb = CONV_W - 1
    for s in range(GROUP):
        r0 = SUBLANES * s
        full_scr[r0:r0 + nb, :] = cbuf_ref[s]
        full_scr[r0 + nb:r0 + nb + S_LEN, :] = pf_ref[S_LEN * s:S_LEN * (s + 1), _seg(PF_U)]
    for s in range(GROUP):
        r0 = SUBLANES * s
        f0 = full_scr[r0:r0 + S_LEN, :]
        f1 = full_scr[r0 + 1:r0 + 1 + S_LEN, :]
        f2 = full_scr[r0 + 2:r0 + 2 + S_LEN, :]
        y_scr[S_LEN * s:S_LEN * (s + 1), :] = cw[0:1] * f0 + cw[1:2] * f1 + cw[2:3] * f2
        cfin_ref[s] = full_scr[r0 + S_LEN:r0 + S_LEN + nb, :]
    out_ref[:, D_A:] = (pb_ref[:, _seg(PB_GB)].astype(F32) * y_scr[...]).astype(BF16)

    b3 = jnp.dot(lmat_ref[...], _split3(pf_ref[:, _seg(PF_LF)]),
                 preferred_element_type=F32) * LOG2E
    causal = amask_ref[...] > 0.0
    gn = gn_ref[...]
    sub = BF16_ROWS
    per_sub = sub // S_LEN
    rid = jax.lax.broadcasted_iota(jnp.int32, (sub, DK), 0)
    own = [(rid >= S_LEN * j) & (rid < S_LEN * (j + 1)) for j in range(per_sub)]
    ones_blk = jnp.ones((sub, DV), BF16)
    zeros_blk = jnp.zeros((sub, DV), BF16)
    for h in range(H_A):
        sl = slice(h * DK, (h + 1) * DK)
        col = lambda seg_i, h=h: slice(seg_i * SEG + h * DK, seg_i * SEG + (h + 1) * DK)
        b = b3[0:CHUNK, sl]
        bm = b3[CHUNK:2 * CHUNK, sl]
        bl = b3[2 * CHUNK:3 * CHUNK, sl]
        q1, k1, q2, k2 = _head_tiles(pb_ref[:, col(PB_Q)].astype(F32),
                                     pb_ref[:, col(PB_K)].astype(F32), b, bm, bl)
        vb = pb_ref[:, col(PB_V)]
        a = pl.dot(q1, k1, trans_b=True)
        a = jnp.where(causal, a, 0.0).astype(BF16)
        o1 = jnp.dot(a, vb, preferred_element_type=F32)
        decay = jnp.exp2(bl)
        for blk in range(CHUNK // sub):
            rs = slice(blk * sub, (blk + 1) * sub)
            q2b, k2b, vbb = q2[rs], k2[rs], vb[rs]
            rhs = jnp.concatenate([jnp.concatenate([vbb, zeros_blk], axis=1),
                                   jnp.concatenate([zeros_blk, ones_blk], axis=1)], axis=0)
            acc = o1[rs]
            for j in range(per_sub):
                s = blk * per_sub + j
                st = s_ref[s, h]
                qm = jnp.where(own[j], q2b, 0.0).astype(BF16)
                acc = acc + jnp.dot(qm, st.astype(BF16), preferred_element_type=F32)
                drows = _decay_rows(decay[S_LEN * s:S_LEN * s + 1, :], rid)
                lhs = jnp.concatenate([jnp.where(own[j], k2b, 0.0), drows], axis=0).astype(BF16)
                ud = pl.dot(lhs, rhs, trans_a=True)
                sfin_ref[s, h] = ud[:, DV:] * st + ud[:, :DV]
            out_ref[rs, sl] = _gated_rmsnorm(acc, gn[:, sl],
                                             pb_ref[rs, col(PB_SG)].astype(F32))


def _group_mats():
    r = np.arange(CHUNK)
    seq, pos = r // S_LEN, r % S_LEN
    same = seq[:, None] == seq[None, :]
    cum = same & (pos[None, :] <= pos[:, None])
    midm = same & (pos[None, :] <= S_MID)
    lmat = np.concatenate([cum, midm, same], axis=0).astype(np.float32)
    lmat = np.concatenate([lmat, lmat, lmat], axis=1)
    return jnp.asarray(lmat, BF16), jnp.asarray(cum.astype(np.float32))


def _mix_group(proj, nseq, s0, cbuf, gn, cw):
    steps = nseq // GROUP
    const2 = lambda i: (0, 0)
    lmat, amask = _group_mats()
    return pl.pallas_call(
        _mix_group_kernel,
        grid=(steps,),
        in_specs=[pl.BlockSpec((CHUNK, PB_W), lambda i: (i, 0)),
                  pl.BlockSpec((CHUNK, PF_W), lambda i: (i, 0)),
                  pl.BlockSpec((GROUP, H_A, DK, DV), lambda i: (i, 0, 0, 0)),
                  pl.BlockSpec((GROUP, CONV_W - 1, D_B), lambda i: (i, 0, 0)),
                  pl.BlockSpec((1, D_A), const2),
                  pl.BlockSpec((CONV_W, D_B), const2),
                  pl.BlockSpec((3 * CHUNK, 3 * CHUNK), const2),
                  pl.BlockSpec((CHUNK, CHUNK), const2)],
        out_specs=[pl.BlockSpec((CHUNK, D_MODEL), lambda i: (i, 0)),
                   pl.BlockSpec((GROUP, H_A, DK, DV), lambda i: (i, 0, 0, 0)),
                   pl.BlockSpec((GROUP, CONV_W - 1, D_B), lambda i: (i, 0, 0))],
        out_shape=[jax.ShapeDtypeStruct((nseq * S_LEN, D_MODEL), BF16),
                   jax.ShapeDtypeStruct((nseq, H_A, DK, DV), F32),
                   jax.ShapeDtypeStruct((nseq, CONV_W - 1, D_B), F32)],
        scratch_shapes=[pltpu.VMEM((SUBLANES * GROUP, D_B), F32), pltpu.VMEM((CHUNK, D_B), F32)],
        compiler_params=pltpu.CompilerParams(
            dimension_semantics=("arbitrary",), vmem_limit_bytes=VMEM_LIMIT),
        name="mix_group",
    )(*proj, s0, cbuf, gn.reshape(1, -1), cw, lmat, amask)


def _outproj_kernel(x_ref, mix_ref, wo_ref, g0_ref, b0_ref, g1_ref, b1_ref, h_ref, *xs_scr):
    if xs_scr:
        _flatten_rows(x_ref, xs_scr[0])
        x_ref = xs_scr[0]
    xn = _layernorm(x_ref[...], g0_ref[...], b0_ref[...])
    m = jnp.dot(mix_ref[...], wo_ref[...], preferred_element_type=F32)
    h_ref[...] = _layernorm(ALPHA * xn + m, g1_ref[...], b1_ref[...])


def _outproj(x, mix, w_o, g0, b0, g1, b1, tm):
    rows = mix.shape[0]
    tm = min(tm, rows)
    assert rows % tm == 0 and (x.ndim == 2 or rows == tm)
    row = lambda a: a.reshape(1, -1)
    const = lambda i: (0, 0)
    vec = pl.BlockSpec((1, D_MODEL), const)
    x_spec = (pl.BlockSpec((tm, D_MODEL), lambda i: (i, 0)) if x.ndim == 2 else
              pl.BlockSpec(x.shape, lambda i: (0, 0, 0), pipeline_mode=pl.Buffered(1)))
    return pl.pallas_call(
        _outproj_kernel,
        grid=(rows // tm,),
        in_specs=[x_spec,
                  pl.BlockSpec((tm, D_MODEL), lambda i: (i, 0)),
                  pl.BlockSpec((D_MODEL, D_MODEL), const),
                  vec, vec, vec, vec],
        out_specs=pl.BlockSpec((tm, D_MODEL), lambda i: (i, 0)),
        out_shape=jax.ShapeDtypeStruct((rows, D_MODEL), F32),
        scratch_shapes=[] if x.ndim == 2 else [pltpu.VMEM((rows, D_MODEL), F32)],
        compiler_params=pltpu.CompilerParams(
            dimension_semantics=("arbitrary",), vmem_limit_bytes=VMEM_LIMIT),
        name="outproj",
    )(x, mix, w_o, row(g0), row(b0), row(g1), row(b1))


def _ffn_kernel(h_ref, wg_ref, wu_ref, wd_ref, g2_ref, b2_ref, o_ref, hb_ref, *acc_scr):
    acc_ref = acc_scr[0] if acc_scr else o_ref
    f = pl.program_id(1)

    @pl.when(f == 0)
    def _():
        h = h_ref[...]
        hb_ref[...] = h.astype(BF16)
        acc_ref[...] = ALPHA * h

    hb = hb_ref[...]
    g = jnp.dot(hb, wg_ref[...], preferred_element_type=F32)
    u = jnp.dot(hb, wu_ref[...], preferred_element_type=F32)
    a = (_silu(g) * u).astype(BF16)
    acc_ref[...] += jnp.dot(a, wd_ref[...], preferred_element_type=F32)

    @pl.when(f == pl.num_programs(1) - 1)
    def _():
        acc_ref[...] = _layernorm(acc_ref[...], g2_ref[...], b2_ref[...])
        if acc_scr:
            _unflatten_rows(acc_ref, o_ref)


def _ffn(h, w_gate, w_up, w_down, g2, b2, tm, tf, seq_shape=None):
    rows = h.shape[0]
    assert rows % tm == 0 and D_FF % tf == 0
    row = lambda a: a.reshape(1, -1)
    vec = pl.BlockSpec((1, D_MODEL), lambda i, f: (0, 0))
    if seq_shape is None:
        out_spec = pl.BlockSpec((tm, D_MODEL), lambda i, f: (i, 0))
        out_shape = jax.ShapeDtypeStruct((rows, D_MODEL), F32)
        acc = []
    else:
        assert rows == tm == seq_shape[0] * seq_shape[1]
        out_spec = pl.BlockSpec((*seq_shape, D_MODEL), lambda i, f: (0, 0, 0))
        out_shape = jax.ShapeDtypeStruct((*seq_shape, D_MODEL), F32)
        acc = [pltpu.VMEM((tm, D_MODEL), F32)]
    return pl.pallas_call(
        _ffn_kernel,
        grid=(rows // tm, D_FF // tf),
        in_specs=[pl.BlockSpec((tm, D_MODEL), lambda i, f: (i, 0)),
                  pl.BlockSpec((D_MODEL, tf), lambda i, f: (0, f)),
                  pl.BlockSpec((D_MODEL, tf), lambda i, f: (0, f)),
                  pl.BlockSpec((tf, D_MODEL), lambda i, f: (f, 0)),
                  vec, vec],
        out_specs=out_spec,
        out_shape=out_shape,
        scratch_shapes=[pltpu.VMEM((tm, D_MODEL), BF16)] + acc,
        compiler_params=pltpu.CompilerParams(
            dimension_semantics=("arbitrary", "arbitrary"), vmem_limit_bytes=VMEM_LIMIT),
        name="ffn",
    )(h, w_gate, w_up, w_down, row(g2), row(b2))


def kernel(x_prompt, x_sample, state_hgrn, state_conv, meta_tokens, ln0_g, ln0_b, w_in, b_f, lb_param, gnorm_g, conv_w, w_o, ln1_g, ln1_b, w_gate, w_up, w_down, ln2_g, ln2_b):
    bp, seq, _ = x_prompt.shape
    bs, dseq, _ = x_sample.shape
    assert dseq == S_LEN and seq % CHUNK == 0 and bs % GROUP == 0

    g0, b0 = ln0_g.astype(F32), ln0_b.astype(F32)

    xp = x_prompt.reshape(bp * seq, D_MODEL)
    proj_s, w_in_b, lb = _inproj_cast(x_sample, meta_tokens.astype(F32), g0, b0, w_in[0],
                                      b_f[0], lb_param)
    proj_p, (w_o_b, wg_b, wu_b, wd_b) = _inproj(
        xp, g0, b0, w_in_b, b_f[0], lb, tm=INPROJ_TM,
        cast=(w_o[0], w_gate[0], w_up[0], w_down[0]))

    zero_st = jnp.zeros((H_A, DV, DK), F32)
    zero_tail = jnp.zeros((CONV_W - 1, D_B), F32)
    _, st_meta, tail_meta = _mix_seq(
        proj_s, bs * dseq, 1, CHUNK, CHUNK, zero_st, zero_tail, gnorm_g[0], conv_w[0],
        mid=N_META // 2, valid_rows=N_META, state_transposed_out=True)

    mix_p, hgrn_p, conv_p = _mix_seq(
        proj_p, 0, bp, seq, MIX_TB, st_meta[0], tail_meta[0], gnorm_g[0], conv_w[0],
        mid=CHUNK // 2)
    mix_s, hgrn_s, conv_s = _mix_group(
        proj_s, bs, state_hgrn[0], state_conv[0], gnorm_g[0], conv_w[0])

    h_p = _outproj(xp, mix_p, w_o_b, g0, b0, ln1_g[0], ln1_b[0], tm=OUTPROJ_TM)
    h_s = _outproj(x_sample, mix_s, w_o_b, g0, b0, ln1_g[0], ln1_b[0], tm=OUTPROJ_TM)

    y_p = _ffn(h_p, wg_b, wu_b, wd_b, ln2_g[0], ln2_b[0], tm=FFN_TM, tf=FFN_TF)
    y_s = _ffn(h_s, wg_b, wu_b, wd_b, ln2_g[0], ln2_b[0], tm=FFN_TM_SMALL, tf=FFN_TF,
               seq_shape=(bs, dseq))

    return (y_p.reshape(bp, seq, D_MODEL), y_s,
            hgrn_p[None], conv_p[None], hgrn_s[None], conv_s[None])
```

```python
import functools

import numpy as np
import jax
import jax.numpy as jnp
from jax.experimental import pallas as pl
from jax.experimental.pallas import tpu as pltpu

F32 = jnp.float32
BF16 = jnp.bfloat16

D_MODEL = 2048
D_A = 1024
D_B = 1024
DK = 128
DV = 128
H_A = 8
SEG = 1024
N_SEG = 7
W_PAD = 128
W_SEG_STRIDE = SEG + W_PAD
PB_Q, PB_K, PB_V, PB_SG, PB_GB = range(5)
PF_LF, PF_U = range(2)
PB_W, PF_W = 5 * SEG, 2 * SEG
N_META = 16
CHUNK = 64
D_FF = 5632
CONV_W = 3
ALPHA = 2.0 ** 0.25
LN_EPS = 1e-5
RMS_EPS = 1e-6

V7X_VMEM_BYTES = 64 * 1024 * 1024
SUBLANES = 8
BF16_ROWS = 16
VMEM_LIMIT = V7X_VMEM_BYTES - 2 * 1024 * 1024

INPROJ_TM = 256
MIX_TB = 512
OUTPROJ_TM = 512
FFN_TM = 1024
FFN_TM_SMALL = 512
FFN_TF = 512

CONV_TAIL = SUBLANES - (CONV_W - 1)


def _seg(i):
    return slice(i * SEG, (i + 1) * SEG)


def _layernorm(x, g, b):
    mu = jnp.mean(x, axis=-1, keepdims=True)
    xc = x - mu
    var = jnp.mean(xc * xc, axis=-1, keepdims=True)
    return xc * jax.lax.rsqrt(var + LN_EPS) * g + b


def _silu(x):
    return x * (1.0 / (1.0 + jnp.exp(-x)))


def _flatten_rows(x3_ref, dst_ref):
    n, slen, _ = x3_ref.shape
    for b in range(n):
        dst_ref[b * slen:(b + 1) * slen, :] = x3_ref[b]


def _unflatten_rows(src_ref, y3_ref):
    n, slen, _ = y3_ref.shape
    for b in range(n):
        y3_ref[b] = src_ref[b * slen:(b + 1) * slen, :]


INPROJ_SUB = 256


def _inproj_kernel(x_ref, g0_ref, b0_ref, w_ref, bf_ref, lb_ref, *refs, n_cast):
    cast_in, refs = refs[:n_cast], refs[n_cast:]
    (pb_ref, pf_ref), cast_out = refs[:2], refs[2:]
    for src, dst in zip(cast_in, cast_out, strict=True):
        _cast_padded(src, dst)

    for r in range(x_ref.shape[0] // INPROJ_SUB):
        rs = slice(r * INPROJ_SUB, (r + 1) * INPROJ_SUB)
        xn = _layernorm(x_ref[rs, :], g0_ref[...], b0_ref[...]).astype(BF16)

        def seg(j, xn=xn):
            return jnp.dot(xn, w_ref[:, j * W_SEG_STRIDE:j * W_SEG_STRIDE + SEG],
                           preferred_element_type=F32)

        pf_ref[rs, _seg(PF_U)] = seg(5) * seg(6)
        pb_ref[rs, _seg(PB_Q)] = _silu(seg(0)).astype(BF16)

        lf, kk = _forget_gate(seg(1) + bf_ref[...], lb_ref[...])
        pf_ref[rs, _seg(PF_LF)] = lf
        pb_ref[rs, _seg(PB_K)] = kk.astype(BF16)

        pb_ref[rs, _seg(PB_SG)] = _silu(seg(3)).astype(BF16)
        pb_ref[rs, _seg(PB_GB)] = seg(4).astype(BF16)
        pb_ref[rs, _seg(PB_V)] = seg(2).astype(BF16)


def _cast_padded(src_ref, dst_ref):
    n_blk = (dst_ref.shape[1] - src_ref.shape[1]) // W_PAD
    cb = src_ref.shape[1] // n_blk
    for b in range(n_blk):
        d0 = b * (cb + W_PAD)
        dst_ref[:, d0:d0 + cb] = src_ref[:, b * cb:(b + 1) * cb].astype(BF16)
        dst_ref[:, d0 + cb:d0 + cb + W_PAD] = jnp.zeros((dst_ref.shape[0], W_PAD), BF16)


def _inproj(x, g0, b0, w_in, b_f, lb, tm, cast=()):
    rows = x.shape[0]
    assert rows % tm == 0 and tm % INPROJ_SUB == 0
    steps = rows // tm
    row = lambda a: a.reshape(1, -1)
    const = lambda i: (0, 0)
    cast_in, cast_out, cast_shapes = [], [], []
    for w, cb in cast:
        assert w.shape[0] % (BF16_ROWS * steps) == 0 and w.shape[1] % cb == 0
        cols = w.shape[1] // cb * (cb + W_PAD)
        cast_in.append(pl.BlockSpec((w.shape[0] // steps, w.shape[1]), lambda i: (i, 0)))
        cast_out.append(pl.BlockSpec((w.shape[0] // steps, cols), lambda i: (i, 0)))
        cast_shapes.append(jax.ShapeDtypeStruct((w.shape[0], cols), BF16))
    res = pl.pallas_call(
        functools.partial(_inproj_kernel, n_cast=len(cast)),
        grid=(steps,),
        in_specs=[
            pl.BlockSpec((tm, D_MODEL), lambda i: (i, 0)),
            pl.BlockSpec((1, D_MODEL), const),
            pl.BlockSpec((1, D_MODEL), const),
            pl.BlockSpec((D_MODEL, N_SEG * W_SEG_STRIDE), const, pipeline_mode=pl.Buffered(1)),
            pl.BlockSpec((1, SEG), const),
            pl.BlockSpec((1, SEG), const),
        ] + cast_in,
        out_specs=[pl.BlockSpec((tm, PB_W), lambda i: (i, 0)),
                   pl.BlockSpec((tm, PF_W), lambda i: (i, 0))] + cast_out,
        out_shape=[jax.ShapeDtypeStruct((rows, PB_W), BF16),
                   jax.ShapeDtypeStruct((rows, PF_W), F32)] + cast_shapes,
        compiler_params=pltpu.CompilerParams(
            dimension_semantics=("arbitrary",), vmem_limit_bytes=VMEM_LIMIT),
        name="inproj",
    )(x, row(g0), row(b0), w_in, row(b_f), row(lb), *(w for w, _ in cast))
    return res[:2], res[2:]


def _forget_gate(z, lb):
    e = jnp.exp(-jnp.abs(z))
    r = 1.0 / (1.0 + e)
    er = e * r
    pos = z >= 0.0
    return jnp.log(lb + (1.0 - lb) * jnp.where(pos, r, er)), (1.0 - lb) * jnp.where(pos, er, r)


def _inproj_cast_kernel(x_ref, meta_ref, g0_ref, b0_ref, w_ref, bf_ref, lbp_ref,
                        pb_ref, pf_ref, wb_ref, lb_ref, xn_scr, gc_scr, xs_scr):
    j = pl.program_id(0)

    @pl.when(j == 0)
    def _():
        n_x, n_m = xs_scr.shape[0], meta_ref.shape[0]
        _flatten_rows(x_ref, xs_scr)
        xn_scr[0:n_x, :] = _layernorm(xs_scr[...], g0_ref[...], b0_ref[...]).astype(BF16)
        xn_scr[n_x:n_x + n_m, :] = _layernorm(meta_ref[...], g0_ref[...],
                                              b0_ref[...]).astype(BF16)
        xn_scr[n_x + n_m:, :] = jnp.zeros((xn_scr.shape[0] - n_x - n_m, D_MODEL), BF16)
        p = lbp_ref[...]
        e = jnp.exp(p - jnp.max(p, axis=0, keepdims=True))
        lb_ref[...] = e[0:1, :] / jnp.sum(e, axis=0, keepdims=True)

    wb = w_ref[...].astype(BF16)
    wb_ref[:, 0:SEG] = wb
    wb_ref[:, SEG:] = jnp.zeros((D_MODEL, W_SEG_STRIDE - SEG), BF16)
    acc = jnp.dot(xn_scr[...], wb, preferred_element_type=F32)

    @pl.when(j == 0)
    def _():
        pb_ref[:, _seg(PB_Q)] = _silu(acc).astype(BF16)

    @pl.when(j == 1)
    def _():
        lf, kk = _forget_gate(acc + bf_ref[...], lb_ref[...])
        pf_ref[:, _seg(PF_LF)] = lf
        pb_ref[:, _seg(PB_K)] = kk.astype(BF16)

    @pl.when(j == 2)
    def _():
        pb_ref[:, _seg(PB_V)] = acc.astype(BF16)

    @pl.when(j == 3)
    def _():
        pb_ref[:, _seg(PB_SG)] = _silu(acc).astype(BF16)

    @pl.when(j == 4)
    def _():
        pb_ref[:, _seg(PB_GB)] = acc.astype(BF16)

    @pl.when(j == 5)
    def _():
        gc_scr[...] = acc

    @pl.when(j == 6)
    def _():
        pf_ref[:, _seg(PF_U)] = gc_scr[...] * acc


def _inproj_cast(x, meta, g0, b0, w_in, b_f, lb_param):
    n_x = x.shape[0] * x.shape[1]
    assert n_x % BF16_ROWS == 0 and meta.shape[0] % BF16_ROWS == 0
    rows = n_x + CHUNK
    row = lambda a: a.reshape(1, -1)
    const = lambda j: (0, 0)
    wseg = lambda: pl.BlockSpec((D_MODEL, SEG), lambda j: (0, j))
    res = pl.pallas_call(
        _inproj_cast_kernel,
        grid=(N_SEG,),
        in_specs=[pl.BlockSpec(x.shape, lambda j: (0, 0, 0), pipeline_mode=pl.Buffered(1)),
                  pl.BlockSpec(meta.shape, const),
                  pl.BlockSpec((1, D_MODEL), const),
                  pl.BlockSpec((1, D_MODEL), const),
                  wseg(),
                  pl.BlockSpec((1, SEG), const),
                  pl.BlockSpec(lb_param.shape, const)],
        out_specs=[pl.BlockSpec((rows, PB_W), const), pl.BlockSpec((rows, PF_W), const),
                   pl.BlockSpec((D_MODEL, W_SEG_STRIDE), lambda j: (0, j)),
                   pl.BlockSpec((1, SEG), const)],
        out_shape=[jax.ShapeDtypeStruct((rows, PB_W), BF16),
                   jax.ShapeDtypeStruct((rows, PF_W), F32),
                   jax.ShapeDtypeStruct((D_MODEL, N_SEG * W_SEG_STRIDE), BF16),
                   jax.ShapeDtypeStruct((1, SEG), F32)],
        scratch_shapes=[pltpu.VMEM((rows, D_MODEL), BF16), pltpu.VMEM((rows, SEG), F32),
                        pltpu.VMEM((n_x, D_MODEL), F32)],
        compiler_params=pltpu.CompilerParams(
            dimension_semantics=("arbitrary",), vmem_limit_bytes=VMEM_LIMIT),
        name="inproj_cast",
    )(x, meta, row(g0), row(b0), w_in, row(b_f), lb_param.astype(F32))
    return res[:2], res[2], res[3]


def _split3(x):
    hi = x.astype(BF16)
    r1 = x - hi.astype(F32)
    mid = r1.astype(BF16)
    lo = (r1 - mid.astype(F32)).astype(BF16)
    return jnp.concatenate([hi, mid, lo], axis=0)


def _cumsum_rows(x):
    rows, lanes = x.shape
    n = rows // SUBLANES
    x3 = x.reshape(n, SUBLANES, lanes)
    pos = jax.lax.broadcasted_iota(jnp.int32, x3.shape, 1)
    s = 1
    while s < SUBLANES:
        x3 = x3 + jnp.where(pos >= s, pltpu.roll(x3, s, 1), 0.0)
        s *= 2
    carry = jnp.zeros((1, lanes), F32)
    out = []
    for g in range(n):
        out.append(x3[g] + carry)
        carry = carry + x3[g, SUBLANES - 1:SUBLANES, :]
    return jnp.concatenate(out, axis=0)


LOG2E = 1.4426950408889634


def _head_tiles(qt, kk, b, bm, bl):
    q1 = (qt * jnp.exp2(b - bm)).astype(BF16)
    k1 = (kk * jnp.exp2(bm - b)).astype(BF16)
    q2 = qt * jnp.exp2(b)
    k2 = kk * jnp.exp2(bl - b)
    return q1, k1, q2, k2


DECAY_ROWS = BF16_ROWS


def _decay_rows(d, rid):
    d_hi = d.astype(BF16).astype(F32)
    d_r = d - d_hi
    d_mid = d_r.astype(BF16).astype(F32)
    return jnp.where(rid == 0, d_hi, jnp.where(rid == 1, d_mid,
                     jnp.where(rid == 2, d_r - d_mid, 0.0)))


def _gated_rmsnorm(o, gn, sg):
    ms = jnp.mean(o * o, axis=-1, keepdims=True)
    return (o * jax.lax.rsqrt(ms + RMS_EPS) * gn * sg).astype(BF16)


def _mix_seq_kernel(pb_ref, pf_ref, st0_ref, tail0_ref, gn_ref, cw_ref, amask_ref,
                    out_ref, sfin_ref, cfin_ref, st_scr, ubuf,
                    *, tb, mid, valid_rows, state_transposed_out):
    t = pl.program_id(1)

    @pl.when(t == 0)
    def _():
        st_scr[...] = st0_ref[...]
        ubuf[CONV_TAIL:SUBLANES, :] = tail0_ref[...]

    u = pf_ref[:, _seg(PF_U)]
    ubuf[SUBLANES:SUBLANES + tb, :] = u
    cw = cw_ref[...]
    conv = (cw[0:1] * ubuf[CONV_TAIL:CONV_TAIL + tb, :]
            + cw[1:2] * ubuf[CONV_TAIL + 1:CONV_TAIL + 1 + tb, :] + cw[2:3] * u)
    out_ref[:, D_A:] = (pb_ref[:, _seg(PB_GB)].astype(F32) * conv).astype(BF16)
    last = valid_rows if valid_rows is not None else tb
    ubuf[CONV_TAIL:SUBLANES, :] = ubuf[CONV_TAIL + last:SUBLANES + last, :]

    causal = amask_ref[...] > 0.0
    gn = gn_ref[...]
    if valid_rows is not None:
        rowmask = jax.lax.broadcasted_iota(jnp.int32, (CHUNK, SEG), 0) < valid_rows
        rowmask_h = jax.lax.broadcasted_iota(jnp.int32, (CHUNK, DK), 0) < valid_rows
    state = [st_scr[h] for h in range(H_A)]
    for ci in range(tb // CHUNK):
        rows = slice(ci * CHUNK, (ci + 1) * CHUNK)
        lf = pf_ref[rows, _seg(PF_LF)]
        if valid_rows is not None:
            lf = jnp.where(rowmask, lf, 0.0)
        b_all = _cumsum_rows(lf) * LOG2E
        for h in range(H_A):
            sl = slice(h * DK, (h + 1) * DK)
            col = lambda seg_i, h=h: slice(seg_i * SEG + h * DK, seg_i * SEG + (h + 1) * DK)
            b = b_all[:, sl]
            bm = b[mid:mid + 1, :]
            bl = b[CHUNK - 1:CHUNK, :]
            kk = pb_ref[rows, col(PB_K)].astype(F32)
            if valid_rows is not None:
                kk = jnp.where(rowmask_h, kk, 0.0)
            q1, k1, q2, k2 = _head_tiles(pb_ref[rows, col(PB_Q)].astype(F32), kk, b, bm, bl)
            vb = pb_ref[rows, col(PB_V)]
            st = state[h]
            a = pl.dot(q1, k1, trans_b=True)
            a = jnp.where(causal, a, 0.0).astype(BF16)
            o = (jnp.dot(a, vb, preferred_element_type=F32)
                 + jnp.dot(q2.astype(BF16), st.T.astype(BF16), preferred_element_type=F32))
            state[h] = st * jnp.exp2(bl) + pl.dot(vb, k2.astype(BF16), trans_a=True)
            out_ref[rows, sl] = _gated_rmsnorm(o, gn[:, sl],
                                               pb_ref[rows, col(PB_SG)].astype(F32))
    for h in range(H_A):
        st_scr[h] = state[h]

    @pl.when(t == pl.num_programs(1) - 1)
    def _():
        for h in range(H_A):
            sfin_ref[0, h] = st_scr[h] if state_transposed_out else st_scr[h].T
        cfin_ref[0] = ubuf[CONV_TAIL:SUBLANES, :]


def _mix_seq(proj, row0, nseq, seq_len, tb, st0, tail0, gn, cw, *, mid, valid_rows=None,
             state_transposed_out=False):
    nt = seq_len // tb
    assert seq_len % tb == 0 and row0 % tb == 0 and tb % CHUNK == 0
    rb0 = row0 // tb
    rowblk = lambda b, t: (rb0 + b * nt + t, 0)
    const2 = lambda b, t: (0, 0)
    tril = np.tril(np.ones((CHUNK, CHUNK), np.float32))
    kern = functools.partial(_mix_seq_kernel, tb=tb, mid=mid, valid_rows=valid_rows,
                             state_transposed_out=state_transposed_out)
    return pl.pallas_call(
        kern,
        grid=(nseq, nt),
        in_specs=[pl.BlockSpec((tb, PB_W), rowblk), pl.BlockSpec((tb, PF_W), rowblk),
                  pl.BlockSpec((H_A, DV, DK), lambda b, t: (0, 0, 0)),
                  pl.BlockSpec((CONV_W - 1, D_B), const2),
                  pl.BlockSpec((1, D_A), const2),
                  pl.BlockSpec((CONV_W, D_B), const2),
                  pl.BlockSpec((CHUNK, CHUNK), const2)],
        out_specs=[pl.BlockSpec((tb, D_MODEL), lambda b, t: (b * nt + t, 0)),
                   pl.BlockSpec((1, H_A, DK, DV), lambda b, t: (b, 0, 0, 0)),
                   pl.BlockSpec((1, CONV_W - 1, D_B), lambda b, t: (b, 0, 0))],
        out_shape=[jax.ShapeDtypeStruct((nseq * seq_len, D_MODEL), BF16),
                   jax.ShapeDtypeStruct((nseq, H_A, DK, DV), F32),
                   jax.ShapeDtypeStruct((nseq, CONV_W - 1, D_B), F32)],
        scratch_shapes=[pltpu.VMEM((H_A, DV, DK), F32), pltpu.VMEM((SUBLANES + tb, D_B), F32)],
        compiler_params=pltpu.CompilerParams(
            dimension_semantics=("arbitrary", "arbitrary"), vmem_limit_bytes=VMEM_LIMIT),
        name="mix_seq",
    )(*proj, st0, tail0, gn.reshape(1, -1), cw, jnp.asarray(tril))


GROUP = 16
S_LEN = 4
S_MID = 2


def _mix_group_kernel(pb_ref, pf_ref, s_ref, cbuf_ref, gn_ref, cw_ref, lmat_ref, amask_ref,
                      out_ref, sfin_ref, cfin_ref, full_scr, y_scr):
    cw = cw_ref[...]
    nb = CONV_W - 1
    for s in range(GROUP):
        r0 = SUBLANES * s
        full_scr[r0:r0 + nb, :] = cbuf_ref[s]
        full_scr[r0 + nb:r0 + nb + S_LEN, :] = pf_ref[S_LEN * s:S_LEN * (s + 1), _seg(PF_U)]
    for s in range(GROUP):
        r0 = SUBLANES * s
        f0 = full_scr[r0:r0 + S_LEN, :]
        f1 = full_scr[r0 + 1:r0 + 1 + S_LEN, :]
        f2 = full_scr[r0 + 2:r0 + 2 + S_LEN, :]
        y_scr[S_LEN * s:S_LEN * (s + 1), :] = cw[0:1] * f0 + cw[1:2] * f1 + cw[2:3] * f2
        cfin_ref[s] = full_scr[r0 + S_LEN:r0 + S_LEN + nb, :]
    out_ref[:, D_A:] = (pb_ref[:, _seg(PB_GB)].astype(F32) * y_scr[...]).astype(BF16)

    b3 = jnp.dot(lmat_ref[...], _split3(pf_ref[:, _seg(PF_LF)]),
                 preferred_element_type=F32) * LOG2E
    causal = amask_ref[...] > 0.0
    gn = gn_ref[...]
    sub = BF16_ROWS
    per_sub = sub // S_LEN
    rid = jax.lax.broadcasted_iota(jnp.int32, (sub, DK), 0)
    own = [(rid >= S_LEN * j) & (rid < S_LEN * (j + 1)) for j in range(per_sub)]
    ones_blk = jnp.ones((sub, DV), BF16)
    zeros_blk = jnp.zeros((sub, DV), BF16)
    for h in range(H_A):
        sl = slice(h * DK, (h + 1) * DK)
        col = lambda seg_i, h=h: slice(seg_i * SEG + h * DK, seg_i * SEG + (h + 1) * DK)
        b = b3[0:CHUNK, sl]
        bm = b3[CHUNK:2 * CHUNK, sl]
        bl = b3[2 * CHUNK:3 * CHUNK, sl]
        q1, k1, q2, k2 = _head_tiles(pb_ref[:, col(PB_Q)].astype(F32),
                                     pb_ref[:, col(PB_K)].astype(F32), b, bm, bl)
        vb = pb_ref[:, col(PB_V)]
        a = pl.dot(q1, k1, trans_b=True)
        a = jnp.where(causal, a, 0.0).astype(BF16)
        o1 = jnp.dot(a, vb, preferred_element_type=F32)
        decay = jnp.exp2(bl)
        for blk in range(CHUNK // sub):
            rs = slice(blk * sub, (blk + 1) * sub)
            q2b, k2b, vbb = q2[rs], k2[rs], vb[rs]
            rhs = jnp.concatenate([jnp.concatenate([vbb, zeros_blk], axis=1),
                                   jnp.concatenate([zeros_blk, ones_blk], axis=1)], axis=0)
            acc = o1[rs]
            for j in range(per_sub):
                s = blk * per_sub + j
                st = s_ref[s, h]
                qm = jnp.where(own[j], q2b, 0.0).astype(BF16)
                acc = acc + jnp.dot(qm, st.astype(BF16), preferred_element_type=F32)
                drows = _decay_rows(decay[S_LEN * s:S_LEN * s + 1, :], rid)
                lhs = jnp.concatenate([jnp.where(own[j], k2b, 0.0), drows], axis=0).astype(BF16)
                ud = pl.dot(lhs, rhs, trans_a=True)
                sfin_ref[s, h] = ud[:, DV:] * st + ud[:, :DV]
            out_ref[rs, sl] = _gated_rmsnorm(acc, gn[:, sl],
                                             pb_ref[rs, col(PB_SG)].astype(F32))


def _group_mats():
    r = np.arange(CHUNK)
    seq, pos = r // S_LEN, r % S_LEN
    same = seq[:, None] == seq[None, :]
    cum = same & (pos[None, :] <= pos[:, None])
    midm = same & (pos[None, :] <= S_MID)
    lmat = np.concatenate([cum, midm, same], axis=0).astype(np.float32)
    lmat = np.concatenate([lmat, lmat, lmat], axis=1)
    return jnp.asarray(lmat, BF16), jnp.asarray(cum.astype(np.float32))


def _mix_group(proj, nseq, s0, cbuf, gn, cw):
    steps = nseq // GROUP
    const2 = lambda i: (0, 0)
    lmat, amask = _group_mats()
    return pl.pallas_call(
        _mix_group_kernel,
        grid=(steps,),
        in_specs=[pl.BlockSpec((CHUNK, PB_W), lambda i: (i, 0)),
                  pl.BlockSpec((CHUNK, PF_W), lambda i: (i, 0)),
                  pl.BlockSpec((GROUP, H_A, DK, DV), lambda i: (i, 0, 0, 0)),
                  pl.BlockSpec((GROUP, CONV_W - 1, D_B), lambda i: (i, 0, 0)),
                  pl.BlockSpec((1, D_A), const2),
                  pl.BlockSpec((CONV_W, D_B), const2),
                  pl.BlockSpec((3 * CHUNK, 3 * CHUNK), const2),
                  pl.BlockSpec((CHUNK, CHUNK), const2)],
        out_specs=[pl.BlockSpec((CHUNK, D_MODEL), lambda i: (i, 0)),
                   pl.BlockSpec((GROUP, H_A, DK, DV), lambda i: (i, 0, 0, 0)),
                   pl.BlockSpec((GROUP, CONV_W - 1, D_B), lambda i: (i, 0, 0))],
        out_shape=[jax.ShapeDtypeStruct((nseq * S_LEN, D_MODEL), BF16),
                   jax.ShapeDtypeStruct((nseq, H_A, DK, DV), F32),
                   jax.ShapeDtypeStruct((nseq, CONV_W - 1, D_B), F32)],
        scratch_shapes=[pltpu.VMEM((SUBLANES * GROUP, D_B), F32), pltpu.VMEM((CHUNK, D_B), F32)],
        compiler_params=pltpu.CompilerParams(
            dimension_semantics=("arbitrary",), vmem_limit_bytes=VMEM_LIMIT),
        name="mix_group",
    )(*proj, s0, cbuf, gn.reshape(1, -1), cw, lmat, amask)


def _outproj_kernel(x_ref, mix_ref, wo_ref, g0_ref, b0_ref, g1_ref, b1_ref, h_ref, *xs_scr):
    if xs_scr:
        _flatten_rows(x_ref, xs_scr[0])
        x_ref = xs_scr[0]
    xn = _layernorm(x_ref[...], g0_ref[...], b0_ref[...])
    m = jnp.dot(mix_ref[...], wo_ref[:, 0:D_MODEL], preferred_element_type=F32)
    h_ref[...] = _layernorm(ALPHA * xn + m, g1_ref[...], b1_ref[...])


def _outproj(x, mix, w_o, g0, b0, g1, b1, tm):
    rows = mix.shape[0]
    tm = min(tm, rows)
    assert rows % tm == 0 and (x.ndim == 2 or rows == tm)
    row = lambda a: a.reshape(1, -1)
    const = lambda i: (0, 0)
    vec = pl.BlockSpec((1, D_MODEL), const)
    x_spec = (pl.BlockSpec((tm, D_MODEL), lambda i: (i, 0)) if x.ndim == 2 else
              pl.BlockSpec(x.shape, lambda i: (0, 0, 0), pipeline_mode=pl.Buffered(1)))
    return pl.pallas_call(
        _outproj_kernel,
        grid=(rows // tm,),
        in_specs=[x_spec,
                  pl.BlockSpec((tm, D_MODEL), lambda i: (i, 0)),
                  pl.BlockSpec((D_MODEL, D_MODEL + W_PAD), const),
                  vec, vec, vec, vec],
        out_specs=pl.BlockSpec((tm, D_MODEL), lambda i: (i, 0)),
        out_shape=jax.ShapeDtypeStruct((rows, D_MODEL), F32),
        scratch_shapes=[] if x.ndim == 2 else [pltpu.VMEM((rows, D_MODEL), F32)],
        compiler_params=pltpu.CompilerParams(
            dimension_semantics=("arbitrary",), vmem_limit_bytes=VMEM_LIMIT),
        name="outproj",
    )(x, mix, w_o, row(g0), row(b0), row(g1), row(b1))


def _ffn_kernel(h_ref, wg_ref, wu_ref, wd_ref, g2_ref, b2_ref, o_ref, hb_ref, *acc_scr):
    acc_ref = acc_scr[0] if acc_scr else o_ref
    f = pl.program_id(1)

    @pl.when(f == 0)
    def _():
        h = h_ref[...]
        hb_ref[...] = h.astype(BF16)
        acc_ref[...] = ALPHA * h

    hb = hb_ref[...]
    tf = wd_ref.shape[0]
    g = jnp.dot(hb, wg_ref[:, 0:tf], preferred_element_type=F32)
    u = jnp.dot(hb, wu_ref[:, 0:tf], preferred_element_type=F32)
    a = (_silu(g) * u).astype(BF16)
    acc_ref[...] += jnp.dot(a, wd_ref[:, 0:D_MODEL], preferred_element_type=F32)

    @pl.when(f == pl.num_programs(1) - 1)
    def _():
        acc_ref[...] = _layernorm(acc_ref[...], g2_ref[...], b2_ref[...])
        if acc_scr:
            _unflatten_rows(acc_ref, o_ref)


def _ffn(h, w_gate, w_up, w_down, g2, b2, tm, tf, seq_shape=None):
    rows = h.shape[0]
    assert rows % tm == 0 and D_FF % tf == 0
    row = lambda a: a.reshape(1, -1)
    vec = pl.BlockSpec((1, D_MODEL), lambda i, f: (0, 0))
    if seq_shape is None:
        out_spec = pl.BlockSpec((tm, D_MODEL), lambda i, f: (i, 0))
        out_shape = jax.ShapeDtypeStruct((rows, D_MODEL), F32)
        acc = []
    else:
        assert rows == tm == seq_shape[0] * seq_shape[1]
        out_spec = pl.BlockSpec((*seq_shape, D_MODEL), lambda i, f: (0, 0, 0))
        out_shape = jax.ShapeDtypeStruct((*seq_shape, D_MODEL), F32)
        acc = [pltpu.VMEM((tm, D_MODEL), F32)]
    return pl.pallas_call(
        _ffn_kernel,
        grid=(rows // tm, D_FF // tf),
        in_specs=[pl.BlockSpec((tm, D_MODEL), lambda i, f: (i, 0)),
                  pl.BlockSpec((D_MODEL, tf + W_PAD), lambda i, f: (0, f)),
                  pl.BlockSpec((D_MODEL, tf + W_PAD), lambda i, f: (0, f)),
                  pl.BlockSpec((tf, D_MODEL + W_PAD), lambda i, f: (f, 0)),
                  vec, vec],
        out_specs=out_spec,
        out_shape=out_shape,
        scratch_shapes=[pltpu.VMEM((tm, D_MODEL), BF16)] + acc,
        compiler_params=pltpu.CompilerParams(
            dimension_semantics=("arbitrary", "arbitrary"), vmem_limit_bytes=VMEM_LIMIT),
        name="ffn",
    )(h, w_gate, w_up, w_down, row(g2), row(b2))


def kernel(x_prompt, x_sample, state_hgrn, state_conv, meta_tokens, ln0_g, ln0_b, w_in, b_f, lb_param, gnorm_g, conv_w, w_o, ln1_g, ln1_b, w_gate, w_up, w_down, ln2_g, ln2_b):
    bp, seq, _ = x_prompt.shape
    bs, dseq, _ = x_sample.shape
    assert dseq == S_LEN and seq % CHUNK == 0 and bs % GROUP == 0

    g0, b0 = ln0_g.astype(F32), ln0_b.astype(F32)

    xp = x_prompt.reshape(bp * seq, D_MODEL)
    proj_s, w_in_b, lb = _inproj_cast(x_sample, meta_tokens.astype(F32), g0, b0, w_in[0],
                                      b_f[0], lb_param)
    proj_p, (w_o_b, wg_b, wu_b, wd_b) = _inproj(
        xp, g0, b0, w_in_b, b_f[0], lb, tm=INPROJ_TM,
        cast=((w_o[0], D_MODEL), (w_gate[0], FFN_TF), (w_up[0], FFN_TF), (w_down[0], D_MODEL)))

    zero_st = jnp.zeros((H_A, DV, DK), F32)
    zero_tail = jnp.zeros((CONV_W - 1, D_B), F32)
    _, st_meta, tail_meta = _mix_seq(
        proj_s, bs * dseq, 1, CHUNK, CHUNK, zero_st, zero_tail, gnorm_g[0], conv_w[0],
        mid=N_META // 2, valid_rows=N_META, state_transposed_out=True)

    mix_p, hgrn_p, conv_p = _mix_seq(
        proj_p, 0, bp, seq, MIX_TB, st_meta[0], tail_meta[0], gnorm_g[0], conv_w[0],
        mid=CHUNK // 2)
    mix_s, hgrn_s, conv_s = _mix_group(
        proj_s, bs, state_hgrn[0], state_conv[0], gnorm_g[0], conv_w[0])

    h_p = _outproj(xp, mix_p, w_o_b, g0, b0, ln1_g[0], ln1_b[0], tm=OUTPROJ_TM)
    h_s = _outproj(x_sample, mix_s, w_o_b, g0, b0, ln1_g[0], ln1_b[0], tm=OUTPROJ_TM)

    y_p = _ffn(h_p, wg_b, wu_b, wd_b, ln2_g[0], ln2_b[0], tm=FFN_TM, tf=FFN_TF)
    y_s = _ffn(h_s, wg_b, wu_b, wd_b, ln2_g[0], ln2_b[0], tm=FFN_TM_SMALL, tf=FFN_TF,
               seq_shape=(bs, dseq))

    return (y_p.reshape(bp, seq, D_MODEL), y_s,
            hgrn_p[None], conv_p[None], hgrn_s[None], conv_s[None])
```

```python
import functools

import numpy as np
import jax
import jax.numpy as jnp
from jax.experimental import pallas as pl
from jax.experimental.pallas import tpu as pltpu

F32 = jnp.float32
BF16 = jnp.bfloat16

D_MODEL = 2048
D_A = 1024
D_B = 1024
DK = 128
DV = 128
H_A = 8
SEG = 1024
N_SEG = 7
W_PAD = 128
W_SEG_STRIDE = SEG + W_PAD
PB_Q, PB_K, PB_V, PB_SG, PB_GB = range(5)
PF_LF, PF_U = range(2)
PB_W, PF_W = 5 * SEG, 2 * SEG
N_META = 16
CHUNK = 64
D_FF = 5632
CONV_W = 3
ALPHA = 2.0 ** 0.25
LN_EPS = 1e-5
RMS_EPS = 1e-6

V7X_VMEM_BYTES = 64 * 1024 * 1024
SUBLANES = 8
BF16_ROWS = 16
VMEM_LIMIT = V7X_VMEM_BYTES - 2 * 1024 * 1024

INPROJ_TM = 256
MIX_TB = 512
OUTPROJ_TM = 512
FFN_TM = 1024
FFN_TM_SMALL = 512
FFN_TF = 512

CONV_TAIL = SUBLANES - (CONV_W - 1)


def _seg(i):
    return slice(i * SEG, (i + 1) * SEG)


def _layernorm(x, g, b):
    mu = jnp.mean(x, axis=-1, keepdims=True)
    xc = x - mu
    var = jnp.mean(xc * xc, axis=-1, keepdims=True)
    return xc * jax.lax.rsqrt(var + LN_EPS) * g + b


def _silu(x):
    return x * (1.0 / (1.0 + jnp.exp(-x)))


def _flatten_rows(x3_ref, dst_ref):
    n, slen, _ = x3_ref.shape
    for b in range(n):
        dst_ref[b * slen:(b + 1) * slen, :] = x3_ref[b]


def _unflatten_rows(src_ref, y3_ref):
    n, slen, _ = y3_ref.shape
    for b in range(n):
        y3_ref[b] = src_ref[b * slen:(b + 1) * slen, :]


INPROJ_SUB = 256


def _inproj_kernel(x_ref, g0_ref, b0_ref, w_ref, bf_ref, lb_ref, *refs, n_cast):
    cast_in, refs = refs[:n_cast], refs[n_cast:]
    (pb_ref, pf_ref), cast_out = refs[:2], refs[2:]
    for src, dst in zip(cast_in, cast_out, strict=True):
        _cast_padded(src, dst)

    for r in range(x_ref.shape[0] // INPROJ_SUB):
        rs = slice(r * INPROJ_SUB, (r + 1) * INPROJ_SUB)
        xn = _layernorm(x_ref[rs, :], g0_ref[...], b0_ref[...]).astype(BF16)

        def seg(j, xn=xn):
            return jnp.dot(xn, w_ref[:, j * W_SEG_STRIDE:j * W_SEG_STRIDE + SEG],
                           preferred_element_type=F32)

        pf_ref[rs, _seg(PF_U)] = seg(5) * seg(6)
        pb_ref[rs, _seg(PB_Q)] = _silu(seg(0)).astype(BF16)

        lf, kk = _forget_gate(seg(1) + bf_ref[...], lb_ref[...])
        pf_ref[rs, _seg(PF_LF)] = lf
        pb_ref[rs, _seg(PB_K)] = kk.astype(BF16)

        pb_ref[rs, _seg(PB_SG)] = _silu(seg(3)).astype(BF16)
        pb_ref[rs, _seg(PB_GB)] = seg(4).astype(BF16)
        pb_ref[rs, _seg(PB_V)] = seg(2).astype(BF16)


def _cast_padded(src_ref, dst_ref):
    n_blk = (dst_ref.shape[1] - src_ref.shape[1]) // W_PAD
    cb = src_ref.shape[1] // n_blk
    for b in range(n_blk):
        d0 = b * (cb + W_PAD)
        dst_ref[:, d0:d0 + cb] = src_ref[:, b * cb:(b + 1) * cb].astype(BF16)
        dst_ref[:, d0 + cb:d0 + cb + W_PAD] = jnp.zeros((dst_ref.shape[0], W_PAD), BF16)


def _inproj(x, g0, b0, w_in, b_f, lb, tm, cast=()):
    rows = x.shape[0]
    assert rows % tm == 0 and tm % INPROJ_SUB == 0
    steps = rows // tm
    row = lambda a: a.reshape(1, -1)
    const = lambda i: (0, 0)
    cast_in, cast_out, cast_shapes = [], [], []
    for w, cb in cast:
        assert w.shape[0] % (BF16_ROWS * steps) == 0 and w.shape[1] % cb == 0
        cols = w.shape[1] // cb * (cb + W_PAD)
        cast_in.append(pl.BlockSpec((w.shape[0] // steps, w.shape[1]), lambda i: (i, 0)))
        cast_out.append(pl.BlockSpec((w.shape[0] // steps, cols), lambda i: (i, 0)))
        cast_shapes.append(jax.ShapeDtypeStruct((w.shape[0], cols), BF16))
    res = pl.pallas_call(
        functools.partial(_inproj_kernel, n_cast=len(cast)),
        grid=(steps,),
        in_specs=[
            pl.BlockSpec((tm, D_MODEL), lambda i: (i, 0)),
            pl.BlockSpec((1, D_MODEL), const),
            pl.BlockSpec((1, D_MODEL), const),
            pl.BlockSpec((D_MODEL, N_SEG * W_SEG_STRIDE), const, pipeline_mode=pl.Buffered(1)),
            pl.BlockSpec((1, SEG), const),
            pl.BlockSpec((1, SEG), const),
        ] + cast_in,
        out_specs=[pl.BlockSpec((tm, PB_W), lambda i: (i, 0)),
                   pl.BlockSpec((tm, PF_W), lambda i: (i, 0))] + cast_out,
        out_shape=[jax.ShapeDtypeStruct((rows, PB_W), BF16),
                   jax.ShapeDtypeStruct((rows, PF_W), F32)] + cast_shapes,
        compiler_params=pltpu.CompilerParams(
            dimension_semantics=("arbitrary",), vmem_limit_bytes=VMEM_LIMIT),
        name="inproj",
    )(x, row(g0), row(b0), w_in, row(b_f), row(lb), *(w for w, _ in cast))
    return res[:2], res[2:]


def _forget_gate(z, lb):
    e = jnp.exp(-jnp.abs(z))
    r = 1.0 / (1.0 + e)
    er = e * r
    pos = z >= 0.0
    return jnp.log(lb + (1.0 - lb) * jnp.where(pos, r, er)), (1.0 - lb) * jnp.where(pos, er, r)


def _inproj_cast_kernel(x_ref, meta_ref, g0_ref, b0_ref, w_ref, bf_ref, lbp_ref,
                        pb_ref, pf_ref, wb_ref, lb_ref, xn_scr, gc_scr, xs_scr):
    j = pl.program_id(0)

    @pl.when(j == 0)
    def _():
        n_x, n_m = xs_scr.shape[0], meta_ref.shape[0]
        _flatten_rows(x_ref, xs_scr)
        xn_scr[0:n_x, :] = _layernorm(xs_scr[...], g0_ref[...], b0_ref[...]).astype(BF16)
        xn_scr[n_x:n_x + n_m, :] = _layernorm(meta_ref[...], g0_ref[...],
                                              b0_ref[...]).astype(BF16)
        xn_scr[n_x + n_m:, :] = jnp.zeros((xn_scr.shape[0] - n_x - n_m, D_MODEL), BF16)
        p = lbp_ref[...]
        e = jnp.exp(p - jnp.max(p, axis=0, keepdims=True))
        lb_ref[...] = e[0:1, :] / jnp.sum(e, axis=0, keepdims=True)

    wb = w_ref[...].astype(BF16)
    wb_ref[:, 0:SEG] = wb
    wb_ref[:, SEG:] = jnp.zeros((D_MODEL, W_SEG_STRIDE - SEG), BF16)
    acc = jnp.dot(xn_scr[...], wb, preferred_element_type=F32)

    @pl.when(j == 0)
    def _():
        pb_ref[:, _seg(PB_Q)] = _silu(acc).astype(BF16)

    @pl.when(j == 1)
    def _():
        lf, kk = _forget_gate(acc + bf_ref[...], lb_ref[...])
        pf_ref[:, _seg(PF_LF)] = lf
        pb_ref[:, _seg(PB_K)] = kk.astype(BF16)

    @pl.when(j == 2)
    def _():
        pb_ref[:, _seg(PB_V)] = acc.astype(BF16)

    @pl.when(j == 3)
    def _():
        pb_ref[:, _seg(PB_SG)] = _silu(acc).astype(BF16)

    @pl.when(j == 4)
    def _():
        pb_ref[:, _seg(PB_GB)] = acc.astype(BF16)

    @pl.when(j == 5)
    def _():
        gc_scr[...] = acc

    @pl.when(j == 6)
    def _():
        pf_ref[:, _seg(PF_U)] = gc_scr[...] * acc


def _inproj_cast(x, meta, g0, b0, w_in, b_f, lb_param):
    n_x = x.shape[0] * x.shape[1]
    assert n_x % BF16_ROWS == 0 and meta.shape[0] % BF16_ROWS == 0
    rows = n_x + CHUNK
    row = lambda a: a.reshape(1, -1)
    const = lambda j: (0, 0)
    wseg = lambda: pl.BlockSpec((D_MODEL, SEG), lambda j: (0, j))
    res = pl.pallas_call(
        _inproj_cast_kernel,
        grid=(N_SEG,),
        in_specs=[pl.BlockSpec(x.shape, lambda j: (0, 0, 0), pipeline_mode=pl.Buffered(1)),
                  pl.BlockSpec(meta.shape, const),
                  pl.BlockSpec((1, D_MODEL), const),
                  pl.BlockSpec((1, D_MODEL), const),
                  wseg(),
                  pl.BlockSpec((1, SEG), const),
                  pl.BlockSpec(lb_param.shape, const)],
        out_specs=[pl.BlockSpec((rows, PB_W), const), pl.BlockSpec((rows, PF_W), const),
                   pl.BlockSpec((D_MODEL, W_SEG_STRIDE), lambda j: (0, j)),
                   pl.BlockSpec((1, SEG), const)],
        out_shape=[jax.ShapeDtypeStruct((rows, PB_W), BF16),
                   jax.ShapeDtypeStruct((rows, PF_W), F32),
                   jax.ShapeDtypeStruct((D_MODEL, N_SEG * W_SEG_STRIDE), BF16),
                   jax.ShapeDtypeStruct((1, SEG), F32)],
        scratch_shapes=[pltpu.VMEM((rows, D_MODEL), BF16), pltpu.VMEM((rows, SEG), F32),
                        pltpu.VMEM((n_x, D_MODEL), F32)],
        compiler_params=pltpu.CompilerParams(
            dimension_semantics=("arbitrary",), vmem_limit_bytes=VMEM_LIMIT),
        name="inproj_cast",
    )(x, meta, row(g0), row(b0), w_in, row(b_f), lb_param.astype(F32))
    return res[:2], res[2], res[3]


def _split3(x):
    hi = x.astype(BF16)
    r1 = x - hi.astype(F32)
    mid = r1.astype(BF16)
    lo = (r1 - mid.astype(F32)).astype(BF16)
    return jnp.concatenate([hi, mid, lo], axis=0)


def _cumsum_rows(x):
    rows, lanes = x.shape
    n = rows // SUBLANES
    x3 = x.reshape(n, SUBLANES, lanes)
    pos = jax.lax.broadcasted_iota(jnp.int32, x3.shape, 1)
    s = 1
    while s < SUBLANES:
        x3 = x3 + jnp.where(pos >= s, pltpu.roll(x3, s, 1), 0.0)
        s *= 2
    carry = jnp.zeros((1, lanes), F32)
    out = []
    for g in range(n):
        out.append(x3[g] + carry)
        carry = carry + x3[g, SUBLANES - 1:SUBLANES, :]
    return jnp.concatenate(out, axis=0)


LOG2E = 1.4426950408889634


def _head_tiles(qt, kk, b, bm, bl):
    q1 = (qt * jnp.exp2(b - bm)).astype(BF16)
    k1 = (kk * jnp.exp2(bm - b)).astype(BF16)
    q2 = qt * jnp.exp2(b)
    k2 = kk * jnp.exp2(bl - b)
    return q1, k1, q2, k2


DECAY_ROWS = BF16_ROWS


def _decay_rows(d, rid):
    d_hi = d.astype(BF16).astype(F32)
    d_r = d - d_hi
    d_mid = d_r.astype(BF16).astype(F32)
    return jnp.where(rid == 0, d_hi, jnp.where(rid == 1, d_mid,
                     jnp.where(rid == 2, d_r - d_mid, 0.0)))


def _gated_rmsnorm(o, gn, sg):
    ms = jnp.mean(o * o, axis=-1, keepdims=True)
    return (o * jax.lax.rsqrt(ms + RMS_EPS) * gn * sg).astype(BF16)


def _mix_seq_kernel(pb_ref, pf_ref, st0_ref, tail0_ref, gn_ref, cw_ref, amask_ref,
                    out_ref, sfin_ref, cfin_ref, st_scr, ubuf,
                    *, tb, mid, valid_rows, state_transposed_out):
    t = pl.program_id(1)

    @pl.when(t == 0)
    def _():
        st_scr[...] = st0_ref[...]
        ubuf[CONV_TAIL:SUBLANES, :] = tail0_ref[...]

    u = pf_ref[:, _seg(PF_U)]
    ubuf[SUBLANES:SUBLANES + tb, :] = u
    cw = cw_ref[...]
    conv = (cw[0:1] * ubuf[CONV_TAIL:CONV_TAIL + tb, :]
            + cw[1:2] * ubuf[CONV_TAIL + 1:CONV_TAIL + 1 + tb, :] + cw[2:3] * u)
    out_ref[:, D_A:] = (pb_ref[:, _seg(PB_GB)].astype(F32) * conv).astype(BF16)
    last = valid_rows if valid_rows is not None else tb
    ubuf[CONV_TAIL:SUBLANES, :] = ubuf[CONV_TAIL + last:SUBLANES + last, :]

    causal = amask_ref[...] > 0.0
    gn = gn_ref[...]
    if valid_rows is not None:
        rowmask = jax.lax.broadcasted_iota(jnp.int32, (CHUNK, SEG), 0) < valid_rows
        rowmask_h = jax.lax.broadcasted_iota(jnp.int32, (CHUNK, DK), 0) < valid_rows
    state = [st_scr[h] for h in range(H_A)]
    for ci in range(tb // CHUNK):
        rows = slice(ci * CHUNK, (ci + 1) * CHUNK)
        lf = pf_ref[rows, _seg(PF_LF)]
        if valid_rows is not None:
            lf = jnp.where(rowmask, lf, 0.0)
        b_all = _cumsum_rows(lf) * LOG2E
        for h in range(H_A):
            sl = slice(h * DK, (h + 1) * DK)
            col = lambda seg_i, h=h: slice(seg_i * SEG + h * DK, seg_i * SEG + (h + 1) * DK)
            b = b_all[:, sl]
            bm = b[mid:mid + 1, :]
            bl = b[CHUNK - 1:CHUNK, :]
            kk = pb_ref[rows, col(PB_K)].astype(F32)
            if valid_rows is not None:
                kk = jnp.where(rowmask_h, kk, 0.0)
            q1, k1, q2, k2 = _head_tiles(pb_ref[rows, col(PB_Q)].astype(F32), kk, b, bm, bl)
            vb = pb_ref[rows, col(PB_V)]
            st = state[h]
            a = pl.dot(q1, k1, trans_b=True)
            a = jnp.where(causal, a, 0.0).astype(BF16)
            o = (jnp.dot(a, vb, preferred_element_type=F32)
                 + jnp.dot(q2.astype(BF16), st.T.astype(BF16), preferred_element_type=F32))
            state[h] = st * jnp.exp2(bl) + pl.dot(vb, k2.astype(BF16), trans_a=True)
            out_ref[rows, sl] = _gated_rmsnorm(o, gn[:, sl],
                                               pb_ref[rows, col(PB_SG)].astype(F32))
    for h in range(H_A):
        st_scr[h] = state[h]

    @pl.when(t == pl.num_programs(1) - 1)
    def _():
        for h in range(H_A):
            sfin_ref[0, h] = st_scr[h] if state_transposed_out else st_scr[h].T
        cfin_ref[0] = ubuf[CONV_TAIL:SUBLANES, :]


def _mix_seq(proj, row0, nseq, seq_len, tb, st0, tail0, gn, cw, *, mid, valid_rows=None,
             state_transposed_out=False):
    nt = seq_len // tb
    assert seq_len % tb == 0 and row0 % tb == 0 and tb % CHUNK == 0
    rb0 = row0 // tb
    rowblk = lambda b, t: (rb0 + b * nt + t, 0)
    const2 = lambda b, t: (0, 0)
    tril = np.tril(np.ones((CHUNK, CHUNK), np.float32))
    kern = functools.partial(_mix_seq_kernel, tb=tb, mid=mid, valid_rows=valid_rows,
                             state_transposed_out=state_transposed_out)
    return pl.pallas_call(
        kern,
        grid=(nseq, nt),
        in_specs=[pl.BlockSpec((tb, PB_W), rowblk), pl.BlockSpec((tb, PF_W), rowblk),
                  pl.BlockSpec((H_A, DV, DK), lambda b, t: (0, 0, 0)),
                  pl.BlockSpec((CONV_W - 1, D_B), const2),
                  pl.BlockSpec((1, D_A), const2),
                  pl.BlockSpec((CONV_W, D_B), const2),
                  pl.BlockSpec((CHUNK, CHUNK), const2)],
        out_specs=[pl.BlockSpec((tb, D_MODEL), lambda b, t: (b * nt + t, 0)),
                   pl.BlockSpec((1, H_A, DK, DV), lambda b, t: (b, 0, 0, 0)),
                   pl.BlockSpec((1, CONV_W - 1, D_B), lambda b, t: (b, 0, 0))],
        out_shape=[jax.ShapeDtypeStruct((nseq * seq_len, D_MODEL), BF16),
                   jax.ShapeDtypeStruct((nseq, H_A, DK, DV), F32),
                   jax.ShapeDtypeStruct((nseq, CONV_W - 1, D_B), F32)],
        scratch_shapes=[pltpu.VMEM((H_A, DV, DK), F32), pltpu.VMEM((SUBLANES + tb, D_B), F32)],
        compiler_params=pltpu.CompilerParams(
            dimension_semantics=("arbitrary", "arbitrary"), vmem_limit_bytes=VMEM_LIMIT),
        name="mix_seq",
    )(*proj, st0, tail0, gn.reshape(1, -1), cw, jnp.asarray(tril))


GROUP = 16
S_LEN = 4
S_MID = 2


def _mix_group_kernel(pb_ref, pf_ref, s_ref, cbuf_ref, gn_ref, cw_ref, lmat_ref, amask_ref,
                      out_ref, sfin_ref, cfin_ref, full_scr, y_scr):
    cw = cw_ref[...]
    nb = CONV_W - 1
    for s in range(GROUP):
        r0 = SUBLANES * s
        full_scr[r0:r0 + nb, :] = cbuf_ref[s]
        full_scr[r0 + nb:r0 + nb + S_LEN, :] = pf_ref[S_LEN * s:S_LEN * (s + 1), _seg(PF_U)]
    for s in range(GROUP):
        r0 = SUBLANES * s
        f0 = full_scr[r0:r0 + S_LEN, :]
        f1 = full_scr[r0 + 1:r0 + 1 + S_LEN, :]
        f2 = full_scr[r0 + 2:r0 + 2 + S_LEN, :]
        y_scr[S_LEN * s:S_LEN * (s + 1), :] = cw[0:1] * f0 + cw[1:2] * f1 + cw[2:3] * f2
        cfin_ref[s] = full_scr[r0 + S_LEN:r0 + S_LEN + nb, :]
    out_ref[:, D_A:] = (pb_ref[:, _seg(PB_GB)].astype(F32) * y_scr[...]).astype(BF16)

    b3 = jnp.dot(lmat_ref[...], _split3(pf_ref[:, _seg(PF_LF)]),
                 preferred_element_type=F32) * LOG2E
    causal = amask_ref[...] > 0.0
    gn = gn_ref[...]
    sub = BF16_ROWS
    per_sub = sub // S_LEN
    rid = jax.lax.broadcasted_iota(jnp.int32, (sub, DK), 0)
    own = [(rid >= S_LEN * j) & (rid < S_LEN * (j + 1)) for j in range(per_sub)]
    ones_blk = jnp.ones((sub, DV), BF16)
    zeros_blk = jnp.zeros((sub, DV), BF16)
    for h in range(H_A):
        sl = slice(h * DK, (h + 1) * DK)
        col = lambda seg_i, h=h: slice(seg_i * SEG + h * DK, seg_i * SEG + (h + 1) * DK)
        b = b3[0:CHUNK, sl]
        bm = b3[CHUNK:2 * CHUNK, sl]
        bl = b3[2 * CHUNK:3 * CHUNK, sl]
        q1, k1, q2, k2 = _head_tiles(pb_ref[:, col(PB_Q)].astype(F32),
                                     pb_ref[:, col(PB_K)].astype(F32), b, bm, bl)
        vb = pb_ref[:, col(PB_V)]
        a = pl.dot(q1, k1, trans_b=True)
        a = jnp.where(causal, a, 0.0).astype(BF16)
        o1 = jnp.dot(a, vb, preferred_element_type=F32)
        decay = jnp.exp2(bl)
        for blk in range(CHUNK // sub):
            rs = slice(blk * sub, (blk + 1) * sub)
            q2b, k2b, vbb = q2[rs], k2[rs], vb[rs]
            rhs = jnp.concatenate([jnp.concatenate([vbb, zeros_blk], axis=1),
                                   jnp.concatenate([zeros_blk, ones_blk], axis=1)], axis=0)
            acc = o1[rs]
            for j in range(per_sub):
                s = blk * per_sub + j
                st = s_ref[s, h]
                qm = jnp.where(own[j], q2b, 0.0).astype(BF16)
                acc = acc + jnp.dot(qm, st.astype(BF16), preferred_element_type=F32)
                drows = _decay_rows(decay[S_LEN * s:S_LEN * s + 1, :], rid)
                lhs = jnp.concatenate([jnp.where(own[j], k2b, 0.0), drows], axis=0).astype(BF16)
                ud = pl.dot(lhs, rhs, trans_a=True)
                sfin_ref[s, h] = ud[:, DV:] * st + ud[:, :DV]
            out_ref[rs, sl] = _gated_rmsnorm(acc, gn[:, sl],
                                             pb_ref[rs, col(PB_SG)].astype(F32))


def _group_mats():
    r = np.arange(CHUNK)
    seq, pos = r // S_LEN, r % S_LEN
    same = seq[:, None] == seq[None, :]
    cum = same & (pos[None, :] <= pos[:, None])
    midm = same & (pos[None, :] <= S_MID)
    lmat = np.concatenate([cum, midm, same], axis=0).astype(np.float32)
    lmat = np.concatenate([lmat, lmat, lmat], axis=1)
    return jnp.asarray(lmat, BF16), jnp.asarray(cum.astype(np.float32))


def _mix_group(proj, nseq, s0, cbuf, gn, cw):
    steps = nseq // GROUP
    const2 = lambda i: (0, 0)
    lmat, amask = _group_mats()
    return pl.pallas_call(
        _mix_group_kernel,
        grid=(steps,),
        in_specs=[pl.BlockSpec((CHUNK, PB_W), lambda i: (i, 0)),
                  pl.BlockSpec((CHUNK, PF_W), lambda i: (i, 0)),
                  pl.BlockSpec((GROUP, H_A, DK, DV), lambda i: (i, 0, 0, 0)),
                  pl.BlockSpec((GROUP, CONV_W - 1, D_B), lambda i: (i, 0, 0)),
                  pl.BlockSpec((1, D_A), const2),
                  pl.BlockSpec((CONV_W, D_B), const2),
                  pl.BlockSpec((3 * CHUNK, 3 * CHUNK), const2),
                  pl.BlockSpec((CHUNK, CHUNK), const2)],
        out_specs=[pl.BlockSpec((CHUNK, D_MODEL), lambda i: (i, 0)),
                   pl.BlockSpec((GROUP, H_A, DK, DV), lambda i: (i, 0, 0, 0)),
                   pl.BlockSpec((GROUP, CONV_W - 1, D_B), lambda i: (i, 0, 0))],
        out_shape=[jax.ShapeDtypeStruct((nseq * S_LEN, D_MODEL), BF16),
                   jax.ShapeDtypeStruct((nseq, H_A, DK, DV), F32),
                   jax.ShapeDtypeStruct((nseq, CONV_W - 1, D_B), F32)],
        scratch_shapes=[pltpu.VMEM((SUBLANES * GROUP, D_B), F32), pltpu.VMEM((CHUNK, D_B), F32)],
        compiler_params=pltpu.CompilerParams(
            dimension_semantics=("arbitrary",), vmem_limit_bytes=VMEM_LIMIT),
        name="mix_group",
    )(*proj, s0, cbuf, gn.reshape(1, -1), cw, lmat, amask)


def _outproj_kernel(x_ref, mix_ref, wo_ref, g0_ref, b0_ref, g1_ref, b1_ref, h_ref, *xs_scr):
    if xs_scr:
        _flatten_rows(x_ref, xs_scr[0])
        x_ref = xs_scr[0]
    xn = _layernorm(x_ref[...], g0_ref[...], b0_ref[...])
    m = jnp.dot(mix_ref[...], wo_ref[:, 0:D_MODEL], preferred_element_type=F32)
    h_ref[...] = _layernorm(ALPHA * xn + m, g1_ref[...], b1_ref[...])


def _outproj(x, mix, w_o, g0, b0, g1, b1, tm):
    rows = mix.shape[0]
    tm = min(tm, rows)
    assert rows % tm == 0 and (x.ndim == 2 or rows == tm)
    row = lambda a: a.reshape(1, -1)
    const = lambda i: (0, 0)
    vec = pl.BlockSpec((1, D_MODEL), const)
    x_spec = (pl.BlockSpec((tm, D_MODEL), lambda i: (i, 0)) if x.ndim == 2 else
              pl.BlockSpec(x.shape, lambda i: (0, 0, 0), pipeline_mode=pl.Buffered(1)))
    return pl.pallas_call(
        _outproj_kernel,
        grid=(rows // tm,),
        in_specs=[x_spec,
                  pl.BlockSpec((tm, D_MODEL), lambda i: (i, 0)),
                  pl.BlockSpec((D_MODEL, D_MODEL + W_PAD), const),
                  vec, vec, vec, vec],
        out_specs=pl.BlockSpec((tm, D_MODEL), lambda i: (i, 0)),
        out_shape=jax.ShapeDtypeStruct((rows, D_MODEL), F32),
        scratch_shapes=[] if x.ndim == 2 else [pltpu.VMEM((rows, D_MODEL), F32)],
        compiler_params=pltpu.CompilerParams(
            dimension_semantics=("arbitrary",), vmem_limit_bytes=VMEM_LIMIT),
        name="outproj",
    )(x, mix, w_o, row(g0), row(b0), row(g1), row(b1))


def _ffn_kernel(h_ref, wg_ref, wu_ref, wd_ref, g2_ref, b2_ref, o_ref, hb_ref, *acc_scr):
    acc_ref = acc_scr[0] if acc_scr else o_ref
    f = pl.program_id(1)

    @pl.when(f == 0)
    def _():
        h = h_ref[...]
        hb_ref[:, 0:D_MODEL] = h.astype(BF16)
        acc_ref[...] = ALPHA * h

    hb = hb_ref[:, 0:D_MODEL]
    tf = wd_ref.shape[0]
    g = jnp.dot(hb, wg_ref[:, 0:tf], preferred_element_type=F32)
    u = jnp.dot(hb, wu_ref[:, 0:tf], preferred_element_type=F32)
    a = (_silu(g) * u).astype(BF16)
    acc_ref[...] += jnp.dot(a, wd_ref[:, 0:D_MODEL], preferred_element_type=F32)

    @pl.when(f == pl.num_programs(1) - 1)
    def _():
        acc_ref[...] = _layernorm(acc_ref[...], g2_ref[...], b2_ref[...])
        if acc_scr:
            _unflatten_rows(acc_ref, o_ref)


def _ffn(h, w_gate, w_up, w_down, g2, b2, tm, tf, seq_shape=None):
    rows = h.shape[0]
    assert rows % tm == 0 and D_FF % tf == 0
    row = lambda a: a.reshape(1, -1)
    vec = pl.BlockSpec((1, D_MODEL), lambda i, f: (0, 0))
    if seq_shape is None:
        out_spec = pl.BlockSpec((tm, D_MODEL), lambda i, f: (i, 0))
        out_shape = jax.ShapeDtypeStruct((rows, D_MODEL), F32)
        acc = []
    else:
        assert rows == tm == seq_shape[0] * seq_shape[1]
        out_spec = pl.BlockSpec((*seq_shape, D_MODEL), lambda i, f: (0, 0, 0))
        out_shape = jax.ShapeDtypeStruct((*seq_shape, D_MODEL), F32)
        acc = [pltpu.VMEM((tm, D_MODEL), F32)]
    return pl.pallas_call(
        _ffn_kernel,
        grid=(rows // tm, D_FF // tf),
        in_specs=[pl.BlockSpec((tm, D_MODEL), lambda i, f: (i, 0)),
                  pl.BlockSpec((D_MODEL, tf + W_PAD), lambda i, f: (0, f)),
                  pl.BlockSpec((D_MODEL, tf + W_PAD), lambda i, f: (0, f)),
                  pl.BlockSpec((tf, D_MODEL + W_PAD), lambda i, f: (f, 0)),
                  vec, vec],
        out_specs=out_spec,
        out_shape=out_shape,
        scratch_shapes=[pltpu.VMEM((tm, D_MODEL + W_PAD), BF16)] + acc,
        compiler_params=pltpu.CompilerParams(
            dimension_semantics=("arbitrary", "arbitrary"), vmem_limit_bytes=VMEM_LIMIT),
        name="ffn",
    )(h, w_gate, w_up, w_down, row(g2), row(b2))


def kernel(x_prompt, x_sample, state_hgrn, state_conv, meta_tokens, ln0_g, ln0_b, w_in, b_f, lb_param, gnorm_g, conv_w, w_o, ln1_g, ln1_b, w_gate, w_up, w_down, ln2_g, ln2_b):
    bp, seq, _ = x_prompt.shape
    bs, dseq, _ = x_sample.shape
    assert dseq == S_LEN and seq % CHUNK == 0 and bs % GROUP == 0

    g0, b0 = ln0_g.astype(F32), ln0_b.astype(F32)

    xp = x_prompt.reshape(bp * seq, D_MODEL)
    proj_s, w_in_b, lb = _inproj_cast(x_sample, meta_tokens.astype(F32), g0, b0, w_in[0],
                                      b_f[0], lb_param)
    proj_p, (w_o_b, wg_b, wu_b, wd_b) = _inproj(
        xp, g0, b0, w_in_b, b_f[0], lb, tm=INPROJ_TM,
        cast=((w_o[0], D_MODEL), (w_gate[0], FFN_TF), (w_up[0], FFN_TF), (w_down[0], D_MODEL)))

    zero_st = jnp.zeros((H_A, DV, DK), F32)
    zero_tail = jnp.zeros((CONV_W - 1, D_B), F32)
    _, st_meta, tail_meta = _mix_seq(
        proj_s, bs * dseq, 1, CHUNK, CHUNK, zero_st, zero_tail, gnorm_g[0], conv_w[0],
        mid=N_META // 2, valid_rows=N_META, state_transposed_out=True)

    mix_p, hgrn_p, conv_p = _mix_seq(
        proj_p, 0, bp, seq, MIX_TB, st_meta[0], tail_meta[0], gnorm_g[0], conv_w[0],
        mid=CHUNK // 2)
    mix_s, hgrn_s, conv_s = _mix_group(
        proj_s, bs, state_hgrn[0], state_conv[0], gnorm_g[0], conv_w[0])

    h_p = _outproj(xp, mix_p, w_o_b, g0, b0, ln1_g[0], ln1_b[0], tm=OUTPROJ_TM)
    h_s = _outproj(x_sample, mix_s, w_o_b, g0, b0, ln1_g[0], ln1_b[0], tm=OUTPROJ_TM)

    y_p = _ffn(h_p, wg_b, wu_b, wd_b, ln2_g[0], ln2_b[0], tm=FFN_TM, tf=FFN_TF)
    y_s = _ffn(h_s, wg_b, wu_b, wd_b, ln2_g[0], ln2_b[0], tm=FFN_TM_SMALL, tf=FFN_TF,
               seq_shape=(bs, dseq))

    return (y_p.reshape(bp, seq, D_MODEL), y_s,
            hgrn_p[None], conv_p[None], hgrn_s[None], conv_s[None])
```

```python
import functools

import numpy as np
import jax
import jax.numpy as jnp
from jax.experimental import pallas as pl
from jax.experimental.pallas import tpu as pltpu

F32 = jnp.float32
BF16 = jnp.bfloat16

D_MODEL = 2048
D_A = 1024
D_B = 1024
DK = 128
DV = 128
H_A = 8
SEG = 1024
N_SEG = 7
W_PAD = 128
W_SEG_STRIDE = SEG + W_PAD
PB_Q, PB_K, PB_V, PB_SG, PB_GB = range(5)
PF_LF, PF_U = range(2)
PB_W, PF_W = 5 * SEG, 2 * SEG
N_META = 16
CHUNK = 64
D_FF = 5632
CONV_W = 3
ALPHA = 2.0 ** 0.25
LN_EPS = 1e-5
RMS_EPS = 1e-6

V7X_VMEM_BYTES = 64 * 1024 * 1024
SUBLANES = 8
BF16_ROWS = 16
VMEM_LIMIT = V7X_VMEM_BYTES - 2 * 1024 * 1024

INPROJ_TM = 256
MIX_TB = 512
OUTPROJ_TM = 512
FFN_TM = 1024
FFN_TM_SMALL = 512
FFN_TF = 512
FFN_LN_SUB = 256

CONV_TAIL = SUBLANES - (CONV_W - 1)


def _seg(i):
    return slice(i * SEG, (i + 1) * SEG)


def _layernorm(x, g, b):
    mu = jnp.mean(x, axis=-1, keepdims=True)
    xc = x - mu
    var = jnp.mean(xc * xc, axis=-1, keepdims=True)
    return xc * jax.lax.rsqrt(var + LN_EPS) * g + b


def _silu(x):
    return x * (1.0 / (1.0 + jnp.exp(-x)))


def _flatten_rows(x3_ref, dst_ref):
    n, slen, _ = x3_ref.shape
    for b in range(n):
        dst_ref[b * slen:(b + 1) * slen, :] = x3_ref[b]


def _unflatten_rows(src_ref, y3_ref):
    n, slen, _ = y3_ref.shape
    for b in range(n):
        y3_ref[b] = src_ref[b * slen:(b + 1) * slen, :]


INPROJ_SUB = 128


def _inproj_kernel(x_ref, g0_ref, b0_ref, w_ref, bf_ref, lb_ref, *refs, n_cast):
    cast_in, refs = refs[:n_cast], refs[n_cast:]
    (pb_ref, pf_ref), cast_out = refs[:2], refs[2:]
    for src, dst in zip(cast_in, cast_out, strict=True):
        _cast_padded(src, dst)

    for r in range(x_ref.shape[0] // INPROJ_SUB):
        rs = slice(r * INPROJ_SUB, (r + 1) * INPROJ_SUB)
        xn = _layernorm(x_ref[rs, :], g0_ref[...], b0_ref[...]).astype(BF16)

        def seg(j, xn=xn):
            return jnp.dot(xn, w_ref[:, j * W_SEG_STRIDE:j * W_SEG_STRIDE + SEG],
                           preferred_element_type=F32)

        pf_ref[rs, _seg(PF_U)] = seg(5) * seg(6)
        pb_ref[rs, _seg(PB_Q)] = _silu(seg(0)).astype(BF16)

        lf, kk = _forget_gate(seg(1) + bf_ref[...], lb_ref[...])
        pf_ref[rs, _seg(PF_LF)] = lf
        pb_ref[rs, _seg(PB_K)] = kk.astype(BF16)

        pb_ref[rs, _seg(PB_SG)] = _silu(seg(3)).astype(BF16)
        pb_ref[rs, _seg(PB_GB)] = seg(4).astype(BF16)
        pb_ref[rs, _seg(PB_V)] = seg(2).astype(BF16)


def _cast_padded(src_ref, dst_ref):
    n_blk = (dst_ref.shape[1] - src_ref.shape[1]) // W_PAD
    cb = src_ref.shape[1] // n_blk
    for b in range(n_blk):
        d0 = b * (cb + W_PAD)
        dst_ref[:, d0:d0 + cb] = src_ref[:, b * cb:(b + 1) * cb].astype(BF16)
        dst_ref[:, d0 + cb:d0 + cb + W_PAD] = jnp.zeros((dst_ref.shape[0], W_PAD), BF16)


def _inproj(x, g0, b0, w_in, b_f, lb, tm, cast=()):
    rows = x.shape[0]
    assert rows % tm == 0 and tm % INPROJ_SUB == 0
    steps = rows // tm
    row = lambda a: a.reshape(1, -1)
    const = lambda i: (0, 0)
    cast_in, cast_out, cast_shapes = [], [], []
    for w, cb in cast:
        assert w.shape[0] % (BF16_ROWS * steps) == 0 and w.shape[1] % cb == 0
        cols = w.shape[1] // cb * (cb + W_PAD)
        cast_in.append(pl.BlockSpec((w.shape[0] // steps, w.shape[1]), lambda i: (i, 0)))
        cast_out.append(pl.BlockSpec((w.shape[0] // steps, cols), lambda i: (i, 0)))
        cast_shapes.append(jax.ShapeDtypeStruct((w.shape[0], cols), BF16))
    res = pl.pallas_call(
        functools.partial(_inproj_kernel, n_cast=len(cast)),
        grid=(steps,),
        in_specs=[
            pl.BlockSpec((tm, D_MODEL), lambda i: (i, 0)),
            pl.BlockSpec((1, D_MODEL), const),
            pl.BlockSpec((1, D_MODEL), const),
            pl.BlockSpec((D_MODEL, N_SEG * W_SEG_STRIDE), const, pipeline_mode=pl.Buffered(1)),
            pl.BlockSpec((1, SEG), const),
            pl.BlockSpec((1, SEG), const),
        ] + cast_in,
        out_specs=[pl.BlockSpec((tm, PB_W), lambda i: (i, 0)),
                   pl.BlockSpec((tm, PF_W), lambda i: (i, 0))] + cast_out,
        out_shape=[jax.ShapeDtypeStruct((rows, PB_W), BF16),
                   jax.ShapeDtypeStruct((rows, PF_W), F32)] + cast_shapes,
        compiler_params=pltpu.CompilerParams(
            dimension_semantics=("arbitrary",), vmem_limit_bytes=VMEM_LIMIT),
        name="inproj",
    )(x, row(g0), row(b0), w_in, row(b_f), row(lb), *(w for w, _ in cast))
    return res[:2], res[2:]


def _forget_gate(z, lb):
    e = jnp.exp(-jnp.abs(z))
    r = 1.0 / (1.0 + e)
    er = e * r
    pos = z >= 0.0
    return jnp.log(lb + (1.0 - lb) * jnp.where(pos, r, er)), (1.0 - lb) * jnp.where(pos, er, r)


def _inproj_cast_kernel(x_ref, meta_ref, g0_ref, b0_ref, w_ref, bf_ref, lbp_ref,
                        pb_ref, pf_ref, wb_ref, lb_ref, xn_scr, gc_scr, xs_scr):
    j = pl.program_id(0)

    @pl.when(j == 0)
    def _():
        n_x, n_m = xs_scr.shape[0], meta_ref.shape[0]
        _flatten_rows(x_ref, xs_scr)
        xn_scr[0:n_x, :] = _layernorm(xs_scr[...], g0_ref[...], b0_ref[...]).astype(BF16)
        xn_scr[n_x:n_x + n_m, :] = _layernorm(meta_ref[...], g0_ref[...],
                                              b0_ref[...]).astype(BF16)
        xn_scr[n_x + n_m:, :] = jnp.zeros((xn_scr.shape[0] - n_x - n_m, D_MODEL), BF16)
        p = lbp_ref[...]
        e = jnp.exp(p - jnp.max(p, axis=0, keepdims=True))
        lb_ref[...] = e[0:1, :] / jnp.sum(e, axis=0, keepdims=True)

    wb = w_ref[...].astype(BF16)
    wb_ref[:, 0:SEG] = wb
    wb_ref[:, SEG:] = jnp.zeros((D_MODEL, W_SEG_STRIDE - SEG), BF16)
    acc = jnp.dot(xn_scr[...], wb, preferred_element_type=F32)

    @pl.when(j == 0)
    def _():
        pb_ref[:, _seg(PB_Q)] = _silu(acc).astype(BF16)

    @pl.when(j == 1)
    def _():
        lf, kk = _forget_gate(acc + bf_ref[...], lb_ref[...])
        pf_ref[:, _seg(PF_LF)] = lf
        pb_ref[:, _seg(PB_K)] = kk.astype(BF16)

    @pl.when(j == 2)
    def _():
        pb_ref[:, _seg(PB_V)] = acc.astype(BF16)

    @pl.when(j == 3)
    def _():
        pb_ref[:, _seg(PB_SG)] = _silu(acc).astype(BF16)

    @pl.when(j == 4)
    def _():
        pb_ref[:, _seg(PB_GB)] = acc.astype(BF16)

    @pl.when(j == 5)
    def _():
        gc_scr[...] = acc

    @pl.when(j == 6)
    def _():
        pf_ref[:, _seg(PF_U)] = gc_scr[...] * acc


def _inproj_cast(x, meta, g0, b0, w_in, b_f, lb_param):
    n_x = x.shape[0] * x.shape[1]
    assert n_x % BF16_ROWS == 0 and meta.shape[0] % BF16_ROWS == 0
    rows = n_x + CHUNK
    row = lambda a: a.reshape(1, -1)
    const = lambda j: (0, 0)
    wseg = lambda: pl.BlockSpec((D_MODEL, SEG), lambda j: (0, j))
    res = pl.pallas_call(
        _inproj_cast_kernel,
        grid=(N_SEG,),
        in_specs=[pl.BlockSpec(x.shape, lambda j: (0, 0, 0), pipeline_mode=pl.Buffered(1)),
                  pl.BlockSpec(meta.shape, const),
                  pl.BlockSpec((1, D_MODEL), const),
                  pl.BlockSpec((1, D_MODEL), const),
                  wseg(),
                  pl.BlockSpec((1, SEG), const),
                  pl.BlockSpec(lb_param.shape, const)],
        out_specs=[pl.BlockSpec((rows, PB_W), const), pl.BlockSpec((rows, PF_W), const),
                   pl.BlockSpec((D_MODEL, W_SEG_STRIDE), lambda j: (0, j)),
                   pl.BlockSpec((1, SEG), const)],
        out_shape=[jax.ShapeDtypeStruct((rows, PB_W), BF16),
                   jax.ShapeDtypeStruct((rows, PF_W), F32),
                   jax.ShapeDtypeStruct((D_MODEL, N_SEG * W_SEG_STRIDE), BF16),
                   jax.ShapeDtypeStruct((1, SEG), F32)],
        scratch_shapes=[pltpu.VMEM((rows, D_MODEL), BF16), pltpu.VMEM((rows, SEG), F32),
                        pltpu.VMEM((n_x, D_MODEL), F32)],
        compiler_params=pltpu.CompilerParams(
            dimension_semantics=("arbitrary",), vmem_limit_bytes=VMEM_LIMIT),
        name="inproj_cast",
    )(x, meta, row(g0), row(b0), w_in, row(b_f), lb_param.astype(F32))
    return res[:2], res[2], res[3]


def _split3(x):
    hi = x.astype(BF16)
    r1 = x - hi.astype(F32)
    mid = r1.astype(BF16)
    lo = (r1 - mid.astype(F32)).astype(BF16)
    return jnp.concatenate([hi, mid, lo], axis=0)


def _cumsum_rows(x):
    rows, lanes = x.shape
    n = rows // SUBLANES
    x3 = x.reshape(n, SUBLANES, lanes)
    pos = jax.lax.broadcasted_iota(jnp.int32, x3.shape, 1)
    s = 1
    while s < SUBLANES:
        x3 = x3 + jnp.where(pos >= s, pltpu.roll(x3, s, 1), 0.0)
        s *= 2
    carry = jnp.zeros((1, lanes), F32)
    out = []
    for g in range(n):
        out.append(x3[g] + carry)
        carry = carry + x3[g, SUBLANES - 1:SUBLANES, :]
    return jnp.concatenate(out, axis=0)


LOG2E = 1.4426950408889634


def _head_tiles(qt, kk, b, bm, bl):
    q1 = (qt * jnp.exp2(b - bm)).astype(BF16)
    k1 = (kk * jnp.exp2(bm - b)).astype(BF16)
    q2 = qt * jnp.exp2(b)
    k2 = kk * jnp.exp2(bl - b)
    return q1, k1, q2, k2


DECAY_ROWS = BF16_ROWS


def _decay_rows(d, rid):
    d_hi = d.astype(BF16).astype(F32)
    d_r = d - d_hi
    d_mid = d_r.astype(BF16).astype(F32)
    return jnp.where(rid == 0, d_hi, jnp.where(rid == 1, d_mid,
                     jnp.where(rid == 2, d_r - d_mid, 0.0)))


def _gated_rmsnorm(o, gn, sg):
    ms = jnp.mean(o * o, axis=-1, keepdims=True)
    return (o * jax.lax.rsqrt(ms + RMS_EPS) * gn * sg).astype(BF16)


def _mix_seq_kernel(pb_ref, pf_ref, st0_ref, tail0_ref, gn_ref, cw_ref, amask_ref,
                    out_ref, sfin_ref, cfin_ref, st_scr, ubuf,
                    *, tb, mid, valid_rows, state_transposed_out):
    t = pl.program_id(1)

    @pl.when(t == 0)
    def _():
        st_scr[...] = st0_ref[...]
        ubuf[CONV_TAIL:SUBLANES, :] = tail0_ref[...]

    u = pf_ref[:, _seg(PF_U)]
    ubuf[SUBLANES:SUBLANES + tb, :] = u
    cw = cw_ref[...]
    conv = (cw[0:1] * ubuf[CONV_TAIL:CONV_TAIL + tb, :]
            + cw[1:2] * ubuf[CONV_TAIL + 1:CONV_TAIL + 1 + tb, :] + cw[2:3] * u)
    out_ref[:, D_A:] = (pb_ref[:, _seg(PB_GB)].astype(F32) * conv).astype(BF16)
    last = valid_rows if valid_rows is not None else tb
    ubuf[CONV_TAIL:SUBLANES, :] = ubuf[CONV_TAIL + last:SUBLANES + last, :]

    causal = amask_ref[...] > 0.0
    gn = gn_ref[...]
    if valid_rows is not None:
        rowmask = jax.lax.broadcasted_iota(jnp.int32, (CHUNK, SEG), 0) < valid_rows
        rowmask_h = jax.lax.broadcasted_iota(jnp.int32, (CHUNK, DK), 0) < valid_rows
    state = [st_scr[h] for h in range(H_A)]
    for ci in range(tb // CHUNK):
        rows = slice(ci * CHUNK, (ci + 1) * CHUNK)
        lf = pf_ref[rows, _seg(PF_LF)]
        if valid_rows is not None:
            lf = jnp.where(rowmask, lf, 0.0)
        b_all = _cumsum_rows(lf) * LOG2E
        for h in range(H_A):
            sl = slice(h * DK, (h + 1) * DK)
            col = lambda seg_i, h=h: slice(seg_i * SEG + h * DK, seg_i * SEG + (h + 1) * DK)
            b = b_all[:, sl]
            bm = b[mid:mid + 1, :]
            bl = b[CHUNK - 1:CHUNK, :]
            kk = pb_ref[rows, col(PB_K)].astype(F32)
            if valid_rows is not None:
                kk = jnp.where(rowmask_h, kk, 0.0)
            q1, k1, q2, k2 = _head_tiles(pb_ref[rows, col(PB_Q)].astype(F32), kk, b, bm, bl)
            vb = pb_ref[rows, col(PB_V)]
            st = state[h]
            a = pl.dot(q1, k1, trans_b=True)
            a = jnp.where(causal, a, 0.0).astype(BF16)
            o = (jnp.dot(a, vb, preferred_element_type=F32)
                 + jnp.dot(q2.astype(BF16), st.T.astype(BF16), preferred_element_type=F32))
            state[h] = st * jnp.exp2(bl) + pl.dot(vb, k2.astype(BF16), trans_a=True)
            out_ref[rows, sl] = _gated_rmsnorm(o, gn[:, sl],
                                               pb_ref[rows, col(PB_SG)].astype(F32))
    for h in range(H_A):
        st_scr[h] = state[h]

    @pl.when(t == pl.num_programs(1) - 1)
    def _():
        for h in range(H_A):
            sfin_ref[0, h] = st_scr[h] if state_transposed_out else st_scr[h].T
        cfin_ref[0] = ubuf[CONV_TAIL:SUBLANES, :]


def _mix_seq(proj, row0, nseq, seq_len, tb, st0, tail0, gn, cw, *, mid, valid_rows=None,
             state_transposed_out=False):
    nt = seq_len // tb
    assert seq_len % tb == 0 and row0 % tb == 0 and tb % CHUNK == 0
    rb0 = row0 // tb
    rowblk = lambda b, t: (rb0 + b * nt + t, 0)
    const2 = lambda b, t: (0, 0)
    tril = np.tril(np.ones((CHUNK, CHUNK), np.float32))
    kern = functools.partial(_mix_seq_kernel, tb=tb, mid=mid, valid_rows=valid_rows,
                             state_transposed_out=state_transposed_out)
    return pl.pallas_call(
        kern,
        grid=(nseq, nt),
        in_specs=[pl.BlockSpec((tb, PB_W), rowblk), pl.BlockSpec((tb, PF_W), rowblk),
                  pl.BlockSpec((H_A, DV, DK), lambda b, t: (0, 0, 0)),
                  pl.BlockSpec((CONV_W - 1, D_B), const2),
                  pl.BlockSpec((1, D_A), const2),
                  pl.BlockSpec((CONV_W, D_B), const2),
                  pl.BlockSpec((CHUNK, CHUNK), const2)],
        out_specs=[pl.BlockSpec((tb, D_MODEL), lambda b, t: (b * nt + t, 0)),
                   pl.BlockSpec((1, H_A, DK, DV), lambda b, t: (b, 0, 0, 0)),
                   pl.BlockSpec((1, CONV_W - 1, D_B), lambda b, t: (b, 0, 0))],
        out_shape=[jax.ShapeDtypeStruct((nseq * seq_len, D_MODEL), BF16),
                   jax.ShapeDtypeStruct((nseq, H_A, DK, DV), F32),
                   jax.ShapeDtypeStruct((nseq, CONV_W - 1, D_B), F32)],
        scratch_shapes=[pltpu.VMEM((H_A, DV, DK), F32), pltpu.VMEM((SUBLANES + tb, D_B), F32)],
        compiler_params=pltpu.CompilerParams(
            dimension_semantics=("arbitrary", "arbitrary"), vmem_limit_bytes=VMEM_LIMIT),
        name="mix_seq",
    )(*proj, st0, tail0, gn.reshape(1, -1), cw, jnp.asarray(tril))


GROUP = 16
S_LEN = 4
S_MID = 2


def _mix_group_kernel(pb_ref, pf_ref, s_ref, cbuf_ref, gn_ref, cw_ref, lmat_ref, amask_ref,
                      out_ref, sfin_ref, cfin_ref, full_scr, y_scr):
    cw = cw_ref[...]
    nb = CONV_W - 1
    for s in range(GROUP):
        r0 = SUBLANES * s
        full_scr[r0:r0 + nb, :] = cbuf_ref[s]
        full_scr[r0 + nb:r0 + nb + S_LEN, :] = pf_ref[S_LEN * s:S_LEN * (s + 1), _seg(PF_U)]
    for s in range(GROUP):
        r0 = SUBLANES * s
        f0 = full_scr[r0:r0 + S_LEN, :]
        f1 = full_scr[r0 + 1:r0 + 1 + S_LEN, :]
        f2 = full_scr[r0 + 2:r0 + 2 + S_LEN, :]
        y_scr[S_LEN * s:S_LEN * (s + 1), :] = cw[0:1] * f0 + cw[1:2] * f1 + cw[2:3] * f2
        cfin_ref[s] = full_scr[r0 + S_LEN:r0 + S_LEN + nb, :]
    out_ref[:, D_A:] = (pb_ref[:, _seg(PB_GB)].astype(F32) * y_scr[...]).astype(BF16)

    b3 = jnp.dot(lmat_ref[...], _split3(pf_ref[:, _seg(PF_LF)]),
                 preferred_element_type=F32) * LOG2E
    causal = amask_ref[...] > 0.0
    gn = gn_ref[...]
    sub = BF16_ROWS
    per_sub = sub // S_LEN
    rid = jax.lax.broadcasted_iota(jnp.int32, (sub, DK), 0)
    own = [(rid >= S_LEN * j) & (rid < S_LEN * (j + 1)) for j in range(per_sub)]
    ones_blk = jnp.ones((sub, DV), BF16)
    zeros_blk = jnp.zeros((sub, DV), BF16)
    for h in range(H_A):
        sl = slice(h * DK, (h + 1) * DK)
        col = lambda seg_i, h=h: slice(seg_i * SEG + h * DK, seg_i * SEG + (h + 1) * DK)
        b = b3[0:CHUNK, sl]
        bm = b3[CHUNK:2 * CHUNK, sl]
        bl = b3[2 * CHUNK:3 * CHUNK, sl]
        q1, k1, q2, k2 = _head_tiles(pb_ref[:, col(PB_Q)].astype(F32),
                                     pb_ref[:, col(PB_K)].astype(F32), b, bm, bl)
        vb = pb_ref[:, col(PB_V)]
        a = pl.dot(q1, k1, trans_b=True)
        a = jnp.where(causal, a, 0.0).astype(BF16)
        o1 = jnp.dot(a, vb, preferred_element_type=F32)
        decay = jnp.exp2(bl)
        for blk in range(CHUNK // sub):
            rs = slice(blk * sub, (blk + 1) * sub)
            q2b, k2b, vbb = q2[rs], k2[rs], vb[rs]
            rhs = jnp.concatenate([jnp.concatenate([vbb, zeros_blk], axis=1),
                                   jnp.concatenate([zeros_blk, ones_blk], axis=1)], axis=0)
            acc = o1[rs]
            for j in range(per_sub):
                s = blk * per_sub + j
                st = s_ref[s, h]
                qm = jnp.where(own[j], q2b, 0.0).astype(BF16)
                acc = acc + jnp.dot(qm, st.astype(BF16), preferred_element_type=F32)
                drows = _decay_rows(decay[S_LEN * s:S_LEN * s + 1, :], rid)
                lhs = jnp.concatenate([jnp.where(own[j], k2b, 0.0), drows], axis=0).astype(BF16)
                ud = pl.dot(lhs, rhs, trans_a=True)
                sfin_ref[s, h] = ud[:, DV:] * st + ud[:, :DV]
            out_ref[rs, sl] = _gated_rmsnorm(acc, gn[:, sl],
                                             pb_ref[rs, col(PB_SG)].astype(F32))


def _group_mats():
    r = np.arange(CHUNK)
    seq, pos = r // S_LEN, r % S_LEN
    same = seq[:, None] == seq[None, :]
    cum = same & (pos[None, :] <= pos[:, None])
    midm = same & (pos[None, :] <= S_MID)
    lmat = np.concatenate([cum, midm, same], axis=0).astype(np.float32)
    lmat = np.concatenate([lmat, lmat, lmat], axis=1)
    return jnp.asarray(lmat, BF16), jnp.asarray(cum.astype(np.float32))


def _mix_group(proj, nseq, s0, cbuf, gn, cw):
    steps = nseq // GROUP
    const2 = lambda i: (0, 0)
    lmat, amask = _group_mats()
    return pl.pallas_call(
        _mix_group_kernel,
        grid=(steps,),
        in_specs=[pl.BlockSpec((CHUNK, PB_W), lambda i: (i, 0)),
                  pl.BlockSpec((CHUNK, PF_W), lambda i: (i, 0)),
                  pl.BlockSpec((GROUP, H_A, DK, DV), lambda i: (i, 0, 0, 0)),
                  pl.BlockSpec((GROUP, CONV_W - 1, D_B), lambda i: (i, 0, 0)),
                  pl.BlockSpec((1, D_A), const2),
                  pl.BlockSpec((CONV_W, D_B), const2),
                  pl.BlockSpec((3 * CHUNK, 3 * CHUNK), const2),
                  pl.BlockSpec((CHUNK, CHUNK), const2)],
        out_specs=[pl.BlockSpec((CHUNK, D_MODEL), lambda i: (i, 0)),
                   pl.BlockSpec((GROUP, H_A, DK, DV), lambda i: (i, 0, 0, 0)),
                   pl.BlockSpec((GROUP, CONV_W - 1, D_B), lambda i: (i, 0, 0))],
        out_shape=[jax.ShapeDtypeStruct((nseq * S_LEN, D_MODEL), BF16),
                   jax.ShapeDtypeStruct((nseq, H_A, DK, DV), F32),
                   jax.ShapeDtypeStruct((nseq, CONV_W - 1, D_B), F32)],
        scratch_shapes=[pltpu.VMEM((SUBLANES * GROUP, D_B), F32), pltpu.VMEM((CHUNK, D_B), F32)],
        compiler_params=pltpu.CompilerParams(
            dimension_semantics=("arbitrary",), vmem_limit_bytes=VMEM_LIMIT),
        name="mix_group",
    )(*proj, s0, cbuf, gn.reshape(1, -1), cw, lmat, amask)


def _outproj_kernel(x_ref, mix_ref, wo_ref, g0_ref, b0_ref, g1_ref, b1_ref, h_ref, *xs_scr):
    if xs_scr:
        _flatten_rows(x_ref, xs_scr[0])
        x_ref = xs_scr[0]
    xn = _layernorm(x_ref[...], g0_ref[...], b0_ref[...])
    m = jnp.dot(mix_ref[...], wo_ref[:, 0:D_MODEL], preferred_element_type=F32)
    h_ref[...] = _layernorm(ALPHA * xn + m, g1_ref[...], b1_ref[...])


def _outproj(x, mix, w_o, g0, b0, g1, b1, tm):
    rows = mix.shape[0]
    tm = min(tm, rows)
    assert rows % tm == 0 and (x.ndim == 2 or rows == tm)
    row = lambda a: a.reshape(1, -1)
    const = lambda i: (0, 0)
    vec = pl.BlockSpec((1, D_MODEL), const)
    x_spec = (pl.BlockSpec((tm, D_MODEL), lambda i: (i, 0)) if x.ndim == 2 else
              pl.BlockSpec(x.shape, lambda i: (0, 0, 0), pipeline_mode=pl.Buffered(1)))
    return pl.pallas_call(
        _outproj_kernel,
        grid=(rows // tm,),
        in_specs=[x_spec,
                  pl.BlockSpec((tm, D_MODEL), lambda i: (i, 0)),
                  pl.BlockSpec((D_MODEL, D_MODEL + W_PAD), const),
                  vec, vec, vec, vec],
        out_specs=pl.BlockSpec((tm, D_MODEL), lambda i: (i, 0)),
        out_shape=jax.ShapeDtypeStruct((rows, D_MODEL), F32),
        scratch_shapes=[] if x.ndim == 2 else [pltpu.VMEM((rows, D_MODEL), F32)],
        compiler_params=pltpu.CompilerParams(
            dimension_semantics=("arbitrary",), vmem_limit_bytes=VMEM_LIMIT),
        name="outproj",
    )(x, mix, w_o, row(g0), row(b0), row(g1), row(b1))


def _ffn_kernel(h_ref, wg_ref, wu_ref, wd_ref, g2_ref, b2_ref, o_ref, hb_ref, *acc_scr):
    acc_ref = acc_scr[0] if acc_scr else o_ref
    f = pl.program_id(1)

    @pl.when(f == 0)
    def _():
        h = h_ref[...]
        hb_ref[...] = h.astype(BF16)
        acc_ref[...] = ALPHA * h

    def hidden():
        hb = hb_ref[...]
        tf = wd_ref.shape[0]
        g = jnp.dot(hb, wg_ref[:, 0:tf], preferred_element_type=F32)
        u = jnp.dot(hb, wu_ref[:, 0:tf], preferred_element_type=F32)
        return (_silu(g) * u).astype(BF16)

    last = pl.num_programs(1) - 1

    @pl.when(f != last)
    def _():
        acc_ref[...] += jnp.dot(hidden(), wd_ref[:, 0:D_MODEL], preferred_element_type=F32)

    @pl.when(f == last)
    def _():
        a = hidden()
        sub = min(FFN_LN_SUB, acc_ref.shape[0])
        for r in range(acc_ref.shape[0] // sub):
            rs = slice(r * sub, (r + 1) * sub)
            d = jnp.dot(a[rs, :], wd_ref[:, 0:D_MODEL], preferred_element_type=F32)
            acc_ref[rs, :] = _layernorm(acc_ref[rs, :] + d, g2_ref[...], b2_ref[...])
        if acc_scr:
            _unflatten_rows(acc_ref, o_ref)


def _ffn(h, w_gate, w_up, w_down, g2, b2, tm, tf, seq_shape=None):
    rows = h.shape[0]
    assert rows % tm == 0 and D_FF % tf == 0
    row = lambda a: a.reshape(1, -1)
    vec = pl.BlockSpec((1, D_MODEL), lambda i, f: (0, 0))
    if seq_shape is None:
        out_spec = pl.BlockSpec((tm, D_MODEL), lambda i, f: (i, 0))
        out_shape = jax.ShapeDtypeStruct((rows, D_MODEL), F32)
        acc = []
    else:
        assert rows == tm == seq_shape[0] * seq_shape[1]
        out_spec = pl.BlockSpec((*seq_shape, D_MODEL), lambda i, f: (0, 0, 0))
        out_shape = jax.ShapeDtypeStruct((*seq_shape, D_MODEL), F32)
        acc = [pltpu.VMEM((tm, D_MODEL), F32)]
    return pl.pallas_call(
        _ffn_kernel,
        grid=(rows // tm, D_FF // tf),
        in_specs=[pl.BlockSpec((tm, D_MODEL), lambda i, f: (i, 0)),
                  pl.BlockSpec((D_MODEL, tf + W_PAD), lambda i, f: (0, f)),
                  pl.BlockSpec((D_MODEL, tf + W_PAD), lambda i, f: (0, f)),
                  pl.BlockSpec((tf, D_MODEL + W_PAD), lambda i, f: (f, 0)),
                  vec, vec],
        out_specs=out_spec,
        out_shape=out_shape,
        scratch_shapes=[pltpu.VMEM((tm, D_MODEL), BF16)] + acc,
        compiler_params=pltpu.CompilerParams(
            dimension_semantics=("arbitrary", "arbitrary"), vmem_limit_bytes=VMEM_LIMIT),
        name="ffn",
    )(h, w_gate, w_up, w_down, row(g2), row(b2))


def kernel(x_prompt, x_sample, state_hgrn, state_conv, meta_tokens, ln0_g, ln0_b, w_in, b_f, lb_param, gnorm_g, conv_w, w_o, ln1_g, ln1_b, w_gate, w_up, w_down, ln2_g, ln2_b):
    bp, seq, _ = x_prompt.shape
    bs, dseq, _ = x_sample.shape
    assert dseq == S_LEN and seq % CHUNK == 0 and bs % GROUP == 0

    g0, b0 = ln0_g.astype(F32), ln0_b.astype(F32)

    xp = x_prompt.reshape(bp * seq, D_MODEL)
    proj_s, w_in_b, lb = _inproj_cast(x_sample, meta_tokens.astype(F32), g0, b0, w_in[0],
                                      b_f[0], lb_param)
    proj_p, (w_o_b, wg_b, wu_b, wd_b) = _inproj(
        xp, g0, b0, w_in_b, b_f[0], lb, tm=INPROJ_TM,
        cast=((w_o[0], D_MODEL), (w_gate[0], FFN_TF), (w_up[0], FFN_TF), (w_down[0], D_MODEL)))

    zero_st = jnp.zeros((H_A, DV, DK), F32)
    zero_tail = jnp.zeros((CONV_W - 1, D_B), F32)
    _, st_meta, tail_meta = _mix_seq(
        proj_s, bs * dseq, 1, CHUNK, CHUNK, zero_st, zero_tail, gnorm_g[0], conv_w[0],
        mid=N_META // 2, valid_rows=N_META, state_transposed_out=True)

    mix_p, hgrn_p, conv_p = _mix_seq(
        proj_p, 0, bp, seq, MIX_TB, st_meta[0], tail_meta[0], gnorm_g[0], conv_w[0],
        mid=CHUNK // 2)
    mix_s, hgrn_s, conv_s = _mix_group(
        proj_s, bs, state_hgrn[0], state_conv[0], gnorm_g[0], conv_w[0])

    h_p = _outproj(xp, mix_p, w_o_b, g0, b0, ln1_g[0], ln1_b[0], tm=OUTPROJ_TM)
    h_s = _outproj(x_sample, mix_s, w_o_b, g0, b0, ln1_g[0], ln1_b[0], tm=OUTPROJ_TM)

    y_p = _ffn(h_p, wg_b, wu_b, wd_b, ln2_g[0], ln2_b[0], tm=FFN_TM, tf=FFN_TF)
    y_s = _ffn(h_s, wg_b, wu_b, wd_b, ln2_g[0], ln2_b[0], tm=FFN_TM_SMALL, tf=FFN_TF,
               seq_shape=(bs, dseq))

    return (y_p.reshape(bp, seq, D_MODEL), y_s,
            hgrn_p[None], conv_p[None], hgrn_s[None], conv_s[None])
```

```python
import functools

import numpy as np
import jax
import jax.numpy as jnp
from jax.experimental import pallas as pl
from jax.experimental.pallas import tpu as pltpu

F32 = jnp.float32
BF16 = jnp.bfloat16

D_MODEL = 2048
D_A = 1024
D_B = 1024
DK = 128
DV = 128
H_A = 8
SEG = 1024
N_SEG = 7
W_PAD = 128
W_SEG_STRIDE = SEG + W_PAD
PB_Q, PB_K, PB_V, PB_SG, PB_GB = range(5)
PF_LF, PF_U = range(2)
PB_W, PF_W = 5 * SEG, 2 * SEG
N_META = 16
CHUNK = 64
D_FF = 5632
CONV_W = 3
ALPHA = 2.0 ** 0.25
LN_EPS = 1e-5
RMS_EPS = 1e-6

V7X_VMEM_BYTES = 64 * 1024 * 1024
SUBLANES = 8
BF16_ROWS = 16
VMEM_LIMIT = V7X_VMEM_BYTES - 2 * 1024 * 1024

INPROJ_TM = 256
MIX_TB = 512
OUTPROJ_TM = 512
FFN_TM = 1024
FFN_TM_SMALL = 512
FFN_TF = 512

CONV_TAIL = SUBLANES - (CONV_W - 1)


def _seg(i):
    return slice(i * SEG, (i + 1) * SEG)


def _layernorm(x, g, b):
    mu = jnp.mean(x, axis=-1, keepdims=True)
    xc = x - mu
    var = jnp.mean(xc * xc, axis=-1, keepdims=True)
    return xc * jax.lax.rsqrt(var + LN_EPS) * g + b


def _silu(x):
    return x * (1.0 / (1.0 + jnp.exp(-x)))


def _flatten_rows(x3_ref, dst_ref):
    n, slen, _ = x3_ref.shape
    for b in range(n):
        dst_ref[b * slen:(b + 1) * slen, :] = x3_ref[b]


def _unflatten_rows(src_ref, y3_ref):
    n, slen, _ = y3_ref.shape
    for b in range(n):
        y3_ref[b] = src_ref[b * slen:(b + 1) * slen, :]


INPROJ_SUB = 128


INPROJ_SEG_ORDER = (5, 6, 0, 1, 3, 4, 2)


def _inproj_kernel(x_ref, g0_ref, b0_ref, w_hbm, bf_ref, lb_ref, *refs, n_cast):
    cast_in, refs = refs[:n_cast], refs[n_cast:]
    (pb_ref, pf_ref), cast_out, (w_ref, sem) = refs[:2], refs[2:-2], refs[-2:]
    for src, dst in zip(cast_in, cast_out, strict=True):
        _cast_padded(src, dst)

    def w_copy(j):
        cols = slice(j * W_SEG_STRIDE, (j + 1) * W_SEG_STRIDE)
        return pltpu.make_async_copy(w_hbm.at[:, cols], w_ref.at[:, cols], sem.at[j])

    def tile(first):
        if first:
            for j in INPROJ_SEG_ORDER:
                w_copy(j).start()
        for r in range(x_ref.shape[0] // INPROJ_SUB):
            rs = slice(r * INPROJ_SUB, (r + 1) * INPROJ_SUB)
            xn = _layernorm(x_ref[rs, :], g0_ref[...], b0_ref[...]).astype(BF16)

            def seg(j, xn=xn, wait=first and r == 0):
                if wait:
                    w_copy(j).wait()
                return jnp.dot(xn, w_ref[:, j * W_SEG_STRIDE:j * W_SEG_STRIDE + SEG],
                               preferred_element_type=F32)

            pf_ref[rs, _seg(PF_U)] = seg(5) * seg(6)
            pb_ref[rs, _seg(PB_Q)] = _silu(seg(0)).astype(BF16)

            lf, kk = _forget_gate(seg(1) + bf_ref[...], lb_ref[...])
            pf_ref[rs, _seg(PF_LF)] = lf
            pb_ref[rs, _seg(PB_K)] = kk.astype(BF16)

            pb_ref[rs, _seg(PB_SG)] = _silu(seg(3)).astype(BF16)
            pb_ref[rs, _seg(PB_GB)] = seg(4).astype(BF16)
            pb_ref[rs, _seg(PB_V)] = seg(2).astype(BF16)

    pl.when(pl.program_id(0) == 0)(functools.partial(tile, True))
    pl.when(pl.program_id(0) != 0)(functools.partial(tile, False))


def _cast_padded(src_ref, dst_ref):
    n_blk = (dst_ref.shape[1] - src_ref.shape[1]) // W_PAD
    cb = src_ref.shape[1] // n_blk
    for b in range(n_blk):
        d0 = b * (cb + W_PAD)
        dst_ref[:, d0:d0 + cb] = src_ref[:, b * cb:(b + 1) * cb].astype(BF16)
        dst_ref[:, d0 + cb:d0 + cb + W_PAD] = jnp.zeros((dst_ref.shape[0], W_PAD), BF16)


def _inproj(x, g0, b0, w_in, b_f, lb, tm, cast=()):
    rows = x.shape[0]
    assert rows % tm == 0 and tm % INPROJ_SUB == 0
    steps = rows // tm
    row = lambda a: a.reshape(1, -1)
    const = lambda i: (0, 0)
    cast_in, cast_out, cast_shapes = [], [], []
    for w, cb in cast:
        assert w.shape[0] % (BF16_ROWS * steps) == 0 and w.shape[1] % cb == 0
        cols = w.shape[1] // cb * (cb + W_PAD)
        cast_in.append(pl.BlockSpec((w.shape[0] // steps, w.shape[1]), lambda i: (i, 0)))
        cast_out.append(pl.BlockSpec((w.shape[0] // steps, cols), lambda i: (i, 0)))
        cast_shapes.append(jax.ShapeDtypeStruct((w.shape[0], cols), BF16))
    res = pl.pallas_call(
        functools.partial(_inproj_kernel, n_cast=len(cast)),
        grid=(steps,),
        in_specs=[
            pl.BlockSpec((tm, D_MODEL), lambda i: (i, 0)),
            pl.BlockSpec((1, D_MODEL), const),
            pl.BlockSpec((1, D_MODEL), const),
            pl.BlockSpec(memory_space=pl.ANY),
            pl.BlockSpec((1, SEG), const),
            pl.BlockSpec((1, SEG), const),
        ] + cast_in,
        out_specs=[pl.BlockSpec((tm, PB_W), lambda i: (i, 0)),
                   pl.BlockSpec((tm, PF_W), lambda i: (i, 0))] + cast_out,
        out_shape=[jax.ShapeDtypeStruct((rows, PB_W), BF16),
                   jax.ShapeDtypeStruct((rows, PF_W), F32)] + cast_shapes,
        scratch_shapes=[pltpu.VMEM((D_MODEL, N_SEG * W_SEG_STRIDE), BF16),
                        pltpu.SemaphoreType.DMA((N_SEG,))],
        compiler_params=pltpu.CompilerParams(
            dimension_semantics=("arbitrary",), vmem_limit_bytes=VMEM_LIMIT),
        name="inproj",
    )(x, row(g0), row(b0), w_in, row(b_f), row(lb), *(w for w, _ in cast))
    return res[:2], res[2:]


def _forget_gate(z, lb):
    e = jnp.exp(-jnp.abs(z))
    r = 1.0 / (1.0 + e)
    er = e * r
    pos = z >= 0.0
    return jnp.log(lb + (1.0 - lb) * jnp.where(pos, r, er)), (1.0 - lb) * jnp.where(pos, er, r)


def _inproj_cast_kernel(x_ref, meta_ref, g0_ref, b0_ref, w_ref, bf_ref, lbp_ref,
                        pb_ref, pf_ref, wb_ref, lb_ref, xn_scr, gc_scr, xs_scr):
    j = pl.program_id(0)

    @pl.when(j == 0)
    def _():
        n_x, n_m = xs_scr.shape[0], meta_ref.shape[0]
        _flatten_rows(x_ref, xs_scr)
        xn_scr[0:n_x, :] = _layernorm(xs_scr[...], g0_ref[...], b0_ref[...]).astype(BF16)
        xn_scr[n_x:n_x + n_m, :] = _layernorm(meta_ref[...], g0_ref[...],
                                              b0_ref[...]).astype(BF16)
        xn_scr[n_x + n_m:, :] = jnp.zeros((xn_scr.shape[0] - n_x - n_m, D_MODEL), BF16)
        p = lbp_ref[...]
        e = jnp.exp(p - jnp.max(p, axis=0, keepdims=True))
        lb_ref[...] = e[0:1, :] / jnp.sum(e, axis=0, keepdims=True)

    wb = w_ref[...].astype(BF16)
    wb_ref[:, 0:SEG] = wb
    wb_ref[:, SEG:] = jnp.zeros((D_MODEL, W_SEG_STRIDE - SEG), BF16)
    acc = jnp.dot(xn_scr[...], wb, preferred_element_type=F32)

    @pl.when(j == 0)
    def _():
        pb_ref[:, _seg(PB_Q)] = _silu(acc).astype(BF16)

    @pl.when(j == 1)
    def _():
        lf, kk = _forget_gate(acc + bf_ref[...], lb_ref[...])
        pf_ref[:, _seg(PF_LF)] = lf
        pb_ref[:, _seg(PB_K)] = kk.astype(BF16)

    @pl.when(j == 2)
    def _():
        pb_ref[:, _seg(PB_V)] = acc.astype(BF16)

    @pl.when(j == 3)
    def _():
        pb_ref[:, _seg(PB_SG)] = _silu(acc).astype(BF16)

    @pl.when(j == 4)
    def _():
        pb_ref[:, _seg(PB_GB)] = acc.astype(BF16)

    @pl.when(j == 5)
    def _():
        gc_scr[...] = acc

    @pl.when(j == 6)
    def _():
        pf_ref[:, _seg(PF_U)] = gc_scr[...] * acc


def _inproj_cast(x, meta, g0, b0, w_in, b_f, lb_param):
    n_x = x.shape[0] * x.shape[1]
    assert n_x % BF16_ROWS == 0 and meta.shape[0] % BF16_ROWS == 0
    rows = n_x + CHUNK
    row = lambda a: a.reshape(1, -1)
    const = lambda j: (0, 0)
    wseg = lambda: pl.BlockSpec((D_MODEL, SEG), lambda j: (0, j))
    res = pl.pallas_call(
        _inproj_cast_kernel,
        grid=(N_SEG,),
        in_specs=[pl.BlockSpec(x.shape, lambda j: (0, 0, 0), pipeline_mode=pl.Buffered(1)),
                  pl.BlockSpec(meta.shape, const),
                  pl.BlockSpec((1, D_MODEL), const),
                  pl.BlockSpec((1, D_MODEL), const),
                  wseg(),
                  pl.BlockSpec((1, SEG), const),
                  pl.BlockSpec(lb_param.shape, const)],
        out_specs=[pl.BlockSpec((rows, PB_W), const), pl.BlockSpec((rows, PF_W), const),
                   pl.BlockSpec((D_MODEL, W_SEG_STRIDE), lambda j: (0, j)),
                   pl.BlockSpec((1, SEG), const)],
        out_shape=[jax.ShapeDtypeStruct((rows, PB_W), BF16),
                   jax.ShapeDtypeStruct((rows, PF_W), F32),
                   jax.ShapeDtypeStruct((D_MODEL, N_SEG * W_SEG_STRIDE), BF16),
                   jax.ShapeDtypeStruct((1, SEG), F32)],
        scratch_shapes=[pltpu.VMEM((rows, D_MODEL), BF16), pltpu.VMEM((rows, SEG), F32),
                        pltpu.VMEM((n_x, D_MODEL), F32)],
        compiler_params=pltpu.CompilerParams(
            dimension_semantics=("arbitrary",), vmem_limit_bytes=VMEM_LIMIT),
        name="inproj_cast",
    )(x, meta, row(g0), row(b0), w_in, row(b_f), lb_param.astype(F32))
    return res[:2], res[2], res[3]


def _split3(x):
    hi = x.astype(BF16)
    r1 = x - hi.astype(F32)
    mid = r1.astype(BF16)
    lo = (r1 - mid.astype(F32)).astype(BF16)
    return jnp.concatenate([hi, mid, lo], axis=0)


def _cumsum_rows(x):
    rows, lanes = x.shape
    n = rows // SUBLANES
    x3 = x.reshape(n, SUBLANES, lanes)
    pos = jax.lax.broadcasted_iota(jnp.int32, x3.shape, 1)
    s = 1
    while s < SUBLANES:
        x3 = x3 + jnp.where(pos >= s, pltpu.roll(x3, s, 1), 0.0)
        s *= 2
    carry = jnp.zeros((1, lanes), F32)
    out = []
    for g in range(n):
        out.append(x3[g] + carry)
        carry = carry + x3[g, SUBLANES - 1:SUBLANES, :]
    return jnp.concatenate(out, axis=0)


LOG2E = 1.4426950408889634


def _head_tiles(qt, kk, b, bm, bl):
    q1 = (qt * jnp.exp2(b - bm)).astype(BF16)
    k1 = (kk * jnp.exp2(bm - b)).astype(BF16)
    q2 = qt * jnp.exp2(b)
    k2 = kk * jnp.exp2(bl - b)
    return q1, k1, q2, k2


DECAY_ROWS = BF16_ROWS


def _decay_rows(d, rid):
    d_hi = d.astype(BF16).astype(F32)
    d_r = d - d_hi
    d_mid = d_r.astype(BF16).astype(F32)
    return jnp.where(rid == 0, d_hi, jnp.where(rid == 1, d_mid,
                     jnp.where(rid == 2, d_r - d_mid, 0.0)))


def _gated_rmsnorm(o, gn, sg):
    ms = jnp.mean(o * o, axis=-1, keepdims=True)
    return (o * jax.lax.rsqrt(ms + RMS_EPS) * gn * sg).astype(BF16)


def _mix_seq_kernel(pb_ref, pf_ref, st0_ref, tail0_ref, gn_ref, cw_ref, amask_ref,
                    out_ref, sfin_ref, cfin_ref, st_scr, ubuf,
                    *, tb, mid, valid_rows, state_transposed_out):
    t = pl.program_id(1)

    @pl.when(t == 0)
    def _():
        st_scr[...] = st0_ref[...]
        ubuf[CONV_TAIL:SUBLANES, :] = tail0_ref[...]

    u = pf_ref[:, _seg(PF_U)]
    ubuf[SUBLANES:SUBLANES + tb, :] = u
    cw = cw_ref[...]
    conv = (cw[0:1] * ubuf[CONV_TAIL:CONV_TAIL + tb, :]
            + cw[1:2] * ubuf[CONV_TAIL + 1:CONV_TAIL + 1 + tb, :] + cw[2:3] * u)
    out_ref[:, D_A:] = (pb_ref[:, _seg(PB_GB)].astype(F32) * conv).astype(BF16)
    last = valid_rows if valid_rows is not None else tb
    ubuf[CONV_TAIL:SUBLANES, :] = ubuf[CONV_TAIL + last:SUBLANES + last, :]

    causal = amask_ref[...] > 0.0
    gn = gn_ref[...]
    if valid_rows is not None:
        rowmask = jax.lax.broadcasted_iota(jnp.int32, (CHUNK, SEG), 0) < valid_rows
        rowmask_h = jax.lax.broadcasted_iota(jnp.int32, (CHUNK, DK), 0) < valid_rows
    state = [st_scr[h] for h in range(H_A)]
    for ci in range(tb // CHUNK):
        rows = slice(ci * CHUNK, (ci + 1) * CHUNK)
        lf = pf_ref[rows, _seg(PF_LF)]
        if valid_rows is not None:
            lf = jnp.where(rowmask, lf, 0.0)
        b_all = _cumsum_rows(lf) * LOG2E
        for h in range(H_A):
            sl = slice(h * DK, (h + 1) * DK)
            col = lambda seg_i, h=h: slice(seg_i * SEG + h * DK, seg_i * SEG + (h + 1) * DK)
            b = b_all[:, sl]
            bm = b[mid:mid + 1, :]
            bl = b[CHUNK - 1:CHUNK, :]
            kk = pb_ref[rows, col(PB_K)].astype(F32)
            if valid_rows is not None:
                kk = jnp.where(rowmask_h, kk, 0.0)
            q1, k1, q2, k2 = _head_tiles(pb_ref[rows, col(PB_Q)].astype(F32), kk, b, bm, bl)
            vb = pb_ref[rows, col(PB_V)]
            st = state[h]
            a = pl.dot(q1, k1, trans_b=True)
            a = jnp.where(causal, a, 0.0).astype(BF16)
            o = (jnp.dot(a, vb, preferred_element_type=F32)
                 + jnp.dot(q2.astype(BF16), st.T.astype(BF16), preferred_element_type=F32))
            state[h] = st * jnp.exp2(bl) + pl.dot(vb, k2.astype(BF16), trans_a=True)
            out_ref[rows, sl] = _gated_rmsnorm(o, gn[:, sl],
                                               pb_ref[rows, col(PB_SG)].astype(F32))
    for h in range(H_A):
        st_scr[h] = state[h]

    @pl.when(t == pl.num_programs(1) - 1)
    def _():
        for h in range(H_A):
            sfin_ref[0, h] = st_scr[h] if state_transposed_out else st_scr[h].T
        cfin_ref[0] = ubuf[CONV_TAIL:SUBLANES, :]


def _mix_seq(proj, row0, nseq, seq_len, tb, st0, tail0, gn, cw, *, mid, valid_rows=None,
             state_transposed_out=False):
    nt = seq_len // tb
    assert seq_len % tb == 0 and row0 % tb == 0 and tb % CHUNK == 0
    rb0 = row0 // tb
    rowblk = lambda b, t: (rb0 + b * nt + t, 0)
    const2 = lambda b, t: (0, 0)
    tril = np.tril(np.ones((CHUNK, CHUNK), np.float32))
    kern = functools.partial(_mix_seq_kernel, tb=tb, mid=mid, valid_rows=valid_rows,
                             state_transposed_out=state_transposed_out)
    return pl.pallas_call(
        kern,
        grid=(nseq, nt),
        in_specs=[pl.BlockSpec((tb, PB_W), rowblk), pl.BlockSpec((tb, PF_W), rowblk),
                  pl.BlockSpec((H_A, DV, DK), lambda b, t: (0, 0, 0)),
                  pl.BlockSpec((CONV_W - 1, D_B), const2),
                  pl.BlockSpec((1, D_A), const2),
                  pl.BlockSpec((CONV_W, D_B), const2),
                  pl.BlockSpec((CHUNK, CHUNK), const2)],
        out_specs=[pl.BlockSpec((tb, D_MODEL), lambda b, t: (b * nt + t, 0)),
                   pl.BlockSpec((1, H_A, DK, DV), lambda b, t: (b, 0, 0, 0)),
                   pl.BlockSpec((1, CONV_W - 1, D_B), lambda b, t: (b, 0, 0))],
        out_shape=[jax.ShapeDtypeStruct((nseq * seq_len, D_MODEL), BF16),
                   jax.ShapeDtypeStruct((nseq, H_A, DK, DV), F32),
                   jax.ShapeDtypeStruct((nseq, CONV_W - 1, D_B), F32)],
        scratch_shapes=[pltpu.VMEM((H_A, DV, DK), F32), pltpu.VMEM((SUBLANES + tb, D_B), F32)],
        compiler_params=pltpu.CompilerParams(
            dimension_semantics=("arbitrary", "arbitrary"), vmem_limit_bytes=VMEM_LIMIT),
        name="mix_seq",
    )(*proj, st0, tail0, gn.reshape(1, -1), cw, jnp.asarray(tril))


GROUP = 16
S_LEN = 4
S_MID = 2


def _mix_group_kernel(pb_ref, pf_ref, s_ref, cbuf_ref, gn_ref, cw_ref, lmat_ref, amask_ref,
                      out_ref, sfin_ref, cfin_ref, full_scr, y_scr):
    cw = cw_ref[...]
    nb = CONV_W - 1
    for s in range(GROUP):
        r0 = SUBLANES * s
        full_scr[r0:r0 + nb, :] = cbuf_ref[s]
        full_scr[r0 + nb:r0 + nb + S_LEN, :] = pf_ref[S_LEN * s:S_LEN * (s + 1), _seg(PF_U)]
    for s in range(GROUP):
        r0 = SUBLANES * s
        f0 = full_scr[r0:r0 + S_LEN, :]
        f1 = full_scr[r0 + 1:r0 + 1 + S_LEN, :]
        f2 = full_scr[r0 + 2:r0 + 2 + S_LEN, :]
        y_scr[S_LEN * s:S_LEN * (s + 1), :] = cw[0:1] * f0 + cw[1:2] * f1 + cw[2:3] * f2
        cfin_ref[s] = full_scr[r0 + S_LEN:r0 + S_LEN + nb, :]
    out_ref[:, D_A:] = (pb_ref[:, _seg(PB_GB)].astype(F32) * y_scr[...]).astype(BF16)

    b3 = jnp.dot(lmat_ref[...], _split3(pf_ref[:, _seg(PF_LF)]),
                 preferred_element_type=F32) * LOG2E
    causal = amask_ref[...] > 0.0
    gn = gn_ref[...]
    sub = BF16_ROWS
    per_sub = sub // S_LEN
    rid = jax.lax.broadcasted_iota(jnp.int32, (sub, DK), 0)
    own = [(rid >= S_LEN * j) & (rid < S_LEN * (j + 1)) for j in range(per_sub)]
    ones_blk = jnp.ones((sub, DV), BF16)
    zeros_blk = jnp.zeros((sub, DV), BF16)
    for h in range(H_A):
        sl = slice(h * DK, (h + 1) * DK)
        col = lambda seg_i, h=h: slice(seg_i * SEG + h * DK, seg_i * SEG + (h + 1) * DK)
        b = b3[0:CHUNK, sl]
        bm = b3[CHUNK:2 * CHUNK, sl]
        bl = b3[2 * CHUNK:3 * CHUNK, sl]
        q1, k1, q2, k2 = _head_tiles(pb_ref[:, col(PB_Q)].astype(F32),
                                     pb_ref[:, col(PB_K)].astype(F32), b, bm, bl)
        vb = pb_ref[:, col(PB_V)]
        a = pl.dot(q1, k1, trans_b=True)
        a = jnp.where(causal, a, 0.0).astype(BF16)
        o1 = jnp.dot(a, vb, preferred_element_type=F32)
        decay = jnp.exp2(bl)
        for blk in range(CHUNK // sub):
            rs = slice(blk * sub, (blk + 1) * sub)
            q2b, k2b, vbb = q2[rs], k2[rs], vb[rs]
            rhs = jnp.concatenate([jnp.concatenate([vbb, zeros_blk], axis=1),
                                   jnp.concatenate([zeros_blk, ones_blk], axis=1)], axis=0)
            acc = o1[rs]
            for j in range(per_sub):
                s = blk * per_sub + j
                st = s_ref[s, h]
                qm = jnp.where(own[j], q2b, 0.0).astype(BF16)
                acc = acc + jnp.dot(qm, st.astype(BF16), preferred_element_type=F32)
                drows = _decay_rows(decay[S_LEN * s:S_LEN * s + 1, :], rid)
                lhs = jnp.concatenate([jnp.where(own[j], k2b, 0.0), drows], axis=0).astype(BF16)
                ud = pl.dot(lhs, rhs, trans_a=True)
                sfin_ref[s, h] = ud[:, DV:] * st + ud[:, :DV]
            out_ref[rs, sl] = _gated_rmsnorm(acc, gn[:, sl],
                                             pb_ref[rs, col(PB_SG)].astype(F32))


def _group_mats():
    r = np.arange(CHUNK)
    seq, pos = r // S_LEN, r % S_LEN
    same = seq[:, None] == seq[None, :]
    cum = same & (pos[None, :] <= pos[:, None])
    midm = same & (pos[None, :] <= S_MID)
    lmat = np.concatenate([cum, midm, same], axis=0).astype(np.float32)
    lmat = np.concatenate([lmat, lmat, lmat], axis=1)
    return jnp.asarray(lmat, BF16), jnp.asarray(cum.astype(np.float32))


def _mix_group(proj, nseq, s0, cbuf, gn, cw):
    steps = nseq // GROUP
    const2 = lambda i: (0, 0)
    lmat, amask = _group_mats()
    return pl.pallas_call(
        _mix_group_kernel,
        grid=(steps,),
        in_specs=[pl.BlockSpec((CHUNK, PB_W), lambda i: (i, 0)),
                  pl.BlockSpec((CHUNK, PF_W), lambda i: (i, 0)),
                  pl.BlockSpec((GROUP, H_A, DK, DV), lambda i: (i, 0, 0, 0)),
                  pl.BlockSpec((GROUP, CONV_W - 1, D_B), lambda i: (i, 0, 0)),
                  pl.BlockSpec((1, D_A), const2),
                  pl.BlockSpec((CONV_W, D_B), const2),
                  pl.BlockSpec((3 * CHUNK, 3 * CHUNK), const2),
                  pl.BlockSpec((CHUNK, CHUNK), const2)],
        out_specs=[pl.BlockSpec((CHUNK, D_MODEL), lambda i: (i, 0)),
                   pl.BlockSpec((GROUP, H_A, DK, DV), lambda i: (i, 0, 0, 0)),
                   pl.BlockSpec((GROUP, CONV_W - 1, D_B), lambda i: (i, 0, 0))],
        out_shape=[jax.ShapeDtypeStruct((nseq * S_LEN, D_MODEL), BF16),
                   jax.ShapeDtypeStruct((nseq, H_A, DK, DV), F32),
                   jax.ShapeDtypeStruct((nseq, CONV_W - 1, D_B), F32)],
        scratch_shapes=[pltpu.VMEM((SUBLANES * GROUP, D_B), F32), pltpu.VMEM((CHUNK, D_B), F32)],
        compiler_params=pltpu.CompilerParams(
            dimension_semantics=("arbitrary",), vmem_limit_bytes=VMEM_LIMIT),
        name="mix_group",
    )(*proj, s0, cbuf, gn.reshape(1, -1), cw, lmat, amask)


def _outproj_kernel(x_ref, mix_ref, wo_ref, g0_ref, b0_ref, g1_ref, b1_ref, h_ref, *xs_scr):
    if xs_scr:
        _flatten_rows(x_ref, xs_scr[0])
        x_ref = xs_scr[0]
    xn = _layernorm(x_ref[...], g0_ref[...], b0_ref[...])
    m = jnp.dot(mix_ref[...], wo_ref[:, 0:D_MODEL], preferred_element_type=F32)
    h_ref[...] = _layernorm(ALPHA * xn + m, g1_ref[...], b1_ref[...])


def _outproj(x, mix, w_o, g0, b0, g1, b1, tm):
    rows = mix.shape[0]
    tm = min(tm, rows)
    assert rows % tm == 0 and (x.ndim == 2 or rows == tm)
    row = lambda a: a.reshape(1, -1)
    const = lambda i: (0, 0)
    vec = pl.BlockSpec((1, D_MODEL), const)
    x_spec = (pl.BlockSpec((tm, D_MODEL), lambda i: (i, 0)) if x.ndim == 2 else
              pl.BlockSpec(x.shape, lambda i: (0, 0, 0), pipeline_mode=pl.Buffered(1)))
    return pl.pallas_call(
        _outproj_kernel,
        grid=(rows // tm,),
        in_specs=[x_spec,
                  pl.BlockSpec((tm, D_MODEL), lambda i: (i, 0)),
                  pl.BlockSpec((D_MODEL, D_MODEL + W_PAD), const),
                  vec, vec, vec, vec],
        out_specs=pl.BlockSpec((tm, D_MODEL), lambda i: (i, 0)),
        out_shape=jax.ShapeDtypeStruct((rows, D_MODEL), F32),
        scratch_shapes=[] if x.ndim == 2 else [pltpu.VMEM((rows, D_MODEL), F32)],
        compiler_params=pltpu.CompilerParams(
            dimension_semantics=("arbitrary",), vmem_limit_bytes=VMEM_LIMIT),
        name="outproj",
    )(x, mix, w_o, row(g0), row(b0), row(g1), row(b1))


def _ffn_kernel(h_ref, wg_ref, wu_ref, wd_ref, g2_ref, b2_ref, o_ref, hb_ref, *acc_scr):
    acc_ref = acc_scr[0] if acc_scr else o_ref
    f = pl.program_id(1)

    @pl.when(f == 0)
    def _():
        h = h_ref[...]
        hb_ref[...] = h.astype(BF16)
        acc_ref[...] = ALPHA * h

    hb = hb_ref[...]
    tf = wd_ref.shape[0]
    g = jnp.dot(hb, wg_ref[:, 0:tf], preferred_element_type=F32)
    u = jnp.dot(hb, wu_ref[:, 0:tf], preferred_element_type=F32)
    a = (_silu(g) * u).astype(BF16)
    acc_ref[...] += jnp.dot(a, wd_ref[:, 0:D_MODEL], preferred_element_type=F32)

    @pl.when(f == pl.num_programs(1) - 1)
    def _():
        acc_ref[...] = _layernorm(acc_ref[...], g2_ref[...], b2_ref[...])
        if acc_scr:
            _unflatten_rows(acc_ref, o_ref)


def _ffn(h, w_gate, w_up, w_down, g2, b2, tm, tf, seq_shape=None):
    rows = h.shape[0]
    assert rows % tm == 0 and D_FF % tf == 0
    row = lambda a: a.reshape(1, -1)
    vec = pl.BlockSpec((1, D_MODEL), lambda i, f: (0, 0))
    if seq_shape is None:
        out_spec = pl.BlockSpec((tm, D_MODEL), lambda i, f: (i, 0))
        out_shape = jax.ShapeDtypeStruct((rows, D_MODEL), F32)
        acc = []
    else:
        assert rows == tm == seq_shape[0] * seq_shape[1]
        out_spec = pl.BlockSpec((*seq_shape, D_MODEL), lambda i, f: (0, 0, 0))
        out_shape = jax.ShapeDtypeStruct((*seq_shape, D_MODEL), F32)
        acc = [pltpu.VMEM((tm, D_MODEL), F32)]
    return pl.pallas_call(
        _ffn_kernel,
        grid=(rows // tm, D_FF // tf),
        in_specs=[pl.BlockSpec((tm, D_MODEL), lambda i, f: (i, 0)),
                  pl.BlockSpec((D_MODEL, tf + W_PAD), lambda i, f: (0, f)),
                  pl.BlockSpec((D_MODEL, tf + W_PAD), lambda i, f: (0, f)),
                  pl.BlockSpec((tf, D_MODEL + W_PAD), lambda i, f: (f, 0)),
                  vec, vec],
        out_specs=out_spec,
        out_shape=out_shape,
        scratch_shapes=[pltpu.VMEM((tm, D_MODEL), BF16)] + acc,
        compiler_params=pltpu.CompilerParams(
            dimension_semantics=("arbitrary", "arbitrary"), vmem_limit_bytes=VMEM_LIMIT),
        name="ffn",
    )(h, w_gate, w_up, w_down, row(g2), row(b2))


def kernel(x_prompt, x_sample, state_hgrn, state_conv, meta_tokens, ln0_g, ln0_b, w_in, b_f, lb_param, gnorm_g, conv_w, w_o, ln1_g, ln1_b, w_gate, w_up, w_down, ln2_g, ln2_b):
    bp, seq, _ = x_prompt.shape
    bs, dseq, _ = x_sample.shape
    assert dseq == S_LEN and seq % CHUNK == 0 and bs % GROUP == 0

    g0, b0 = ln0_g.astype(F32), ln0_b.astype(F32)

    xp = x_prompt.reshape(bp * seq, D_MODEL)
    proj_s, w_in_b, lb = _inproj_cast(x_sample, meta_tokens.astype(F32), g0, b0, w_in[0],
                                      b_f[0], lb_param)
    proj_p, (w_o_b, wg_b, wu_b, wd_b) = _inproj(
        xp, g0, b0, w_in_b, b_f[0], lb, tm=INPROJ_TM,
        cast=((w_o[0], D_MODEL), (w_gate[0], FFN_TF), (w_up[0], FFN_TF), (w_down[0], D_MODEL)))

    zero_st = jnp.zeros((H_A, DV, DK), F32)
    zero_tail = jnp.zeros((CONV_W - 1, D_B), F32)
    _, st_meta, tail_meta = _mix_seq(
        proj_s, bs * dseq, 1, CHUNK, CHUNK, zero_st, zero_tail, gnorm_g[0], conv_w[0],
        mid=N_META // 2, valid_rows=N_META, state_transposed_out=True)

    mix_p, hgrn_p, conv_p = _mix_seq(
        proj_p, 0, bp, seq, MIX_TB, st_meta[0], tail_meta[0], gnorm_g[0], conv_w[0],
        mid=CHUNK // 2)
    mix_s, hgrn_s, conv_s = _mix_group(
        proj_s, bs, state_hgrn[0], state_conv[0], gnorm_g[0], conv_w[0])

    h_p = _outproj(xp, mix_p, w_o_b, g0, b0, ln1_g[0], ln1_b[0], tm=OUTPROJ_TM)
    h_s = _outproj(x_sample, mix_s, w_o_b, g0, b0, ln1_g[0], ln1_b[0], tm=OUTPROJ_TM)

    y_p = _ffn(h_p, wg_b, wu_b, wd_b, ln2_g[0], ln2_b[0], tm=FFN_TM, tf=FFN_TF)
    y_s = _ffn(h_s, wg_b, wu_b, wd_b, ln2_g[0], ln2_b[0], tm=FFN_TM_SMALL, tf=FFN_TF,
               seq_shape=(bs, dseq))

    return (y_p.reshape(bp, seq, D_MODEL), y_s,
            hgrn_p[None], conv_p[None], hgrn_s[None], conv_s[None])
```
